```python
import jax, jax.numpy as jnp
from jax import lax
import numpy as np

D_MODEL = 2048
BATCH = 2
SEQ = 8192
DEPTH = 1

GRID_W = 64
CTX_LEN = 256
N_MOD = 6
CONV_W = 4
PAD_L = (CONV_W - 1) // 2
PAD_R = CONV_W // 2
NH_A = 16
W_A = -(-4 * D_MODEL // (3 * 256)) * 256
BW_A = W_A // NH_A
LRU_C = 8.0
NK = 16
DK = 128
NV = 32
DV = 128
QK_DIM = NK * DK
V_DIM = NV * DV
QKV_DIM = 2 * QK_DIM + V_DIM
CHUNK = 64
F_HIDDEN = -(-8 * D_MODEL // (3 * 256)) * 256
IN_SIZES = (W_A, W_A, QKV_DIM, V_DIM, 2 * NV, 2 * NV, 2 * D_MODEL)
IN_COLS = sum(IN_SIZES)
IN_SPLIT = [sum(IN_SIZES[:i + 1]) for i in range(len(IN_SIZES) - 1)]
EPS = 1e-6

kernel_name = "hybrid_rglru_gdeltanet_dit_block"


def rms_norm(x, w):
    xf = x.astype(jnp.float32)
    y = xf * lax.rsqrt(jnp.mean(xf * xf, axis=-1, keepdims=True) + EPS)
    return (y * w.astype(jnp.float32)).astype(x.dtype)


def modulate(h, shift, scale):
    return h * (1 + scale) + shift


def raster_to_column(t, rows):
    b = t.shape[0]
    return t.reshape((b, rows, GRID_W) + t.shape[2:]).swapaxes(1, 2).reshape(t.shape)


def column_to_raster(t, rows):
    b = t.shape[0]
    return t.reshape((b, GRID_W, rows) + t.shape[2:]).swapaxes(1, 2).reshape(t.shape)


def depthwise_conv(x, w):
    return lax.conv_general_dilated(
        x, w[:, None, :].astype(x.dtype), window_strides=(1,), padding=[(PAD_L, PAD_R)],
        dimension_numbers=("NWC", "WIO", "NWC"), feature_group_count=x.shape[-1])


def rg_lru_coeffs(xa, gate_w, gate_b, lam):
    b, l, _ = xa.shape
    gates = jnp.einsum("blhi,hij->blhj", xa.reshape(b, l, NH_A, BW_A), gate_w.astype(jnp.float32))
    gates = gates + gate_b.astype(jnp.float32)
    r = jax.nn.sigmoid(gates[..., :BW_A]).reshape(b, l, W_A)
    i = jax.nn.sigmoid(gates[..., BW_A:]).reshape(b, l, W_A)
    log_a = -LRU_C * r * jax.nn.softplus(-lam.astype(jnp.float32))
    a = jnp.exp(log_a)
    return a, jnp.sqrt(-jnp.expm1(2 * log_a)) * (i * xa)


def linear_scan(a, bx, h0):
    def combine(left, right):
        a_l, b_l = left
        a_r, b_r = right
        return a_l * a_r, a_r * b_l + b_r
    a_cum, b_cum = lax.associative_scan(combine, (a, bx), axis=1)
    h = a_cum * h0[:, None, :] + b_cum
    return h, h[:, -1]


def rglru_mixer(xa_c, xa_l, conv_w, conv_b, gate_w, gate_b, lam):
    xc = (depthwise_conv(xa_c, conv_w) + conv_b.astype(xa_c.dtype)).astype(jnp.float32)
    xl = (depthwise_conv(xa_l, conv_w) + conv_b.astype(xa_l.dtype)).astype(jnp.float32)
    h0 = jnp.zeros((xc.shape[0], W_A), jnp.float32)
    outs_c, outs_l = [], []
    for d in range(2):
        a_c, b_c = rg_lru_coeffs(xc, gate_w[d], gate_b[d], lam[d])
        a_l, b_l = rg_lru_coeffs(xl, gate_w[d], gate_b[d], lam[d])
        if d == 1:
            a_c, b_c, a_l, b_l = [jnp.flip(t, axis=1) for t in (a_c, b_c, a_l, b_l)]
        h_c, s_c = linear_scan(a_c, b_c, h0)
        h_l, _ = linear_scan(a_l, b_l, s_c)
        if d == 1:
            h_c, h_l = jnp.flip(h_c, axis=1), jnp.flip(h_l, axis=1)
        outs_c.append(h_c)
        outs_l.append(h_l)
    return outs_l[0] + outs_l[1], outs_c[0] + outs_c[1]


def l2norm(t):
    return t * lax.rsqrt(jnp.sum(t * t, axis=-1, keepdims=True) + EPS)


def gdn_qkv(qkv_raw, conv_w):
    y = jax.nn.silu(depthwise_conv(qkv_raw, conv_w).astype(jnp.float32))
    b, l = y.shape[:2]
    q, k, v = jnp.split(y, [QK_DIM, 2 * QK_DIM], axis=-1)
    q = jnp.repeat(l2norm(q.reshape(b, l, NK, DK)), NV // NK, axis=2) * (DK ** -0.5)
    k = jnp.repeat(l2norm(k.reshape(b, l, NK, DK)), NV // NK, axis=2)
    return q, k, v.reshape(b, l, NV, DV)


def gdn_gates(alpha, beta, d, a_log, dt_bias):
    b, l = alpha.shape[:2]
    al = alpha.astype(jnp.float32).reshape(b, l, 2, NV)[:, :, d]
    bt = beta.astype(jnp.float32).reshape(b, l, 2, NV)[:, :, d]
    g = -jnp.exp(a_log[d].astype(jnp.float32)) * jax.nn.softplus(al + dt_bias[d].astype(jnp.float32))
    return g, jax.nn.sigmoid(bt)


def gated_delta_chunked(q, k, v, g, beta, h0):
    b, l, h, dk = q.shape
    dv = v.shape[-1]
    n = l // CHUNK

    def chunks(t):
        t = t.reshape((b, n, CHUNK) + t.shape[2:])
        return jnp.moveaxis(jnp.moveaxis(t, 3, 2), 1, 0)

    qc, kc, vc, gc, bc = [chunks(t) for t in (q, k, v, g, beta)]
    gc = jnp.cumsum(gc, axis=-1)
    idx = jnp.arange(CHUNK)
    incl = idx[:, None] >= idx[None, :]
    strict = idx[:, None] > idx[None, :]
    diff = gc[..., :, None] - gc[..., None, :]
    decay = jnp.where(incl, jnp.exp(jnp.where(incl, diff, 0.0)), 0.0)
    kb = kc * bc[..., None]
    lower = jnp.where(strict, jnp.einsum("nbhid,nbhjd->nbhij", kb, kc) * decay, 0.0)
    tri = lower + jnp.eye(CHUNK, dtype=lower.dtype)
    rhs = jnp.concatenate([vc * bc[..., None], kb * jnp.exp(gc)[..., None]], axis=-1)
    sol = lax.linalg.triangular_solve(tri, rhs, left_side=True, lower=True, unit_diagonal=True)
    u, w = sol[..., :dv], sol[..., dv:]
    qk = jnp.einsum("nbhid,nbhjd->nbhij", qc, kc) * decay
    q_dec = qc * jnp.exp(gc)[..., None]
    g_last = gc[..., -1]
    k_dec = kc * jnp.exp(g_last[..., None] - gc)[..., None]

    def step(state, inp):
        u_n, w_n, qd_n, qk_n, kd_n, gl_n = inp
        v_new = u_n - jnp.einsum("bhck,bhkv->bhcv", w_n, state)
        o_n = jnp.einsum("bhck,bhkv->bhcv", qd_n, state) + jnp.einsum("bhij,bhjv->bhiv", qk_n, v_new)
        state = state * jnp.exp(gl_n)[..., None, None] + jnp.einsum("bhck,bhcv->bhkv", kd_n, v_new)
        return state, o_n

    s_final, o = lax.scan(step, h0, (u, w, q_dec, qk, k_dec, g_last))
    o = jnp.moveaxis(jnp.moveaxis(o, 0, 1), 2, 3).reshape(b, l, h, dv)
    return o, s_final


def gdn_mixer(qkv_c, qkv_l, beta_c, beta_l, alpha_c, alpha_l, conv_w, a_log, dt_bias):
    q_c, k_c, v_c = gdn_qkv(qkv_c, conv_w)
    q_l, k_l, v_l = gdn_qkv(qkv_l, conv_w)
    s0 = jnp.zeros((qkv_c.shape[0], NV, DK, DV), jnp.float32)
    outs_c, outs_l = [], []
    for d in range(2):
        g_c, b_c = gdn_gates(alpha_c, beta_c, d, a_log, dt_bias)
        g_l, b_l = gdn_gates(alpha_l, beta_l, d, a_log, dt_bias)
        seq_c = (q_c, k_c, v_c, g_c, b_c)
        seq_l = (q_l, k_l, v_l, g_l, b_l)
        if d == 1:
            seq_c = [jnp.flip(t, axis=1) for t in seq_c]
            seq_l = [jnp.flip(t, axis=1) for t in seq_l]
        o_c, s_c = gated_delta_chunked(*seq_c, s0)
        o_l, _ = gated_delta_chunked(*seq_l, s_c)
        if d == 1:
            o_c, o_l = jnp.flip(o_c, axis=1), jnp.flip(o_l, axis=1)
        outs_c.append(o_c)
        outs_l.append(o_l)
    return outs_l[0] + outs_l[1], outs_c[0] + outs_c[1]


def gated_rms_norm(o, z, w):
    zf = z.astype(jnp.float32).reshape(o.shape)
    y = o * lax.rsqrt(jnp.mean(o * o, axis=-1, keepdims=True) + EPS)
    y = y * w.astype(jnp.float32) * jax.nn.silu(zf)
    return y.reshape(o.shape[:2] + (-1,))


def merge_branches(y_rnn, a_gate, y_gdn, z, merge_logits, w_a_out_l, w_b_out_l, norm_w_l, w_out_l):
    dtype = a_gate.dtype
    ya = (y_rnn * jax.nn.gelu(a_gate.astype(jnp.float32))).astype(dtype) @ w_a_out_l
    yb = gated_rms_norm(y_gdn, z, norm_w_l).astype(dtype) @ w_b_out_l
    ga, gb = jnp.split(jax.nn.sigmoid(merge_logits), 2, axis=-1)
    return (ga * ya + gb * yb) @ w_out_l


def swiglu(h, w_in_l, w_out_l):
    gate, up = jnp.split(h @ w_in_l, 2, axis=-1)
    return (jax.nn.silu(gate) * up) @ w_out_l


def setup_inputs(seed: int = 0) -> dict:
    key = jax.random.key(seed)
    ks = jax.random.split(key, 26)
    f32 = jnp.float32

    def nrm(k, shape, fan_in):
        return jax.random.normal(k, shape, f32) * (fan_in ** -0.5)

    def noise(k, shape, s):
        return jax.random.normal(k, shape, f32) * s

    L = DEPTH
    u = jax.random.uniform(ks[14], (L, 2, W_A), f32, minval=0.9, maxval=0.999)
    a0 = u ** (1.0 / LRU_C)
    lru_lambda = jnp.log(a0) - jnp.log1p(-a0)
    return {
        "x": jax.random.normal(ks[0], (BATCH, SEQ, D_MODEL), f32),
        "c": jax.random.normal(ks[1], (BATCH, D_MODEL), f32),
        "ctx": jax.random.normal(ks[2], (BATCH, CTX_LEN, D_MODEL), f32),
        "c_ctx": jax.random.normal(ks[3], (D_MODEL,), f32),
        "w_mod": nrm(ks[4], (L, D_MODEL, N_MOD * D_MODEL), D_MODEL) * 0.5,
        "b_mod": noise(ks[5], (L, N_MOD * D_MODEL), 0.02),
        "ln1_w": 1.0 + noise(ks[6], (L, D_MODEL), 0.02),
        "ln2_w": 1.0 + noise(ks[7], (L, D_MODEL), 0.02),
        "w_in": nrm(ks[8], (L, D_MODEL, IN_COLS), D_MODEL),
        "conv_a_w": nrm(ks[9], (L, CONV_W, W_A), CONV_W),
        "conv_a_b": noise(ks[10], (L, W_A), 0.02),
        "lru_gate_w": nrm(ks[11], (L, 2, NH_A, BW_A, 2 * BW_A), BW_A),
        "lru_gate_b": noise(ks[12], (L, 2, NH_A, 2 * BW_A), 0.02),
        "lru_lambda": lru_lambda,
        "conv_qkv_w": nrm(ks[13], (L, CONV_W, QKV_DIM), CONV_W),
        "gdn_a_log": jnp.log(jax.random.uniform(ks[15], (L, 2, NV), f32, minval=1.0, maxval=16.0)),
        "gdn_dt_bias": 1.0 + noise(ks[16], (L, 2, NV), 0.1),
        "gdn_norm_w": 1.0 + noise(ks[17], (L, DV), 0.02),
        "w_a_out": nrm(ks[18], (L, W_A, D_MODEL), W_A),
        "w_b_out": nrm(ks[19], (L, V_DIM, D_MODEL), V_DIM),
        "w_out": nrm(ks[20], (L, D_MODEL, D_MODEL), D_MODEL),
        "w_ffn_in": nrm(ks[21], (L, D_MODEL, 2 * F_HIDDEN), D_MODEL),
        "w_ffn_out": nrm(ks[22], (L, F_HIDDEN, D_MODEL), F_HIDDEN),
        "final_norm_w": 1.0 + noise(ks[23], (D_MODEL,), 0.02),
    }


def reference(x, c, ctx, c_ctx, w_mod, b_mod, ln1_w, ln2_w, w_in, conv_a_w, conv_a_b,
              lru_gate_w, lru_gate_b, lru_lambda, conv_qkv_w, gdn_a_log, gdn_dt_bias,
              gdn_norm_w, w_a_out, w_b_out, w_out, w_ffn_in, w_ffn_out, final_norm_w):
    ROWS = x.shape[1] // GRID_W
    silu_c = jax.nn.silu(c)
    silu_cc = jax.nn.silu(c_ctx)
    for l in range(DEPTH):
        update_ctx = l < DEPTH - 1
        mod_x = (silu_c @ w_mod[l] + b_mod[l])[:, None, :]
        mod_c = (silu_cc @ w_mod[l] + b_mod[l])[None, None, :]
        sh1, sc1, gt1, sh2, sc2, gt2 = jnp.split(mod_x, N_MOD, axis=-1)
        csh1, csc1, cgt1, csh2, csc2, cgt2 = jnp.split(mod_c, N_MOD, axis=-1)

        hx = modulate(rms_norm(x, ln1_w[l]), sh1, sc1)
        hc = modulate(rms_norm(ctx, ln1_w[l]), csh1, csc1)
        ax, ag, qkv, z, bt, al, mg = jnp.split(hx @ w_in[l], IN_SPLIT, axis=-1)
        cax, cag, cqkv, cz, cbt, cal, cmg = jnp.split(hc @ w_in[l], IN_SPLIT, axis=-1)

        ya_l, ya_c = rglru_mixer(cax, ax, conv_a_w[l], conv_a_b[l], lru_gate_w[l],
                                 lru_gate_b[l], lru_lambda[l])
        yb_col, yb_c = gdn_mixer(cqkv, raster_to_column(qkv, ROWS), cbt, raster_to_column(bt, ROWS),
                                 cal, raster_to_column(al, ROWS), conv_qkv_w[l], gdn_a_log[l],
                                 gdn_dt_bias[l])
        yb_l = column_to_raster(yb_col, ROWS)

        x_mix = merge_branches(ya_l, ag, yb_l, z, mg, w_a_out[l], w_b_out[l], gdn_norm_w[l], w_out[l])
        if update_ctx:
            c_mix = merge_branches(ya_c, cag, yb_c, cz, cmg, w_a_out[l], w_b_out[l], gdn_norm_w[l], w_out[l])
            ctx = ctx + cgt1 * c_mix
        x = x + gt1 * x_mix

        x = x + gt2 * swiglu(modulate(rms_norm(x, ln2_w[l]), sh2, sc2), w_ffn_in[l], w_ffn_out[l])
        if update_ctx:
            ctx = ctx + cgt2 * swiglu(modulate(rms_norm(ctx, ln2_w[l]), csh2, csc2), w_ffn_in[l], w_ffn_out[l])
    return rms_norm(x, final_norm_w)
```

```python
import functools

import jax
import jax.numpy as jnp
from jax import lax
from jax.experimental import pallas as pl
from jax.experimental.pallas import tpu as pltpu

F32 = jnp.float32
BF16 = jnp.bfloat16
HIGHEST = lax.Precision.HIGHEST

EPS = 1e-6
GRID_W = 64
CONV_W = 4
N_MOD = 6
NH_A = 16
LRU_C = 8.0
NK, DK, NV, DV = 16, 128, 32, 128
CHUNK = 64
LANES = 128
SUBLANES = 8
HALO = 16
PAIR_W = 384
VMEM_LIMIT = 60 * 1024 * 1024


def _cparams(sem):
    return pltpu.CompilerParams(dimension_semantics=sem, vmem_limit_bytes=VMEM_LIMIT)


def _nt_dot(a, b, **kw):
    return lax.dot_general(a, b, (((1,), (1,)), ((), ())), preferred_element_type=F32, **kw)


def _tn_dot(a, b):
    return lax.dot_general(a, b, (((0,), (0,)), ((), ())), preferred_element_type=F32)


def _bdot(a, b):
    return jnp.dot(a.astype(BF16), b.astype(BF16), preferred_element_type=F32)


def _mod_kernel(c_ref, w_ref, b_ref, o_ref):
    s = c_ref[...]
    s = s * jax.nn.sigmoid(s)
    o_ref[...] = jnp.dot(s, w_ref[...], precision=HIGHEST, preferred_element_type=F32) + b_ref[...]


def _mod(cc, w, b, tn=1024):
    m, d = cc.shape
    n = w.shape[1]
    return pl.pallas_call(
        _mod_kernel, grid=(n // tn,),
        in_specs=[pl.BlockSpec((m, d), lambda j: (0, 0)),
                  pl.BlockSpec((d, tn), lambda j: (0, j)),
                  pl.BlockSpec((1, tn), lambda j: (0, j))],
        out_specs=pl.BlockSpec((m, tn), lambda j: (0, j)),
        out_shape=jax.ShapeDtypeStruct((m, n), F32),
        compiler_params=_cparams(("parallel",)), name="mod")(cc, w, b)


def _rms_mod(xb, lnw, sc1, sh):
    xb = xb.astype(F32)
    ms = jnp.mean(xb * xb, axis=-1, keepdims=True)
    return (xb * lax.rsqrt(ms + EPS) * lnw) * sc1 + sh


def _inproj_kernel(x_ref, lnw_ref, sh_ref, sc_ref, w_ref, o_ref, h_scr, *, colmajor):
    @pl.when(pl.program_id(2) == 0)
    def _():
        lnw = lnw_ref[...]
        sc1 = 1.0 + sc_ref[...]
        sh = sh_ref[...]
        if colmajor:
            r = x_ref.shape[0]
            for k in range(SUBLANES):
                h_scr[k * r:(k + 1) * r, :] = _rms_mod(x_ref[:, k, :], lnw, sc1, sh).astype(h_scr.dtype)
        else:
            tm = x_ref.shape[0]
            ch = min(tm, 256)
            for r0 in range(0, tm, ch):
                h_scr[r0:r0 + ch, :] = _rms_mod(x_ref[r0:r0 + ch, :], lnw, sc1, sh).astype(h_scr.dtype)

    o_ref[...] = jnp.dot(h_scr[...], w_ref[...], preferred_element_type=F32).astype(o_ref.dtype)


def _inproj(x, lnw, sh, sc, w, *, ncols, colmajor, tn, out_dtype, tm=None, name="inproj"):
    b, s, d = x.shape
    if colmajor:
        rows = s // GRID_W
        tm = SUBLANES * rows
        xin = x.reshape(b, rows, GRID_W, d)
        x_spec = pl.BlockSpec((None, rows, SUBLANES, d), lambda bb, i, j: (bb, 0, i, 0))
    else:
        xin = x
        x_spec = pl.BlockSpec((None, tm, d), lambda bb, i, j: (bb, i, 0))
    return pl.pallas_call(
        functools.partial(_inproj_kernel, colmajor=colmajor),
        grid=(b, s // tm, ncols // tn),
        in_specs=[x_spec,
                  pl.BlockSpec((1, d), lambda bb, i, j: (0, 0)),
                  pl.BlockSpec((None, 1, d), lambda bb, i, j: (bb, 0, 0)),
                  pl.BlockSpec((None, 1, d), lambda bb, i, j: (bb, 0, 0)),
                  pl.BlockSpec((d, tn), lambda bb, i, j: (0, j))],
        out_specs=pl.BlockSpec((None, tm, tn), lambda bb, i, j: (bb, i, j)),
        out_shape=jax.ShapeDtypeStruct((b, s, ncols), out_dtype),
        scratch_shapes=[pltpu.VMEM((tm, d), BF16)],
        compiler_params=_cparams(("parallel", "parallel", "arbitrary")), name=name)(xin, lnw, sh, sc, w)


def _fill_padded(pad, xm_ref, xp_ref, xn_ref, first, last, tt):
    pad[0:HALO, :] = jnp.where(first, 0.0, xp_ref[...].astype(F32))
    pad[HALO:HALO + tt, :] = xm_ref[...].astype(F32)
    pad[HALO + tt:2 * HALO + tt, :] = jnp.where(last, 0.0, xn_ref[...].astype(F32))


def _conv_rows(pad, cw_ref, r0, n):
    acc = cw_ref[0:1, :] * pad[HALO + r0 - 1:HALO + r0 - 1 + n, :]
    for k in range(1, CONV_W):
        acc = acc + cw_ref[k:k + 1, :] * pad[HALO + r0 - 1 + k:HALO + r0 - 1 + k + n, :]
    return acc


def _halo_specs(tt, s, width, tb_of, col_of):
    nh = tt // HALO
    last_blk = s // HALO - 1
    return [
        pl.BlockSpec((None, tt, width), lambda *g: (g[0], tb_of(*g), col_of(*g))),
        pl.BlockSpec((None, HALO, width), lambda *g: (g[0], jnp.maximum(tb_of(*g) * nh - 1, 0), col_of(*g))),
        pl.BlockSpec((None, HALO, width), lambda *g: (g[0], jnp.minimum((tb_of(*g) + 1) * nh, last_blk), col_of(*g))),
    ]


def _lru_kernel(xm_ref, xp_ref, xn_ref, cw_ref, cb_ref, wg_ref, bg_ref, nla_ref, h0_ref,
                y_ref, ht_ref, pad, a_scr, b_scr, carry, *, nt, tt, reverse):
    i = pl.program_id(2)
    tb = nt - 1 - i if reverse else i
    seg = tt // SUBLANES
    nlc = PAIR_W // LANES

    @pl.when(i == 0)
    def _():
        carry[...] = h0_ref[...]

    _fill_padded(pad, xm_ref, xp_ref, xn_ref, tb == 0, tb == nt - 1, tt)

    for s in range(SUBLANES):
        xc = _conv_rows(pad, cw_ref, s * seg, seg) + cb_ref[...]
        gates = _bdot(xc, wg_ref[...]) + bg_ref[...]
        r = jax.nn.sigmoid(gates[:, :PAIR_W])
        ig = jax.nn.sigmoid(gates[:, PAIR_W:])
        la = nla_ref[...] * r
        a = jnp.exp(la)
        bx = jnp.sqrt(-jnp.tanh(la) * (a * a + 1.0)) * (ig * xc)
        for c in range(nlc):
            a_scr[c, :, s, :] = a[:, c * LANES:(c + 1) * LANES]
            b_scr[c, :, s, :] = bx[:, c * LANES:(c + 1) * LANES]

    def scan_body(jj, hp):
        j = seg - 1 - jj if reverse else jj
        out = []
        for c in range(nlc):
            a = a_scr[c, j]
            h = a * hp[2 * c] + b_scr[c, j]
            p = a * hp[2 * c + 1]
            b_scr[c, j] = h
            a_scr[c, j] = p
            out += [h, p]
        return tuple(out)

    init = tuple(jnp.zeros((SUBLANES, LANES), F32) if k % 2 == 0 else jnp.ones((SUBLANES, LANES), F32)
                 for k in range(2 * nlc))
    hp = lax.fori_loop(0, seg, scan_body, init, unroll=8)

    sub = lax.broadcasted_iota(jnp.int32, (SUBLANES, LANES), 0)
    cvecs = []
    for c in range(nlc):
        h, p = hp[2 * c], hp[2 * c + 1]
        cin = carry[:, c * LANES:(c + 1) * LANES]
        cvec = jnp.zeros((SUBLANES, LANES), F32)
        for s in (range(SUBLANES - 1, -1, -1) if reverse else range(SUBLANES)):
            cvec = jnp.where(sub == s, jnp.broadcast_to(cin, (SUBLANES, LANES)), cvec)
            cin = p[s:s + 1, :] * cin + h[s:s + 1, :]
        carry[:, c * LANES:(c + 1) * LANES] = cin
        cvecs.append(cvec)

    def fix_body(j, _):
        for c in range(nlc):
            b_scr[c, j] = b_scr[c, j] + a_scr[c, j] * cvecs[c]
        return 0

    lax.fori_loop(0, seg, fix_body, 0, unroll=8)

    for s in range(SUBLANES):
        for c in range(nlc):
            y_ref[s * seg:(s + 1) * seg, c * LANES:(c + 1) * LANES] = b_scr[c, :, s, :].astype(y_ref.dtype)

    @pl.when(i == nt - 1)
    def _():
        ht_ref[...] = carry[...]


def _lru(r_arr, cw, cb, wg, bg, nla, h0, *, reverse, tt, name):
    b, s, _ = r_arr.shape
    npair = cw.shape[1] // PAIR_W
    nt = s // tt
    seg = tt // SUBLANES
    tb_of = (lambda bb, p, i: nt - 1 - i) if reverse else (lambda bb, p, i: i)
    col_of = lambda bb, p, i: p
    par = lambda bb, p, i: (0, p)
    return pl.pallas_call(
        functools.partial(_lru_kernel, nt=nt, tt=tt, reverse=reverse),
        grid=(b, npair, nt),
        in_specs=_halo_specs(tt, s, PAIR_W, tb_of, col_of) + [
            pl.BlockSpec((CONV_W, PAIR_W), par),
            pl.BlockSpec((1, PAIR_W), par),
            pl.BlockSpec((None, PAIR_W, 2 * PAIR_W), lambda bb, p, i: (p, 0, 0)),
            pl.BlockSpec((None, 1, 2 * PAIR_W), lambda bb, p, i: (p, 0, 0)),
            pl.BlockSpec((1, PAIR_W), par),
            pl.BlockSpec((None, 1, PAIR_W), lambda bb, p, i: (bb, 0, p)),
        ],
        out_specs=[pl.BlockSpec((None, tt, PAIR_W), lambda bb, p, i: (bb, tb_of(bb, p, i), p)),
                   pl.BlockSpec((None, 1, PAIR_W), lambda bb, p, i: (bb, 0, p))],
        out_shape=[jax.ShapeDtypeStruct((b, s, npair * PAIR_W), BF16),
                   jax.ShapeDtypeStruct((b, 1, npair * PAIR_W), F32)],
        scratch_shapes=[pltpu.VMEM((tt + 2 * HALO, PAIR_W), F32),
                        pltpu.VMEM((PAIR_W // LANES, seg, SUBLANES, LANES), F32),
                        pltpu.VMEM((PAIR_W // LANES, seg, SUBLANES, LANES), F32),
                        pltpu.VMEM((1, PAIR_W), F32)],
        compiler_params=_cparams(("parallel", "parallel", "arbitrary")), name=name,
    )(r_arr, r_arr, r_arr, cw, cb, wg, bg, nla, h0)


def _prep_kernel(xm_ref, xp_ref, xn_ref, cw_ref, o_ref, pad, *, nt, tt, mode):
    i = pl.program_id(1)
    _fill_padded(pad, xm_ref, xp_ref, xn_ref, i == 0, i == nt - 1, tt)
    y = _conv_rows(pad, cw_ref, 0, tt)
    y = y * jax.nn.sigmoid(y)
    for h in range(o_ref.shape[0]):
        yh = y[:, h * LANES:(h + 1) * LANES]
        if mode != "v":
            yh = yh * lax.rsqrt(jnp.sum(yh * yh, axis=-1, keepdims=True) + EPS)
        if mode == "q":
            yh = yh * (DK ** -0.5)
        o_ref[h] = yh.astype(o_ref.dtype)


def _prep(q_arr, cw, *, mode, col0, nheads, tt, name):
    b, s, _ = q_arr.shape
    nh = 4
    width = nh * LANES
    nt = s // tt
    tb_of = lambda bb, i, j: i
    col_of = lambda bb, i, j: col0 + j
    return pl.pallas_call(
        functools.partial(_prep_kernel, nt=nt, tt=tt, mode=mode),
        grid=(b, nt, nheads // nh),
        in_specs=_halo_specs(tt, s, width, tb_of, col_of) + [pl.BlockSpec((CONV_W, width), lambda bb, i, j: (0, col0 + j))],
        out_specs=pl.BlockSpec((None, nh, tt, LANES), lambda bb, i, j: (bb, j, i, 0)),
        out_shape=jax.ShapeDtypeStruct((b, nheads, s, LANES), BF16),
        scratch_shapes=[pltpu.VMEM((tt + 2 * HALO, width), F32)],
        compiler_params=_cparams(("parallel", "parallel", "parallel")), name=name,
    )(q_arr, q_arr, q_arr, cw)


def _gates_kernel(x_ref, na_ref, dt_ref, o_ref):
    x = x_ref[...].astype(F32)
    lane = lax.broadcasted_iota(jnp.int32, x.shape, 1)
    z = x + dt_ref[...]
    sp = jnp.maximum(z, 0.0) + jnp.log1p(jnp.exp(-jnp.abs(z)))
    val = jnp.where(lane < 2 * NV, jax.nn.sigmoid(x), na_ref[...] * sp)
    o_ref[...] = val.T


def _gates(q_arr, na, dtb, *, col, tt):
    b, s, _ = q_arr.shape
    return pl.pallas_call(
        _gates_kernel, grid=(b, s // tt),
        in_specs=[pl.BlockSpec((None, tt, LANES), lambda bb, i: (bb, i, col)),
                  pl.BlockSpec((1, LANES), lambda bb, i: (0, 0)),
                  pl.BlockSpec((1, LANES), lambda bb, i: (0, 0))],
        out_specs=pl.BlockSpec((None, LANES, tt), lambda bb, i: (bb, 0, i)),
        out_shape=jax.ShapeDtypeStruct((b, LANES, s), F32),
        compiler_params=_cparams(("parallel", "parallel")), name="gdn_gates")(q_arr, na, dtb)


def _gdn_kernel(q_ref, k_ref, v_ref, bt_ref, g_ref, s0_ref, *rest, nt, tb, reverse, raster_out, write_out):
    if write_out:
        o_ref, st_ref, st = rest
    else:
        st_ref, st = rest
    i = pl.program_id(2)
    nch = tb // CHUNK
    nkh = q_ref.shape[0]

    @pl.when(i == 0)
    def _():
        st[...] = s0_ref[...]

    row = lax.broadcasted_iota(jnp.int32, (CHUNK, CHUNK), 0)
    col = lax.broadcasted_iota(jnp.int32, (CHUNK, CHUNK), 1)
    if reverse:
        incl, strict, tl = row <= col, row < col, 0
    else:
        incl, strict, tl = row >= col, row > col, CHUNK - 1
    tri = incl.astype(F32)
    tri_t = (col <= row if reverse else col >= row).astype(F32)
    eye = (row == col).astype(F32)

    def chunk_body(cc, carry):
        c = nch - 1 - cc if reverse else cc
        r0 = pl.multiple_of(c * CHUNK, CHUNK)
        gg = g_ref[:, pl.ds(c, 1), :].reshape(2 * nkh, CHUNK)
        bg = bt_ref[:, pl.ds(c, 1), :].reshape(2 * nkh, CHUNK)
        gc_row = jnp.dot(gg, tri_t, precision=HIGHEST, preferred_element_type=F32)
        gc_col = _nt_dot(tri, gg, precision=HIGHEST)
        b_col = _nt_dot(eye, bg, precision=HIGHEST)
        for kh in range(nkh):
            k = k_ref[kh, pl.ds(r0, CHUNK), :]
            q = q_ref[kh, pl.ds(r0, CHUNK), :]
            kk = _nt_dot(k, k)
            qk = _nt_dot(q, k)
            kf = k.astype(F32)
            qf = q.astype(F32)
            for e in range(2):
                m = 2 * kh + e
                gcc = gc_col[:, m:m + 1]
                gcr = gc_row[m:m + 1, :]
                bc = b_col[:, m:m + 1]
                decay = jnp.where(incl, jnp.exp(jnp.where(incl, gcc - gcr, 0.0)), 0.0)
                p = jnp.where(strict, -(kk * bc) * decay, 0.0)
                t = eye + p
                for _ in range(5):
                    p = _bdot(p, p)
                    t = t + _bdot(t, p)
                eg = jnp.exp(gcc)
                vf = v_ref[m, pl.ds(r0, CHUNK), :].astype(F32)
                rhs = jnp.concatenate([vf * bc, kf * (bc * eg)], axis=1)
                sol = _bdot(t, rhs)
                u = sol[:, :DV]
                w = sol[:, DV:]
                s_m = st[m]
                vnew = u - _bdot(w, s_m)
                if write_out:
                    o = _bdot(qf * eg, s_m) + _bdot(qk * decay, vnew)
                    if raster_out:
                        cpc = o_ref.shape[1] // CHUNK
                        o_ref[m, pl.ds(pl.multiple_of((c % cpc) * CHUNK, CHUNK), CHUNK), c // cpc, :] = o.astype(o_ref.dtype)
                    else:
                        o_ref[m, pl.ds(r0, CHUNK), :] = o.astype(o_ref.dtype)
                gl = gcc[tl:tl + 1, :]
                kd = kf * jnp.exp(gl - gcc)
                st[m] = s_m * jnp.exp(gl) + _tn_dot(kd.astype(BF16), vnew.astype(BF16))
        return carry

    lax.fori_loop(0, nch, chunk_body, 0)

    @pl.when(i == nt - 1)
    def _():
        st_ref[...] = st[...]


def _gdn(qh, kh, vh, gr, s0, *, reverse, tb, raster_out, write_out, name):
    b, nk, s, _ = qh.shape
    nv = vh.shape[1]
    nkh = 4
    nvh = 2 * nkh
    ngrp = nk // nkh
    nt = s // tb
    nch = tb // CHUNK
    d = 1 if reverse else 0
    tb_of = (lambda i: nt - 1 - i) if reverse else (lambda i: i)
    in_specs = [
        pl.BlockSpec((None, nkh, tb, DK), lambda bb, g, i: (bb, g, tb_of(i), 0)),
        pl.BlockSpec((None, nkh, tb, DK), lambda bb, g, i: (bb, g, tb_of(i), 0)),
        pl.BlockSpec((None, nvh, tb, DV), lambda bb, g, i: (bb, g, tb_of(i), 0)),
        pl.BlockSpec((None, nvh, nch, CHUNK), lambda bb, g, i: (bb, d * ngrp + g, tb_of(i), 0)),
        pl.BlockSpec((None, nvh, nch, CHUNK), lambda bb, g, i: (bb, (2 + d) * ngrp + g, tb_of(i), 0)),
        pl.BlockSpec((None, nvh, DK, DV), lambda bb, g, i: (bb, g, 0, 0)),
    ]
    st_spec = pl.BlockSpec((None, nvh, DK, DV), lambda bb, g, i: (bb, g, 0, 0))
    st_shape = jax.ShapeDtypeStruct((b, nv, DK, DV), F32)
    if write_out:
        if raster_out:
            rows = s // GRID_W
            assert tb == SUBLANES * rows and rows % CHUNK == 0
            o_spec = pl.BlockSpec((None, nvh, rows, SUBLANES, DV), lambda bb, g, i: (bb, g, 0, tb_of(i), 0))
            o_shape = jax.ShapeDtypeStruct((b, nv, rows, GRID_W, DV), F32)
        else:
            o_spec = pl.BlockSpec((None, nvh, tb, DV), lambda bb, g, i: (bb, g, tb_of(i), 0))
            o_shape = jax.ShapeDtypeStruct((b, nv, s, DV), BF16)
        out_specs, out_shape = [o_spec, st_spec], [o_shape, st_shape]
    else:
        out_specs, out_shape = [st_spec], [st_shape]
    return pl.pallas_call(
        functools.partial(_gdn_kernel, nt=nt, tb=tb, reverse=reverse, raster_out=raster_out, write_out=write_out),
        grid=(b, ngrp, nt), in_specs=in_specs, out_specs=out_specs, out_shape=out_shape,
        scratch_shapes=[pltpu.VMEM((nvh, DK, DV), F32)],
        compiler_params=_cparams(("parallel", "parallel", "arbitrary")), name=name,
    )(qh, kh, vh, gr, gr, s0)


def _gelu_tanh(x):
    return 0.5 * x * (1.0 + jnp.tanh(0.7978845608028654 * (x + 0.044715 * (x * x * x))))


def _aout_kernel(yf_ref, yb_ref, ag_ref, w_ref, o_ref, a_scr):
    @pl.when(pl.program_id(2) == 0)
    def _():
        y = yf_ref[...].astype(F32) + yb_ref[...].astype(F32)
        a_scr[...] = (y * _gelu_tanh(ag_ref[...].astype(F32))).astype(a_scr.dtype)

    o_ref[...] = jnp.dot(a_scr[...], w_ref[...], preferred_element_type=F32)


def _aout(yf, yb, r_arr, w, *, ag_blk, tm, tn):
    b, s, wa = yf.shape
    n = w.shape[1]
    act = pl.BlockSpec((None, tm, wa), lambda bb, i, j: (bb, i, 0))
    return pl.pallas_call(
        _aout_kernel, grid=(b, s // tm, n // tn),
        in_specs=[act, act,
                  pl.BlockSpec((None, tm, wa), lambda bb, i, j: (bb, i, ag_blk)),
                  pl.BlockSpec((wa, tn), lambda bb, i, j: (0, j))],
        out_specs=pl.BlockSpec((None, tm, tn), lambda bb, i, j: (bb, i, j)),
        out_shape=jax.ShapeDtypeStruct((b, s, n), F32),
        scratch_shapes=[pltpu.VMEM((tm, wa), BF16)],
        compiler_params=_cparams(("parallel", "parallel", "arbitrary")), name="aout")(yf, yb, r_arr, w)


def _bout_kernel(of_ref, ob_ref, zl_ref, zh_ref, nw_ref, ya_ref, ga_ref, gb_ref, w_ref, o_ref, y_scr):
    @pl.when(pl.program_id(2) == 0)
    def _():
        nw = nw_ref[...]
        nvh = of_ref.shape[0]
        half = nvh // 2
        for h in range(nvh):
            o = of_ref[h].astype(F32) + ob_ref[h].astype(F32)
            z_ref = zl_ref if h < half else zh_ref
            hh = h % half
            z = z_ref[:, hh * DV:(hh + 1) * DV].astype(F32)
            y = o * lax.rsqrt(jnp.mean(o * o, axis=-1, keepdims=True) + EPS)
            y = y * nw * (z * jax.nn.sigmoid(z))
            y_scr[:, h * DV:(h + 1) * DV] = y.astype(y_scr.dtype)

    yb = jnp.dot(y_scr[...], w_ref[...], preferred_element_type=F32)
    ga = jax.nn.sigmoid(ga_ref[...].astype(F32))
    gb = jax.nn.sigmoid(gb_ref[...].astype(F32))
    o_ref[...] = (ga * ya_ref[...] + gb * yb).astype(o_ref.dtype)


def _bout(of, ob, r_arr, nw, ya, w, *, z_blk, mg_col, tm, tn):
    b, nv, s, dv = of.shape
    n = w.shape[1]
    hw = nv * dv // 2
    o_spec = pl.BlockSpec((None, nv, tm, dv), lambda bb, i, j: (bb, 0, i, 0))
    return pl.pallas_call(
        _bout_kernel, grid=(b, s // tm, n // tn),
        in_specs=[o_spec, o_spec,
                  pl.BlockSpec((None, tm, hw), lambda bb, i, j: (bb, i, z_blk)),
                  pl.BlockSpec((None, tm, hw), lambda bb, i, j: (bb, i, z_blk + 1)),
                  pl.BlockSpec((1, dv), lambda bb, i, j: (0, 0)),
                  pl.BlockSpec((None, tm, tn), lambda bb, i, j: (bb, i, j)),
                  pl.BlockSpec((None, tm, tn), lambda bb, i, j: (bb, i, mg_col // tn + j)),
                  pl.BlockSpec((None, tm, tn), lambda bb, i, j: (bb, i, (mg_col + n) // tn + j)),
                  pl.BlockSpec((nv * dv, tn), lambda bb, i, j: (0, j))],
        out_specs=pl.BlockSpec((None, tm, tn), lambda bb, i, j: (bb, i, j)),
        out_shape=jax.ShapeDtypeStruct((b, s, n), BF16),
        scratch_shapes=[pltpu.VMEM((tm, nv * dv), BF16)],
        compiler_params=_cparams(("parallel", "parallel", "arbitrary")), name="bout",
    )(of, ob, r_arr, r_arr, nw, ya, r_arr, r_arr, w)


def _wout_kernel(m_ref, x_ref, gt_ref, w_ref, o_ref):
    o_ref[...] = x_ref[...] + gt_ref[...] * jnp.dot(m_ref[...], w_ref[...], preferred_element_type=F32)


def _wout(m, x, gt, w, *, tm, tn):
    b, s, d = x.shape
    return pl.pallas_call(
        _wout_kernel, grid=(b, s // tm, d // tn),
        in_specs=[pl.BlockSpec((None, tm, d), lambda bb, i, j: (bb, i, 0)),
                  pl.BlockSpec((None, tm, tn), lambda bb, i, j: (bb, i, j)),
                  pl.BlockSpec((None, 1, tn), lambda bb, i, j: (bb, 0, j)),
                  pl.BlockSpec((d, tn), lambda bb, i, j: (0, j))],
        out_specs=pl.BlockSpec((None, tm, tn), lambda bb, i, j: (bb, i, j)),
        out_shape=jax.ShapeDtypeStruct((b, s, d), F32),
        compiler_params=_cparams(("parallel", "parallel", "parallel")), name="wout")(m, x, gt, w)


def _ffn_kernel(x_ref, lnw_ref, sh_ref, sc_ref, gt_ref, wg_ref, wu_ref, wo_ref, fw_ref, o_ref, h_scr, acc, *, nf):
    f = pl.program_id(2)

    @pl.when(f == 0)
    def _():
        tm = x_ref.shape[0]
        ch = min(tm, 256)
        for r0 in range(0, tm, ch):
            h_scr[r0:r0 + ch, :] = _rms_mod(x_ref[r0:r0 + ch, :], lnw_ref[...], 1.0 + sc_ref[...], sh_ref[...]).astype(h_scr.dtype)
        acc[...] = jnp.zeros_like(acc)

    h = h_scr[...]
    g = jnp.dot(h, wg_ref[...], preferred_element_type=F32)
    u = jnp.dot(h, wu_ref[...], preferred_element_type=F32)
    a = (g * jax.nn.sigmoid(g) * u).astype(BF16)
    acc[...] += jnp.dot(a, wo_ref[...], preferred_element_type=F32)

    @pl.when(f == nf - 1)
    def _():
        x2 = x_ref[...] + gt_ref[...] * acc[...]
        ms = jnp.mean(x2 * x2, axis=-1, keepdims=True)
        o_ref[...] = x2 * lax.rsqrt(ms + EPS) * fw_ref[...]


def _ffn(x, lnw, sh, sc, gt, w_in, w_out, fw, *, tm, tf):
    b, s, d = x.shape
    fh = w_out.shape[0]
    nf = fh // tf
    vec = pl.BlockSpec((None, 1, d), lambda bb, i, f: (bb, 0, 0))
    par = pl.BlockSpec((1, d), lambda bb, i, f: (0, 0))
    return pl.pallas_call(
        functools.partial(_ffn_kernel, nf=nf), grid=(b, s // tm, nf),
        in_specs=[pl.BlockSpec((None, tm, d), lambda bb, i, f: (bb, i, 0)),
                  par, vec, vec, vec,
                  pl.BlockSpec((d, tf), lambda bb, i, f: (0, f)),
                  pl.BlockSpec((d, tf), lambda bb, i, f: (0, nf + f)),
                  pl.BlockSpec((tf, d), lambda bb, i, f: (f, 0)),
                  par],
        out_specs=pl.BlockSpec((None, tm, d), lambda bb, i, f: (bb, i, 0)),
        out_shape=jax.ShapeDtypeStruct((b, s, d), F32),
        scratch_shapes=[pltpu.VMEM((tm, d), BF16), pltpu.VMEM((tm, d), F32)],
        compiler_params=_cparams(("parallel", "parallel", "arbitrary")), name="ffn",
    )(x, lnw, sh, sc, gt, w_in, w_in, w_out, fw)


def _pad_pairs(t, axis):
    n = t.shape[axis]
    bw = n // NH_A
    shp = t.shape[:axis] + (NH_A // 2, 2 * bw) + t.shape[axis + 1:]
    t = t.reshape(shp)
    padw = [(0, 0)] * t.ndim
    padw[axis + 1] = (0, PAIR_W - 2 * bw)
    t = jnp.pad(t, padw)
    return t.reshape(t.shape[:axis] + (NH_A // 2 * PAIR_W,) + t.shape[axis + 2:])


def _pair_gate_weights(gw, gb):
    bw = gw.shape[1]
    npair = NH_A // 2
    g = gw.reshape(npair, 2, bw, 2, bw)
    wp = jnp.zeros((npair, PAIR_W, 2, PAIR_W), gw.dtype)
    for e in range(2):
        wp = wp.at[:, e * bw:(e + 1) * bw, :, e * bw:(e + 1) * bw].set(g[:, e])
    b = gb.reshape(npair, 2, 2, bw).transpose(0, 2, 1, 3).reshape(npair, 2, 2 * bw)
    b = jnp.pad(b, ((0, 0), (0, 0), (0, PAIR_W - 2 * bw)))
    return wp.reshape(npair, PAIR_W, 2 * PAIR_W).astype(BF16), b.reshape(npair, 1, 2 * PAIR_W)


def kernel(x, c, ctx, c_ctx, w_mod, b_mod, ln1_w, ln2_w, w_in, conv_a_w, conv_a_b, lru_gate_w, lru_gate_b, lru_lambda, conv_qkv_w, gdn_a_log, gdn_dt_bias, gdn_norm_w, w_a_out, w_b_out, w_out, w_ffn_in, w_ffn_out, final_norm_w):
    depth = w_mod.shape[0]
    assert depth == 1, "context stream updates are only needed for depth > 1"
    bsz, seq, d = x.shape
    lctx = ctx.shape[1]
    wa = lru_lambda.shape[-1]
    wap = NH_A // 2 * PAIR_W
    qk_dim, v_dim = NK * DK, NV * DV
    qkv_dim = 2 * qk_dim + v_dim
    l = 0

    sizes = (wa, wa, qkv_dim, v_dim, 2 * NV, 2 * NV, 2 * d)
    offs = [sum(sizes[:k]) for k in range(len(sizes) + 1)]
    wi = w_in[l]
    seg_w = [wi[:, offs[k]:offs[k + 1]] for k in range(len(sizes))]
    w_ax, w_ag, w_qkv, w_z, w_bt, w_al, w_mg = seg_w
    w_r = jnp.concatenate([_pad_pairs(w_ax, 1), _pad_pairs(w_ag, 1), w_z, w_mg], axis=1).astype(BF16)
    nc_c = -(-(qkv_dim + 4 * NV) // 512) * 512
    w_c = jnp.concatenate([w_qkv, w_bt, w_al, jnp.zeros((d, nc_c - qkv_dim - 4 * NV), F32)], axis=1).astype(BF16)
    nr = w_r.shape[1]
    cw_a = _pad_pairs(conv_a_w[l], 1)
    cb_a = _pad_pairs(conv_a_b[l][None], 1)
    nla = _pad_pairs(-LRU_C * jax.nn.softplus(-lru_lambda[l]), 1)
    gate_w = [_pair_gate_weights(lru_gate_w[l, dd], lru_gate_b[l, dd]) for dd in range(2)]
    neg_a = -jnp.exp(gdn_a_log[l]).reshape(1, 2 * NV)
    na_l = jnp.concatenate([jnp.zeros((1, 2 * NV), F32), neg_a], axis=1)
    dt_l = jnp.concatenate([jnp.zeros((1, 2 * NV), F32), gdn_dt_bias[l].reshape(1, 2 * NV)], axis=1)
    w_a = _pad_pairs(w_a_out[l], 0).astype(BF16)
    w_b = w_b_out[l].astype(BF16)
    w_o = w_out[l].astype(BF16)
    w_f1 = w_ffn_in[l].astype(BF16)
    w_f2 = w_ffn_out[l].astype(BF16)

    nrow = -(-(bsz + 1) // SUBLANES) * SUBLANES
    cc = jnp.concatenate([c, c_ctx[None], jnp.zeros((nrow - bsz - 1, d), F32)], axis=0)
    mod = _mod(cc, w_mod[l], b_mod[l][None]).reshape(nrow, N_MOD, 1, d)
    sh1, sc1, gt1, sh2, sc2, gt2 = (mod[:bsz, k] for k in range(N_MOD))
    csh1 = jnp.broadcast_to(mod[bsz, 0], (bsz, 1, d))
    csc1 = jnp.broadcast_to(mod[bsz, 1], (bsz, 1, d))
    ln1 = ln1_w[l][None]

    r_x = _inproj(x, ln1, sh1, sc1, w_r, ncols=nr, colmajor=False, tn=512, out_dtype=BF16, tm=min(1024, seq), name="inproj_r")
    q_x = _inproj(x, ln1, sh1, sc1, w_c, ncols=nc_c, colmajor=True, tn=512, out_dtype=F32, name="inproj_c")
    r_c = _inproj(ctx, ln1, csh1, csc1, w_r, ncols=wap, colmajor=False, tn=512, out_dtype=BF16, tm=lctx, name="inproj_rc")
    q_c = _inproj(ctx, ln1, csh1, csc1, w_c, ncols=nc_c, colmajor=False, tn=512, out_dtype=F32, tm=lctx, name="inproj_cc")

    ys = []
    for dd in range(2):
        wg, bg = gate_w[dd]
        kw = dict(reverse=bool(dd))
        h0 = jnp.zeros((bsz, 1, wap), F32)
        _, hc = _lru(r_c, cw_a, cb_a, wg, bg, nla[dd:dd + 1], h0, tt=min(512, lctx), name=f"lru_c{dd}", **kw)
        y, _ = _lru(r_x, cw_a, cb_a, wg, bg, nla[dd:dd + 1], hc, tt=min(512, seq), name=f"lru_x{dd}", **kw)
        ys.append(y)

    cw_q = conv_qkv_w[l]
    gcol = qkv_dim // LANES

    def prep_all(q_arr, tt, tag):
        qh = _prep(q_arr, cw_q, mode="q", col0=0, nheads=NK, tt=tt, name="prep_q" + tag)
        kh = _prep(q_arr, cw_q, mode="k", col0=qk_dim // 512, nheads=NK, tt=tt, name="prep_k" + tag)
        vh = _prep(q_arr, cw_q, mode="v", col0=2 * qk_dim // 512, nheads=NV, tt=tt, name="prep_v" + tag)
        gr = _gates(q_arr, na_l, dt_l, col=gcol, tt=tt)
        s_len = q_arr.shape[1]
        return qh, kh, vh, gr.reshape(bsz, LANES, s_len // CHUNK, CHUNK)

    pc = prep_all(q_c, min(512, lctx), "_c")
    px = prep_all(q_x, min(512, seq), "_x")
    rows = seq // GRID_W
    os_ = []
    for dd in range(2):
        s0 = jnp.zeros((bsz, NV, DK, DV), F32)
        (sc_state,) = _gdn(*pc, s0, reverse=bool(dd), tb=lctx, raster_out=False, write_out=False, name=f"gdn_c{dd}")
        o, _ = _gdn(*px, sc_state, reverse=bool(dd), tb=SUBLANES * rows, raster_out=True, write_out=True, name=f"gdn_x{dd}")
        os_.append(o.reshape(bsz, NV, seq, DV))

    ya = _aout(ys[0], ys[1], r_x, w_a, ag_blk=1, tm=min(512, seq), tn=512)
    m = _bout(os_[0], os_[1], r_x, gdn_norm_w[l][None], ya, w_b, z_blk=2 * wap // (v_dim // 2), mg_col=2 * wap + v_dim,
              tm=min(256, seq), tn=512)
    x1 = _wout(m, x, gt1, w_o, tm=min(1024, seq), tn=512)

    return _ffn(x1, ln2_w[l][None], sh2, sc2, gt2, w_f1, w_f2, final_norm_w[None], tm=min(512, seq), tf=512)
```

```python
import functools

import jax
import jax.numpy as jnp
from jax import lax
from jax.experimental import pallas as pl
from jax.experimental.pallas import tpu as pltpu

F32 = jnp.float32
BF16 = jnp.bfloat16
HIGHEST = lax.Precision.HIGHEST

EPS = 1e-6
GRID_W = 64
CONV_W = 4
N_MOD = 6
NH_A = 16
LRU_C = 8.0
NK, DK, NV, DV = 16, 128, 32, 128
CHUNK = 64
LANES = 128
SUBLANES = 8
HALO = 16
PAIR_W = 384
VMEM_LIMIT = 60 * 1024 * 1024


def _cparams(sem):
    return pltpu.CompilerParams(dimension_semantics=sem, vmem_limit_bytes=VMEM_LIMIT)


def _nt_dot(a, b, **kw):
    return lax.dot_general(a, b, (((1,), (1,)), ((), ())), preferred_element_type=F32, **kw)


def _tn_dot(a, b):
    return lax.dot_general(a, b, (((0,), (0,)), ((), ())), preferred_element_type=F32)


def _bdot(a, b):
    return jnp.dot(a.astype(BF16), b.astype(BF16), preferred_element_type=F32)


def _mod_kernel(c_ref, w_ref, b_ref, o_ref):
    s = c_ref[...]
    s = s * jax.nn.sigmoid(s)
    o_ref[...] = jnp.dot(s, w_ref[...], precision=HIGHEST, preferred_element_type=F32) + b_ref[...]


def _mod(cc, w, b, tn=1024):
    m, d = cc.shape
    n = w.shape[1]
    return pl.pallas_call(
        _mod_kernel, grid=(n // tn,),
        in_specs=[pl.BlockSpec((m, d), lambda j: (0, 0)),
                  pl.BlockSpec((d, tn), lambda j: (0, j)),
                  pl.BlockSpec((1, tn), lambda j: (0, j))],
        out_specs=pl.BlockSpec((m, tn), lambda j: (0, j)),
        out_shape=jax.ShapeDtypeStruct((m, n), F32),
        compiler_params=_cparams(("parallel",)), name="mod")(cc, w, b)


def _rms_mod(xb, lnw, sc1, sh):
    xb = xb.astype(F32)
    ms = jnp.mean(xb * xb, axis=-1, keepdims=True)
    return (xb * lax.rsqrt(ms + EPS) * lnw) * sc1 + sh


def _inproj_kernel(x_ref, lnw_ref, sh_ref, sc_ref, w_ref, o_ref, h_scr, *, colmajor):
    @pl.when(pl.program_id(2) == 0)
    def _():
        lnw = lnw_ref[...]
        sc1 = 1.0 + sc_ref[...]
        sh = sh_ref[...]
        if colmajor:
            r = x_ref.shape[0]
            for k in range(SUBLANES):
                h_scr[k * r:(k + 1) * r, :] = _rms_mod(x_ref[:, k, :], lnw, sc1, sh).astype(h_scr.dtype)
        else:
            tm = x_ref.shape[0]
            ch = min(tm, 256)
            for r0 in range(0, tm, ch):
                h_scr[r0:r0 + ch, :] = _rms_mod(x_ref[r0:r0 + ch, :], lnw, sc1, sh).astype(h_scr.dtype)

    o_ref[...] = jnp.dot(h_scr[...], w_ref[...], preferred_element_type=F32).astype(o_ref.dtype)


def _inproj(x, lnw, sh, sc, w, *, ncols, colmajor, tn, out_dtype, tm=None, name="inproj"):
    b, s, d = x.shape
    if colmajor:
        rows = s // GRID_W
        tm = SUBLANES * rows
        xin = x.reshape(b, rows, GRID_W, d)
        x_spec = pl.BlockSpec((None, rows, SUBLANES, d), lambda bb, i, j: (bb, 0, i, 0))
    else:
        xin = x
        x_spec = pl.BlockSpec((None, tm, d), lambda bb, i, j: (bb, i, 0))
    return pl.pallas_call(
        functools.partial(_inproj_kernel, colmajor=colmajor),
        grid=(b, s // tm, ncols // tn),
        in_specs=[x_spec,
                  pl.BlockSpec((1, d), lambda bb, i, j: (0, 0)),
                  pl.BlockSpec((None, 1, d), lambda bb, i, j: (bb, 0, 0)),
                  pl.BlockSpec((None, 1, d), lambda bb, i, j: (bb, 0, 0)),
                  pl.BlockSpec((d, tn), lambda bb, i, j: (0, j))],
        out_specs=pl.BlockSpec((None, tm, tn), lambda bb, i, j: (bb, i, j)),
        out_shape=jax.ShapeDtypeStruct((b, s, ncols), out_dtype),
        scratch_shapes=[pltpu.VMEM((tm, d), BF16)],
        compiler_params=_cparams(("parallel", "parallel", "arbitrary")), name=name)(xin, lnw, sh, sc, w)


def _fill_padded(pad, xm_ref, xp_ref, xn_ref, first, last, tt):
    pad[0:HALO, :] = jnp.where(first, 0.0, xp_ref[...].astype(F32))
    pad[HALO:HALO + tt, :] = xm_ref[...].astype(F32)
    pad[HALO + tt:2 * HALO + tt, :] = jnp.where(last, 0.0, xn_ref[...].astype(F32))


def _conv_rows(pad, cw_ref, r0, n):
    acc = cw_ref[0:1, :] * pad[HALO + r0 - 1:HALO + r0 - 1 + n, :]
    for k in range(1, CONV_W):
        acc = acc + cw_ref[k:k + 1, :] * pad[HALO + r0 - 1 + k:HALO + r0 - 1 + k + n, :]
    return acc


def _halo_specs(tt, s, width, tb_of, col_of):
    nh = tt // HALO
    last_blk = s // HALO - 1
    return [
        pl.BlockSpec((None, tt, width), lambda *g: (g[0], tb_of(*g), col_of(*g))),
        pl.BlockSpec((None, HALO, width), lambda *g: (g[0], jnp.maximum(tb_of(*g) * nh - 1, 0), col_of(*g))),
        pl.BlockSpec((None, HALO, width), lambda *g: (g[0], jnp.minimum((tb_of(*g) + 1) * nh, last_blk), col_of(*g))),
    ]


def _lru_kernel(xm_ref, xp_ref, xn_ref, cw_ref, cb_ref, wg_ref, bg_ref, nla_ref, h0_ref,
                y_ref, ht_ref, pad, a_scr, b_scr, carry, *, nt, tt, reverse):
    i = pl.program_id(2)
    tb = nt - 1 - i if reverse else i
    seg = tt // SUBLANES
    nlc = PAIR_W // LANES

    @pl.when(i == 0)
    def _():
        carry[...] = h0_ref[...]

    _fill_padded(pad, xm_ref, xp_ref, xn_ref, tb == 0, tb == nt - 1, tt)

    for s in range(SUBLANES):
        xc = _conv_rows(pad, cw_ref, s * seg, seg) + cb_ref[...]
        gates = _bdot(xc, wg_ref[...]) + bg_ref[...]
        r = jax.nn.sigmoid(gates[:, :PAIR_W])
        ig = jax.nn.sigmoid(gates[:, PAIR_W:])
        la = nla_ref[...] * r
        a = jnp.exp(la)
        bx = jnp.sqrt(-jnp.tanh(la) * (a * a + 1.0)) * (ig * xc)
        for c in range(nlc):
            a_scr[c, :, s, :] = a[:, c * LANES:(c + 1) * LANES]
            b_scr[c, :, s, :] = bx[:, c * LANES:(c + 1) * LANES]

    def scan_body(jj, hp):
        j = seg - 1 - jj if reverse else jj
        out = []
        for c in range(nlc):
            a = a_scr[c, j]
            h = a * hp[2 * c] + b_scr[c, j]
            p = a * hp[2 * c + 1]
            b_scr[c, j] = h
            a_scr[c, j] = p
            out += [h, p]
        return tuple(out)

    init = tuple(jnp.zeros((SUBLANES, LANES), F32) if k % 2 == 0 else jnp.ones((SUBLANES, LANES), F32)
                 for k in range(2 * nlc))
    hp = lax.fori_loop(0, seg, scan_body, init, unroll=8)

    sub = lax.broadcasted_iota(jnp.int32, (SUBLANES, LANES), 0)
    cvecs = []
    for c in range(nlc):
        h, p = hp[2 * c], hp[2 * c + 1]
        cin = carry[:, c * LANES:(c + 1) * LANES]
        cvec = jnp.zeros((SUBLANES, LANES), F32)
        for s in (range(SUBLANES - 1, -1, -1) if reverse else range(SUBLANES)):
            cvec = jnp.where(sub == s, jnp.broadcast_to(cin, (SUBLANES, LANES)), cvec)
            cin = p[s:s + 1, :] * cin + h[s:s + 1, :]
        carry[:, c * LANES:(c + 1) * LANES] = cin
        cvecs.append(cvec)

    def fix_body(j, _):
        for c in range(nlc):
            b_scr[c, j] = b_scr[c, j] + a_scr[c, j] * cvecs[c]
        return 0

    lax.fori_loop(0, seg, fix_body, 0, unroll=8)

    for s in range(SUBLANES):
        for c in range(nlc):
            y_ref[s * seg:(s + 1) * seg, c * LANES:(c + 1) * LANES] = b_scr[c, :, s, :].astype(y_ref.dtype)

    @pl.when(i == nt - 1)
    def _():
        ht_ref[...] = carry[...]


def _lru(r_arr, cw, cb, wg, bg, nla, h0, *, reverse, tt, name):
    b, s, _ = r_arr.shape
    npair = cw.shape[1] // PAIR_W
    nt = s // tt
    seg = tt // SUBLANES
    tb_of = (lambda bb, p, i: nt - 1 - i) if reverse else (lambda bb, p, i: i)
    col_of = lambda bb, p, i: p
    par = lambda bb, p, i: (0, p)
    return pl.pallas_call(
        functools.partial(_lru_kernel, nt=nt, tt=tt, reverse=reverse),
        grid=(b, npair, nt),
        in_specs=_halo_specs(tt, s, PAIR_W, tb_of, col_of) + [
            pl.BlockSpec((CONV_W, PAIR_W), par),
            pl.BlockSpec((1, PAIR_W), par),
            pl.BlockSpec((None, PAIR_W, 2 * PAIR_W), lambda bb, p, i: (p, 0, 0)),
            pl.BlockSpec((None, 1, 2 * PAIR_W), lambda bb, p, i: (p, 0, 0)),
            pl.BlockSpec((1, PAIR_W), par),
            pl.BlockSpec((None, 1, PAIR_W), lambda bb, p, i: (bb, 0, p)),
        ],
        out_specs=[pl.BlockSpec((None, tt, PAIR_W), lambda bb, p, i: (bb, tb_of(bb, p, i), p)),
                   pl.BlockSpec((None, 1, PAIR_W), lambda bb, p, i: (bb, 0, p))],
        out_shape=[jax.ShapeDtypeStruct((b, s, npair * PAIR_W), BF16),
                   jax.ShapeDtypeStruct((b, 1, npair * PAIR_W), F32)],
        scratch_shapes=[pltpu.VMEM((tt + 2 * HALO, PAIR_W), F32),
                        pltpu.VMEM((PAIR_W // LANES, seg, SUBLANES, LANES), F32),
                        pltpu.VMEM((PAIR_W // LANES, seg, SUBLANES, LANES), F32),
                        pltpu.VMEM((1, PAIR_W), F32)],
        compiler_params=_cparams(("parallel", "parallel", "arbitrary")), name=name,
    )(r_arr, r_arr, r_arr, cw, cb, wg, bg, nla, h0)


def _prep_kernel(xm_ref, xp_ref, xn_ref, cw_ref, o_ref, pad, *, nt, tt, mode):
    i = pl.program_id(1)
    _fill_padded(pad, xm_ref, xp_ref, xn_ref, i == 0, i == nt - 1, tt)
    y = _conv_rows(pad, cw_ref, 0, tt)
    y = y * jax.nn.sigmoid(y)
    for h in range(o_ref.shape[0]):
        yh = y[:, h * LANES:(h + 1) * LANES]
        if mode != "v":
            yh = yh * lax.rsqrt(jnp.sum(yh * yh, axis=-1, keepdims=True) + EPS)
        if mode == "q":
            yh = yh * (DK ** -0.5)
        o_ref[h] = yh.astype(o_ref.dtype)


def _prep(q_arr, cw, *, mode, col0, nheads, tt, name):
    b, s, _ = q_arr.shape
    nh = 4
    width = nh * LANES
    nt = s // tt
    tb_of = lambda bb, i, j: i
    col_of = lambda bb, i, j: col0 + j
    return pl.pallas_call(
        functools.partial(_prep_kernel, nt=nt, tt=tt, mode=mode),
        grid=(b, nt, nheads // nh),
        in_specs=_halo_specs(tt, s, width, tb_of, col_of) + [pl.BlockSpec((CONV_W, width), lambda bb, i, j: (0, col0 + j))],
        out_specs=pl.BlockSpec((None, nh, tt, LANES), lambda bb, i, j: (bb, j, i, 0)),
        out_shape=jax.ShapeDtypeStruct((b, nheads, s, LANES), BF16),
        scratch_shapes=[pltpu.VMEM((tt + 2 * HALO, width), F32)],
        compiler_params=_cparams(("parallel", "parallel", "parallel")), name=name,
    )(q_arr, q_arr, q_arr, cw)


def _gates_kernel(x_ref, na_ref, dt_ref, o_ref):
    x = x_ref[...].astype(F32)
    lane = lax.broadcasted_iota(jnp.int32, x.shape, 1)
    z = x + dt_ref[...]
    sp = jnp.maximum(z, 0.0) + jnp.log1p(jnp.exp(-jnp.abs(z)))
    val = jnp.where(lane < 2 * NV, jax.nn.sigmoid(x), na_ref[...] * sp)
    o_ref[...] = val.T


def _gates(q_arr, na, dtb, *, col, tt):
    b, s, _ = q_arr.shape
    return pl.pallas_call(
        _gates_kernel, grid=(b, s // tt),
        in_specs=[pl.BlockSpec((None, tt, LANES), lambda bb, i: (bb, i, col)),
                  pl.BlockSpec((1, LANES), lambda bb, i: (0, 0)),
                  pl.BlockSpec((1, LANES), lambda bb, i: (0, 0))],
        out_specs=pl.BlockSpec((None, LANES, tt), lambda bb, i: (bb, 0, i)),
        out_shape=jax.ShapeDtypeStruct((b, LANES, s), F32),
        compiler_params=_cparams(("parallel", "parallel")), name="gdn_gates")(q_arr, na, dtb)


def _gdn_kernel(q_ref, k_ref, v_ref, bt_ref, g_ref, s0_ref, *rest, nt, tb, reverse, raster_out, write_out):
    if write_out:
        o_ref, st_ref, st = rest
    else:
        st_ref, st = rest
    i = pl.program_id(2)
    nch = tb // CHUNK
    nkh = q_ref.shape[0]

    @pl.when(i == 0)
    def _():
        st[...] = s0_ref[...]

    c2 = 2 * CHUNK
    row = lax.broadcasted_iota(jnp.int32, (CHUNK, c2), 0)
    lane = lax.broadcasted_iota(jnp.int32, (CHUNK, c2), 1)
    colm = lane % CHUNK
    left = lane < CHUNK
    if reverse:
        incl, strict, tl = row <= colm, row < colm, 0
    else:
        incl, strict, tl = row >= colm, row > colm, CHUNK - 1
    eye2 = (row == colm).astype(F32)
    r64 = lax.broadcasted_iota(jnp.int32, (CHUNK, CHUNK), 0)
    c64 = lax.broadcasted_iota(jnp.int32, (CHUNK, CHUNK), 1)
    tri = (r64 <= c64 if reverse else r64 >= c64).astype(F32)
    tri_t = (c64 <= r64 if reverse else c64 >= r64).astype(F32)
    eye = (r64 == c64).astype(F32)
    z64 = jnp.zeros((CHUNK, CHUNK), F32)
    tri_t_l = jnp.concatenate([tri_t, z64], axis=1)
    tri_t_r = jnp.concatenate([z64, tri_t], axis=1)
    r128 = lax.broadcasted_iota(jnp.int32, (c2, c2), 0)
    l128 = lax.broadcasted_iota(jnp.int32, (c2, c2), 1)
    diag_blocks = (r128 < CHUNK) == (l128 < CHUNK)

    def split_stack(x2):
        return jnp.where(diag_blocks, jnp.concatenate([x2, x2], axis=0), 0.0).astype(BF16)

    def chunk_body(cc, carry):
        c = nch - 1 - cc if reverse else cc
        r0 = pl.multiple_of(c * CHUNK, CHUNK)
        rows = pl.ds(r0, CHUNK)
        gg = g_ref[:, pl.ds(c, 1), :].reshape(2 * nkh, CHUNK)
        bg = bt_ref[:, pl.ds(c, 1), :].reshape(2 * nkh, CHUNK)
        gc_col = _nt_dot(tri, gg, precision=HIGHEST)
        b_col = _nt_dot(eye, bg, precision=HIGHEST)
        gc_row2 = (jnp.dot(gg[:nkh], tri_t_l, precision=HIGHEST, preferred_element_type=F32)
                   + jnp.dot(gg[nkh:], tri_t_r, precision=HIGHEST, preferred_element_type=F32))

        ks, qs, pp, tt_, aa, cols = [], [], [], [], [], []
        for p in range(nkh):
            k = k_ref[p, rows, :]
            q = q_ref[p, rows, :]
            kq = _nt_dot(jnp.concatenate([k, q], axis=0), jnp.concatenate([k, k], axis=0))
            gca, gcb = gc_col[:, p:p + 1], gc_col[:, nkh + p:nkh + p + 1]
            bca, bcb = b_col[:, p:p + 1], b_col[:, nkh + p:nkh + p + 1]
            gcc2 = jnp.where(left, gca, gcb)
            bc2 = jnp.where(left, bca, bcb)
            decay2 = jnp.where(incl, jnp.exp(jnp.where(incl, gcc2 - gc_row2[p:p + 1, :], 0.0)), 0.0)
            pn = jnp.where(strict, -(kq[:CHUNK] * bc2) * decay2, 0.0)
            ks.append(k.astype(F32))
            qs.append(q.astype(F32))
            pp.append(pn)
            tt_.append(eye2 + pn)
            aa.append(kq[CHUNK:] * decay2)
            cols.append(((gca, bca), (gcb, bcb)))

        for p in range(nkh):
            pp[p] = jnp.dot(pp[p].astype(BF16), split_stack(pp[p]), preferred_element_type=F32)
        for step in range(4):
            for p in range(nkh):
                y = jnp.dot(jnp.concatenate([pp[p], tt_[p]], axis=0).astype(BF16), split_stack(pp[p]),
                            preferred_element_type=F32)
                pp[p] = y[:CHUNK]
                tt_[p] = tt_[p] + y[CHUNK:]
        for p in range(nkh):
            tt_[p] = tt_[p] + jnp.dot(tt_[p].astype(BF16), split_stack(pp[p]), preferred_element_type=F32)

        sols = []
        for p in range(nkh):
            rhs = []
            for e in range(2):
                gch, bch = cols[p][e]
                vf = v_ref[2 * p + e, rows, :].astype(F32)
                rhs.append(jnp.concatenate([vf * bch, ks[p] * (bch * jnp.exp(gch))], axis=1))
            sols.append(jnp.dot(split_stack(tt_[p]), jnp.concatenate(rhs, axis=0).astype(BF16),
                                preferred_element_type=F32))

        wss, vns = [], []
        for p in range(nkh):
            for e in range(2):
                m = 2 * p + e
                gch, _ = cols[p][e]
                sol = sols[p][e * CHUNK:(e + 1) * CHUNK]
                lhs = jnp.concatenate([sol[:, DV:], qs[p] * jnp.exp(gch)], axis=0)
                ws = _bdot(lhs, st[m])
                wss.append(ws)
                vns.append(sol[:, :DV] - ws[:CHUNK])

        for p in range(nkh):
            if write_out:
                oi = jnp.dot(split_stack(aa[p]), jnp.concatenate([vns[2 * p], vns[2 * p + 1]], axis=0).astype(BF16),
                             preferred_element_type=F32)
            for e in range(2):
                m = 2 * p + e
                gch, _ = cols[p][e]
                if write_out:
                    o = wss[2 * p + e][CHUNK:] + oi[e * CHUNK:(e + 1) * CHUNK]
                    if raster_out:
                        cpc = o_ref.shape[1] // CHUNK
                        o_ref[m, pl.ds(pl.multiple_of((c % cpc) * CHUNK, CHUNK), CHUNK), c // cpc, :] = o.astype(o_ref.dtype)
                    else:
                        o_ref[m, rows, :] = o.astype(o_ref.dtype)
                gl = gch[tl:tl + 1, :]
                kd = ks[p] * jnp.exp(gl - gch)
                st[m] = st[m] * jnp.exp(gl) + _tn_dot(kd.astype(BF16), vns[2 * p + e].astype(BF16))
        return carry

    lax.fori_loop(0, nch, chunk_body, 0)

    @pl.when(i == nt - 1)
    def _():
        st_ref[...] = st[...]


def _gdn(qh, kh, vh, gr, s0, *, reverse, tb, raster_out, write_out, name):
    b, nk, s, _ = qh.shape
    nv = vh.shape[1]
    nkh = 4
    nvh = 2 * nkh
    ngrp = nk // nkh
    nt = s // tb
    nch = tb // CHUNK
    d = 1 if reverse else 0
    tb_of = (lambda i: nt - 1 - i) if reverse else (lambda i: i)
    in_specs = [
        pl.BlockSpec((None, nkh, tb, DK), lambda bb, g, i: (bb, g, tb_of(i), 0)),
        pl.BlockSpec((None, nkh, tb, DK), lambda bb, g, i: (bb, g, tb_of(i), 0)),
        pl.BlockSpec((None, nvh, tb, DV), lambda bb, g, i: (bb, g, tb_of(i), 0)),
        pl.BlockSpec((None, nvh, nch, CHUNK), lambda bb, g, i: (bb, d * ngrp + g, tb_of(i), 0)),
        pl.BlockSpec((None, nvh, nch, CHUNK), lambda bb, g, i: (bb, (2 + d) * ngrp + g, tb_of(i), 0)),
        pl.BlockSpec((None, nvh, DK, DV), lambda bb, g, i: (bb, g, 0, 0)),
    ]
    st_spec = pl.BlockSpec((None, nvh, DK, DV), lambda bb, g, i: (bb, g, 0, 0))
    st_shape = jax.ShapeDtypeStruct((b, nv, DK, DV), F32)
    if write_out:
        if raster_out:
            rows = s // GRID_W
            assert tb == SUBLANES * rows and rows % CHUNK == 0
            o_spec = pl.BlockSpec((None, nvh, rows, SUBLANES, DV), lambda bb, g, i: (bb, g, 0, tb_of(i), 0))
            o_shape = jax.ShapeDtypeStruct((b, nv, rows, GRID_W, DV), F32)
        else:
            o_spec = pl.BlockSpec((None, nvh, tb, DV), lambda bb, g, i: (bb, g, tb_of(i), 0))
            o_shape = jax.ShapeDtypeStruct((b, nv, s, DV), BF16)
        out_specs, out_shape = [o_spec, st_spec], [o_shape, st_shape]
    else:
        out_specs, out_shape = [st_spec], [st_shape]
    return pl.pallas_call(
        functools.partial(_gdn_kernel, nt=nt, tb=tb, reverse=reverse, raster_out=raster_out, write_out=write_out),
        grid=(b, ngrp, nt), in_specs=in_specs, out_specs=out_specs, out_shape=out_shape,
        scratch_shapes=[pltpu.VMEM((nvh, DK, DV), F32)],
        compiler_params=_cparams(("parallel", "parallel", "arbitrary")), name=name,
    )(qh, kh, vh, gr, gr, s0)


def _gelu_tanh(x):
    return 0.5 * x * (1.0 + jnp.tanh(0.7978845608028654 * (x + 0.044715 * (x * x * x))))


def _aout_kernel(yf_ref, yb_ref, ag_ref, w_ref, o_ref, a_scr):
    @pl.when(pl.program_id(2) == 0)
    def _():
        y = yf_ref[...].astype(F32) + yb_ref[...].astype(F32)
        a_scr[...] = (y * _gelu_tanh(ag_ref[...].astype(F32))).astype(a_scr.dtype)

    o_ref[...] = jnp.dot(a_scr[...], w_ref[...], preferred_element_type=F32)


def _aout(yf, yb, r_arr, w, *, ag_blk, tm, tn):
    b, s, wa = yf.shape
    n = w.shape[1]
    act = pl.BlockSpec((None, tm, wa), lambda bb, i, j: (bb, i, 0))
    return pl.pallas_call(
        _aout_kernel, grid=(b, s // tm, n // tn),
        in_specs=[act, act,
                  pl.BlockSpec((None, tm, wa), lambda bb, i, j: (bb, i, ag_blk)),
                  pl.BlockSpec((wa, tn), lambda bb, i, j: (0, j))],
        out_specs=pl.BlockSpec((None, tm, tn), lambda bb, i, j: (bb, i, j)),
        out_shape=jax.ShapeDtypeStruct((b, s, n), F32),
        scratch_shapes=[pltpu.VMEM((tm, wa), BF16)],
        compiler_params=_cparams(("parallel", "parallel", "arbitrary")), name="aout")(yf, yb, r_arr, w)


def _bout_kernel(of_ref, ob_ref, zl_ref, zh_ref, nw_ref, ya_ref, ga_ref, gb_ref, w_ref, o_ref, y_scr):
    @pl.when(pl.program_id(2) == 0)
    def _():
        nw = nw_ref[...]
        nvh = of_ref.shape[0]
        half = nvh // 2
        for h in range(nvh):
            o = of_ref[h].astype(F32) + ob_ref[h].astype(F32)
            z_ref = zl_ref if h < half else zh_ref
            hh = h % half
            z = z_ref[:, hh * DV:(hh + 1) * DV].astype(F32)
            y = o * lax.rsqrt(jnp.mean(o * o, axis=-1, keepdims=True) + EPS)
            y = y * nw * (z * jax.nn.sigmoid(z))
            y_scr[:, h * DV:(h + 1) * DV] = y.astype(y_scr.dtype)

    yb = jnp.dot(y_scr[...], w_ref[...], preferred_element_type=F32)
    ga = jax.nn.sigmoid(ga_ref[...].astype(F32))
    gb = jax.nn.sigmoid(gb_ref[...].astype(F32))
    o_ref[...] = (ga * ya_ref[...] + gb * yb).astype(o_ref.dtype)


def _bout(of, ob, r_arr, nw, ya, w, *, z_blk, mg_col, tm, tn):
    b, nv, s, dv = of.shape
    n = w.shape[1]
    hw = nv * dv // 2
    o_spec = pl.BlockSpec((None, nv, tm, dv), lambda bb, i, j: (bb, 0, i, 0))
    return pl.pallas_call(
        _bout_kernel, grid=(b, s // tm, n // tn),
        in_specs=[o_spec, o_spec,
                  pl.BlockSpec((None, tm, hw), lambda bb, i, j: (bb, i, z_blk)),
                  pl.BlockSpec((None, tm, hw), lambda bb, i, j: (bb, i, z_blk + 1)),
                  pl.BlockSpec((1, dv), lambda bb, i, j: (0, 0)),
                  pl.BlockSpec((None, tm, tn), lambda bb, i, j: (bb, i, j)),
                  pl.BlockSpec((None, tm, tn), lambda bb, i, j: (bb, i, mg_col // tn + j)),
                  pl.BlockSpec((None, tm, tn), lambda bb, i, j: (bb, i, (mg_col + n) // tn + j)),
                  pl.BlockSpec((nv * dv, tn), lambda bb, i, j: (0, j))],
        out_specs=pl.BlockSpec((None, tm, tn), lambda bb, i, j: (bb, i, j)),
        out_shape=jax.ShapeDtypeStruct((b, s, n), BF16),
        scratch_shapes=[pltpu.VMEM((tm, nv * dv), BF16)],
        compiler_params=_cparams(("parallel", "parallel", "arbitrary")), name="bout",
    )(of, ob, r_arr, r_arr, nw, ya, r_arr, r_arr, w)


def _wout_kernel(m_ref, x_ref, gt_ref, w_ref, o_ref):
    o_ref[...] = x_ref[...] + gt_ref[...] * jnp.dot(m_ref[...], w_ref[...], preferred_element_type=F32)


def _wout(m, x, gt, w, *, tm, tn):
    b, s, d = x.shape
    return pl.pallas_call(
        _wout_kernel, grid=(b, s // tm, d // tn),
        in_specs=[pl.BlockSpec((None, tm, d), lambda bb, i, j: (bb, i, 0)),
                  pl.BlockSpec((None, tm, tn), lambda bb, i, j: (bb, i, j)),
                  pl.BlockSpec((None, 1, tn), lambda bb, i, j: (bb, 0, j)),
                  pl.BlockSpec((d, tn), lambda bb, i, j: (0, j))],
        out_specs=pl.BlockSpec((None, tm, tn), lambda bb, i, j: (bb, i, j)),
        out_shape=jax.ShapeDtypeStruct((b, s, d), F32),
        compiler_params=_cparams(("parallel", "parallel", "parallel")), name="wout")(m, x, gt, w)


def _ffn_kernel(x_ref, lnw_ref, sh_ref, sc_ref, gt_ref, wg_ref, wu_ref, wo_ref, fw_ref, o_ref, h_scr, acc, *, nf):
    f = pl.program_id(2)

    @pl.when(f == 0)
    def _():
        tm = x_ref.shape[0]
        ch = min(tm, 256)
        for r0 in range(0, tm, ch):
            h_scr[r0:r0 + ch, :] = _rms_mod(x_ref[r0:r0 + ch, :], lnw_ref[...], 1.0 + sc_ref[...], sh_ref[...]).astype(h_scr.dtype)
        acc[...] = jnp.zeros_like(acc)

    h = h_scr[...]
    g = jnp.dot(h, wg_ref[...], preferred_element_type=F32)
    u = jnp.dot(h, wu_ref[...], preferred_element_type=F32)
    a = (g * jax.nn.sigmoid(g) * u).astype(BF16)
    acc[...] += jnp.dot(a, wo_ref[...], preferred_element_type=F32)

    @pl.when(f == nf - 1)
    def _():
        x2 = x_ref[...] + gt_ref[...] * acc[...]
        ms = jnp.mean(x2 * x2, axis=-1, keepdims=True)
        o_ref[...] = x2 * lax.rsqrt(ms + EPS) * fw_ref[...]


def _ffn(x, lnw, sh, sc, gt, w_in, w_out, fw, *, tm, tf):
    b, s, d = x.shape
    fh = w_out.shape[0]
    nf = fh // tf
    vec = pl.BlockSpec((None, 1, d), lambda bb, i, f: (bb, 0, 0))
    par = pl.BlockSpec((1, d), lambda bb, i, f: (0, 0))
    return pl.pallas_call(
        functools.partial(_ffn_kernel, nf=nf), grid=(b, s // tm, nf),
        in_specs=[pl.BlockSpec((None, tm, d), lambda bb, i, f: (bb, i, 0)),
                  par, vec, vec, vec,
                  pl.BlockSpec((d, tf), lambda bb, i, f: (0, f)),
                  pl.BlockSpec((d, tf), lambda bb, i, f: (0, nf + f)),
                  pl.BlockSpec((tf, d), lambda bb, i, f: (f, 0)),
                  par],
        out_specs=pl.BlockSpec((None, tm, d), lambda bb, i, f: (bb, i, 0)),
        out_shape=jax.ShapeDtypeStruct((b, s, d), F32),
        scratch_shapes=[pltpu.VMEM((tm, d), BF16), pltpu.VMEM((tm, d), F32)],
        compiler_params=_cparams(("parallel", "parallel", "arbitrary")), name="ffn",
    )(x, lnw, sh, sc, gt, w_in, w_in, w_out, fw)


def _pad_pairs(t, axis):
    n = t.shape[axis]
    bw = n // NH_A
    shp = t.shape[:axis] + (NH_A // 2, 2 * bw) + t.shape[axis + 1:]
    t = t.reshape(shp)
    padw = [(0, 0)] * t.ndim
    padw[axis + 1] = (0, PAIR_W - 2 * bw)
    t = jnp.pad(t, padw)
    return t.reshape(t.shape[:axis] + (NH_A // 2 * PAIR_W,) + t.shape[axis + 2:])


def _pair_gate_weights(gw, gb):
    bw = gw.shape[1]
    npair = NH_A // 2
    g = gw.reshape(npair, 2, bw, 2, bw)
    wp = jnp.zeros((npair, PAIR_W, 2, PAIR_W), gw.dtype)
    for e in range(2):
        wp = wp.at[:, e * bw:(e + 1) * bw, :, e * bw:(e + 1) * bw].set(g[:, e])
    b = gb.reshape(npair, 2, 2, bw).transpose(0, 2, 1, 3).reshape(npair, 2, 2 * bw)
    b = jnp.pad(b, ((0, 0), (0, 0), (0, PAIR_W - 2 * bw)))
    return wp.reshape(npair, PAIR_W, 2 * PAIR_W).astype(BF16), b.reshape(npair, 1, 2 * PAIR_W)


def kernel(x, c, ctx, c_ctx, w_mod, b_mod, ln1_w, ln2_w, w_in, conv_a_w, conv_a_b, lru_gate_w, lru_gate_b, lru_lambda, conv_qkv_w, gdn_a_log, gdn_dt_bias, gdn_norm_w, w_a_out, w_b_out, w_out, w_ffn_in, w_ffn_out, final_norm_w):
    depth = w_mod.shape[0]
    assert depth == 1, "context stream updates are only needed for depth > 1"
    bsz, seq, d = x.shape
    lctx = ctx.shape[1]
    wa = lru_lambda.shape[-1]
    wap = NH_A // 2 * PAIR_W
    qk_dim, v_dim = NK * DK, NV * DV
    qkv_dim = 2 * qk_dim + v_dim
    l = 0

    sizes = (wa, wa, qkv_dim, v_dim, 2 * NV, 2 * NV, 2 * d)
    offs = [sum(sizes[:k]) for k in range(len(sizes) + 1)]
    wi = w_in[l]
    seg_w = [wi[:, offs[k]:offs[k + 1]] for k in range(len(sizes))]
    w_ax, w_ag, w_qkv, w_z, w_bt, w_al, w_mg = seg_w
    w_r = jnp.concatenate([_pad_pairs(w_ax, 1), _pad_pairs(w_ag, 1), w_z, w_mg], axis=1).astype(BF16)
    nc_c = -(-(qkv_dim + 4 * NV) // 512) * 512
    w_c = jnp.concatenate([w_qkv, w_bt, w_al, jnp.zeros((d, nc_c - qkv_dim - 4 * NV), F32)], axis=1).astype(BF16)
    nr = w_r.shape[1]
    cw_a = _pad_pairs(conv_a_w[l], 1)
    cb_a = _pad_pairs(conv_a_b[l][None], 1)
    nla = _pad_pairs(-LRU_C * jax.nn.softplus(-lru_lambda[l]), 1)
    gate_w = [_pair_gate_weights(lru_gate_w[l, dd], lru_gate_b[l, dd]) for dd in range(2)]
    neg_a = -jnp.exp(gdn_a_log[l]).reshape(1, 2 * NV)
    na_l = jnp.concatenate([jnp.zeros((1, 2 * NV), F32), neg_a], axis=1)
    dt_l = jnp.concatenate([jnp.zeros((1, 2 * NV), F32), gdn_dt_bias[l].reshape(1, 2 * NV)], axis=1)
    w_a = _pad_pairs(w_a_out[l], 0).astype(BF16)
    w_b = w_b_out[l].astype(BF16)
    w_o = w_out[l].astype(BF16)
    w_f1 = w_ffn_in[l].astype(BF16)
    w_f2 = w_ffn_out[l].astype(BF16)

    nrow = -(-(bsz + 1) // SUBLANES) * SUBLANES
    cc = jnp.concatenate([c, c_ctx[None], jnp.zeros((nrow - bsz - 1, d), F32)], axis=0)
    mod = _mod(cc, w_mod[l], b_mod[l][None]).reshape(nrow, N_MOD, 1, d)
    sh1, sc1, gt1, sh2, sc2, gt2 = (mod[:bsz, k] for k in range(N_MOD))
    csh1 = jnp.broadcast_to(mod[bsz, 0], (bsz, 1, d))
    csc1 = jnp.broadcast_to(mod[bsz, 1], (bsz, 1, d))
    ln1 = ln1_w[l][None]

    r_x = _inproj(x, ln1, sh1, sc1, w_r, ncols=nr, colmajor=False, tn=512, out_dtype=BF16, tm=min(1024, seq), name="inproj_r")
    q_x = _inproj(x, ln1, sh1, sc1, w_c, ncols=nc_c, colmajor=True, tn=512, out_dtype=F32, name="inproj_c")
    r_c = _inproj(ctx, ln1, csh1, csc1, w_r, ncols=wap, colmajor=False, tn=512, out_dtype=BF16, tm=lctx, name="inproj_rc")
    q_c = _inproj(ctx, ln1, csh1, csc1, w_c, ncols=nc_c, colmajor=False, tn=512, out_dtype=F32, tm=lctx, name="inproj_cc")

    ys = []
    for dd in range(2):
        wg, bg = gate_w[dd]
        kw = dict(reverse=bool(dd))
        h0 = jnp.zeros((bsz, 1, wap), F32)
        _, hc = _lru(r_c, cw_a, cb_a, wg, bg, nla[dd:dd + 1], h0, tt=min(512, lctx), name=f"lru_c{dd}", **kw)
        y, _ = _lru(r_x, cw_a, cb_a, wg, bg, nla[dd:dd + 1], hc, tt=min(512, seq), name=f"lru_x{dd}", **kw)
        ys.append(y)

    cw_q = conv_qkv_w[l]
    gcol = qkv_dim // LANES

    def prep_all(q_arr, tt, tag):
        qh = _prep(q_arr, cw_q, mode="q", col0=0, nheads=NK, tt=tt, name="prep_q" + tag)
        kh = _prep(q_arr, cw_q, mode="k", col0=qk_dim // 512, nheads=NK, tt=tt, name="prep_k" + tag)
        vh = _prep(q_arr, cw_q, mode="v", col0=2 * qk_dim // 512, nheads=NV, tt=tt, name="prep_v" + tag)
        gr = _gates(q_arr, na_l, dt_l, col=gcol, tt=tt)
        s_len = q_arr.shape[1]
        gr = gr.reshape(bsz, LANES // 8, 4, 2, s_len).transpose(0, 1, 3, 2, 4)
        return qh, kh, vh, gr.reshape(bsz, LANES, s_len // CHUNK, CHUNK)

    pc = prep_all(q_c, min(512, lctx), "_c")
    px = prep_all(q_x, min(512, seq), "_x")
    rows = seq // GRID_W
    os_ = []
    for dd in range(2):
        s0 = jnp.zeros((bsz, NV, DK, DV), F32)
        (sc_state,) = _gdn(*pc, s0, reverse=bool(dd), tb=lctx, raster_out=False, write_out=False, name=f"gdn_c{dd}")
        o, _ = _gdn(*px, sc_state, reverse=bool(dd), tb=SUBLANES * rows, raster_out=True, write_out=True, name=f"gdn_x{dd}")
        os_.append(o.reshape(bsz, NV, seq, DV))

    ya = _aout(ys[0], ys[1], r_x, w_a, ag_blk=1, tm=min(512, seq), tn=512)
    m = _bout(os_[0], os_[1], r_x, gdn_norm_w[l][None], ya, w_b, z_blk=2 * wap // (v_dim // 2), mg_col=2 * wap + v_dim,
              tm=min(256, seq), tn=512)
    x1 = _wout(m, x, gt1, w_o, tm=min(1024, seq), tn=512)

    return _ffn(x1, ln2_w[l][None], sh2, sc2, gt2, w_f1, w_f2, final_norm_w[None], tm=min(512, seq), tf=512)
```

```python
import functools

import jax
import jax.numpy as jnp
import numpy as np
from jax import lax
from jax.experimental import pallas as pl
from jax.experimental.pallas import tpu as pltpu

F32 = jnp.float32
BF16 = jnp.bfloat16
HIGHEST = lax.Precision.HIGHEST

EPS = 1e-6
GRID_W = 64
CONV_W = 4
N_MOD = 6
NH_A = 16
LRU_C = 8.0
NK, DK, NV, DV = 16, 128, 32, 128
CHUNK = 64
LANES = 128
SUBLANES = 8
HALO = 16
PAIR_W = 384
STRIP = SUBLANES * GRID_W
TILE_R = 64
VMEM_LIMIT = 60 * 1024 * 1024


def _cparams(sem):
    return pltpu.CompilerParams(dimension_semantics=sem, vmem_limit_bytes=VMEM_LIMIT)


def _nt_dot(a, b, **kw):
    return lax.dot_general(a, b, (((1,), (1,)), ((), ())), preferred_element_type=F32, **kw)


def _tn_dot(a, b):
    return lax.dot_general(a, b, (((0,), (0,)), ((), ())), preferred_element_type=F32)


def _bdot(a, b):
    return jnp.dot(a.astype(BF16), b.astype(BF16), preferred_element_type=F32)


def _mod_kernel(c_ref, w_ref, b_ref, o_ref):
    s = c_ref[...]
    s = s * jax.nn.sigmoid(s)
    o_ref[...] = jnp.dot(s, w_ref[...], precision=HIGHEST, preferred_element_type=F32) + b_ref[...]


def _mod(cc, w, b, tn=1024):
    m, d = cc.shape
    n = w.shape[1]
    return pl.pallas_call(
        _mod_kernel, grid=(n // tn,),
        in_specs=[pl.BlockSpec((m, d), lambda j: (0, 0)),
                  pl.BlockSpec((d, tn), lambda j: (0, j)),
                  pl.BlockSpec((1, tn), lambda j: (0, j))],
        out_specs=pl.BlockSpec((m, tn), lambda j: (0, j)),
        out_shape=jax.ShapeDtypeStruct((m, n), F32),
        compiler_params=_cparams(("parallel",)), name="mod")(cc, w, b)


def _rms_mod(xb, lnw, sc1, sh):
    xb = xb.astype(F32)
    ms = jnp.mean(xb * xb, axis=-1, keepdims=True)
    return (xb * lax.rsqrt(ms + EPS) * lnw) * sc1 + sh


def _strip_perm(tt):
    seg = tt // SUBLANES
    i = np.arange(tt)
    p = np.zeros((tt, tt), np.float32)
    p[i, (i % SUBLANES) * seg + i // SUBLANES] = 1.0
    return jnp.asarray(p, BF16)


def _inproj_kernel(*refs, order, tiles, tt):
    nseg = len(tiles)
    x_ref, lnw_ref, sh_ref, sc_ref = refs[:4]
    pos = 5 if order == "strip" else 4
    w_refs = refs[pos:pos + nseg]
    o_refs = refs[pos + nseg:pos + 2 * nseg]
    scr = refs[pos + 2 * nseg:]
    h_scr = scr[0]
    j = pl.program_id(2)

    @pl.when(j == 0)
    def _():
        lnw = lnw_ref[...]
        sc1 = 1.0 + sc_ref[...]
        sh = sh_ref[...]
        if order == "colmajor":
            r = x_ref.shape[0]
            for k in range(SUBLANES):
                h_scr[k * r:(k + 1) * r, :] = _rms_mod(x_ref[:, k, :], lnw, sc1, sh).astype(h_scr.dtype)
        else:
            dst = scr[1] if order == "strip" else h_scr
            tm = x_ref.shape[0]
            ch = min(tm, 256)
            for r0 in range(0, tm, ch):
                dst[r0:r0 + ch, :] = _rms_mod(x_ref[r0:r0 + ch, :], lnw, sc1, sh).astype(dst.dtype)
            if order == "strip":
                perm = refs[4][...]
                for r0 in range(0, tm, tt):
                    h_scr[r0:r0 + tt, :] = jnp.dot(perm, dst[r0:r0 + tt, :], preferred_element_type=F32).astype(h_scr.dtype)

    lo = 0
    for k in range(nseg):
        @pl.when((j >= lo) & (j < lo + tiles[k]))
        def _(k=k):
            o_refs[k][...] = jnp.dot(h_scr[...], w_refs[k][...], preferred_element_type=F32).astype(o_refs[k].dtype)
        lo += tiles[k]


def _inproj(x, lnw, sh, sc, segs, *, order, tm=None, tt=None, name="inproj"):
    b, s, d = x.shape
    ins = [None, lnw, sh, sc]
    if order == "colmajor":
        rows = s // GRID_W
        tm = SUBLANES * rows
        ins[0] = x.reshape(b, rows, GRID_W, d)
        x_spec = pl.BlockSpec((None, rows, SUBLANES, d), lambda bb, i, j: (bb, 0, i, 0))
    else:
        ins[0] = x
        x_spec = pl.BlockSpec((None, tm, d), lambda bb, i, j: (bb, i, 0))
    in_specs = [x_spec,
                pl.BlockSpec((1, d), lambda bb, i, j: (0, 0)),
                pl.BlockSpec((None, 1, d), lambda bb, i, j: (bb, 0, 0)),
                pl.BlockSpec((None, 1, d), lambda bb, i, j: (bb, 0, 0))]
    scratch = [pltpu.VMEM((tm, d), BF16)]
    if order == "strip":
        ins.append(_strip_perm(tt))
        in_specs.append(pl.BlockSpec((tt, tt), lambda bb, i, j: (0, 0)))
        scratch.append(pltpu.VMEM((tm, d), BF16))
    tiles = tuple(sg[2] for sg in segs)
    out_specs, out_shape = [], []
    lo = 0
    for w, tn, nt, dt in segs:
        clip = functools.partial(lambda j, lo, nt: jnp.clip(j - lo, 0, nt - 1), lo=lo, nt=nt)
        ins.append(w)
        in_specs.append(pl.BlockSpec((d, tn), lambda bb, i, j, clip=clip: (0, clip(j))))
        out_specs.append(pl.BlockSpec((None, tm, tn), lambda bb, i, j, clip=clip: (bb, i, clip(j))))
        out_shape.append(jax.ShapeDtypeStruct((b, s, tn * nt), dt))
        lo += nt
    return pl.pallas_call(
        functools.partial(_inproj_kernel, order=order, tiles=tiles, tt=tt),
        grid=(b, s // tm, lo), in_specs=in_specs, out_specs=out_specs, out_shape=out_shape,
        scratch_shapes=scratch,
        compiler_params=_cparams(("parallel", "parallel", "arbitrary")), name=name)(*ins)


def _lru_kernel(xm_ref, xp_ref, xn_ref, cw_ref, cb_ref, wg_ref, bg_ref, nla_ref, h0_ref,
                y_ref, ht_ref, pad, a_scr, b_scr, carry, *, nt, tt, reverse):
    i = pl.program_id(2)
    tb = nt - 1 - i if reverse else i
    seg = tt // SUBLANES
    gm = min(tt, 256)

    @pl.when(i == 0)
    def _():
        carry[...] = h0_ref[...]

    sub = lax.broadcasted_iota(jnp.int32, (SUBLANES, PAIR_W), 0)
    first, last = tb == 0, tb == nt - 1
    prev_last = jnp.where(first, 0.0, xp_ref[SUBLANES:2 * SUBLANES, :].astype(F32))
    nxt0 = jnp.where(last, 0.0, xn_ref[0:SUBLANES, :].astype(F32))
    nxt1 = jnp.where(last, 0.0, xn_ref[SUBLANES:2 * SUBLANES, :].astype(F32))
    x_end = xm_ref[tt - SUBLANES:tt, :].astype(F32)
    x_0 = xm_ref[0:SUBLANES, :].astype(F32)
    x_1 = xm_ref[SUBLANES:2 * SUBLANES, :].astype(F32)
    up = SUBLANES - 1
    pad[0:SUBLANES, :] = jnp.where(sub == 0, pltpu.roll(prev_last, 1, 0), pltpu.roll(x_end, 1, 0))
    pad[SUBLANES:SUBLANES + tt, :] = xm_ref[...].astype(F32)
    pad[SUBLANES + tt:2 * SUBLANES + tt, :] = jnp.where(sub == up, pltpu.roll(nxt0, up, 0), pltpu.roll(x_0, up, 0))
    pad[2 * SUBLANES + tt:3 * SUBLANES + tt, :] = jnp.where(sub == up, pltpu.roll(nxt1, up, 0), pltpu.roll(x_1, up, 0))

    for r0 in range(0, tt, gm):
        xc = cb_ref[...] + cw_ref[0:1, :] * pad[r0:r0 + gm, :]
        for k in range(1, CONV_W):
            xc = xc + cw_ref[k:k + 1, :] * pad[SUBLANES * k + r0:SUBLANES * k + r0 + gm, :]
        gates = _bdot(xc, wg_ref[...]) + bg_ref[...]
        r = jax.nn.sigmoid(gates[:, :PAIR_W])
        ig = jax.nn.sigmoid(gates[:, PAIR_W:])
        la = nla_ref[...] * r
        a = jnp.exp(la)
        a_scr[r0:r0 + gm, :] = a
        b_scr[r0:r0 + gm, :] = jnp.sqrt(-jnp.tanh(la) * (a * a + 1.0)) * (ig * xc)

    def scan_body(jj, hp):
        j = seg - 1 - jj if reverse else jj
        rows = pl.ds(pl.multiple_of(j * SUBLANES, SUBLANES), SUBLANES)
        a = a_scr[rows, :]
        h = a * hp[0] + b_scr[rows, :]
        p = a * hp[1]
        b_scr[rows, :] = h
        a_scr[rows, :] = p
        return h, p

    h, p = lax.fori_loop(0, seg, scan_body,
                         (jnp.zeros((SUBLANES, PAIR_W), F32), jnp.ones((SUBLANES, PAIR_W), F32)), unroll=8)

    cin = carry[...]
    cvec = jnp.zeros((SUBLANES, PAIR_W), F32)
    for s in (range(SUBLANES - 1, -1, -1) if reverse else range(SUBLANES)):
        cvec = jnp.where(sub == s, jnp.broadcast_to(cin, (SUBLANES, PAIR_W)), cvec)
        cin = p[s:s + 1, :] * cin + h[s:s + 1, :]
    carry[...] = cin
    cvec2 = jnp.concatenate([cvec, cvec], axis=0)

    def fix_body(j, _):
        rows = pl.ds(pl.multiple_of(j * 2 * SUBLANES, 2 * SUBLANES), 2 * SUBLANES)
        y_ref[rows, :] = (b_scr[rows, :] + a_scr[rows, :] * cvec2).astype(y_ref.dtype)
        return 0

    lax.fori_loop(0, seg // 2, fix_body, 0, unroll=4)

    @pl.when(i == nt - 1)
    def _():
        ht_ref[...] = carry[...]


def _lru(ax, cw, cb, wg, bg, nla, h0, *, reverse, tt, name):
    b, s, _ = ax.shape
    npair = cw.shape[1] // PAIR_W
    nt = s // tt
    nh = tt // HALO
    last_blk = s // HALO - 1
    tb_of = (lambda i: nt - 1 - i) if reverse else (lambda i: i)
    par = lambda bb, p, i: (0, p)
    return pl.pallas_call(
        functools.partial(_lru_kernel, nt=nt, tt=tt, reverse=reverse),
        grid=(b, npair, nt),
        in_specs=[
            pl.BlockSpec((None, tt, PAIR_W), lambda bb, p, i: (bb, tb_of(i), p)),
            pl.BlockSpec((None, HALO, PAIR_W), lambda bb, p, i: (bb, jnp.maximum(tb_of(i) * nh - 1, 0), p)),
            pl.BlockSpec((None, HALO, PAIR_W), lambda bb, p, i: (bb, jnp.minimum((tb_of(i) + 1) * nh, last_blk), p)),
            pl.BlockSpec((CONV_W, PAIR_W), par),
            pl.BlockSpec((1, PAIR_W), par),
            pl.BlockSpec((None, PAIR_W, 2 * PAIR_W), lambda bb, p, i: (p, 0, 0)),
            pl.BlockSpec((None, 1, 2 * PAIR_W), lambda bb, p, i: (p, 0, 0)),
            pl.BlockSpec((1, PAIR_W), par),
            pl.BlockSpec((None, 1, PAIR_W), lambda bb, p, i: (bb, 0, p)),
        ],
        out_specs=[pl.BlockSpec((None, tt, PAIR_W), lambda bb, p, i: (bb, tb_of(i), p)),
                   pl.BlockSpec((None, 1, PAIR_W), lambda bb, p, i: (bb, 0, p))],
        out_shape=[jax.ShapeDtypeStruct((b, s, npair * PAIR_W), BF16),
                   jax.ShapeDtypeStruct((b, 1, npair * PAIR_W), F32)],
        scratch_shapes=[pltpu.VMEM((tt + 3 * SUBLANES, PAIR_W), F32),
                        pltpu.VMEM((tt, PAIR_W), F32),
                        pltpu.VMEM((tt, PAIR_W), F32),
                        pltpu.VMEM((1, PAIR_W), F32)],
        compiler_params=_cparams(("parallel", "parallel", "arbitrary")), name=name,
    )(ax, ax, ax, cw, cb, wg, bg, nla, h0)


def _prep_kernel(xm_ref, xp_ref, xn_ref, cw_ref, o_ref, pad, *, nt, tt, mode):
    i = pl.program_id(1)
    pad[0:HALO, :] = jnp.where(i == 0, 0.0, xp_ref[...].astype(F32))
    pad[HALO:HALO + tt, :] = xm_ref[...].astype(F32)
    pad[HALO + tt:2 * HALO + tt, :] = jnp.where(i == nt - 1, 0.0, xn_ref[...].astype(F32))
    y = cw_ref[0:1, :] * pad[HALO - 1:HALO - 1 + tt, :]
    for k in range(1, CONV_W):
        y = y + cw_ref[k:k + 1, :] * pad[HALO - 1 + k:HALO - 1 + k + tt, :]
    y = y * jax.nn.sigmoid(y)
    for h in range(o_ref.shape[0]):
        yh = y[:, h * LANES:(h + 1) * LANES]
        if mode != "v":
            yh = yh * lax.rsqrt(jnp.sum(yh * yh, axis=-1, keepdims=True) + EPS)
        if mode == "q":
            yh = yh * (DK ** -0.5)
        o_ref[h] = yh.astype(o_ref.dtype)


def _prep(q_arr, cw, *, mode, col0, nheads, tt, name):
    b, s, _ = q_arr.shape
    nh = 4
    width = nh * LANES
    nt = s // tt
    nhalo = tt // HALO
    last_blk = s // HALO - 1
    return pl.pallas_call(
        functools.partial(_prep_kernel, nt=nt, tt=tt, mode=mode),
        grid=(b, nt, nheads // nh),
        in_specs=[pl.BlockSpec((None, tt, width), lambda bb, i, j: (bb, i, col0 + j)),
                  pl.BlockSpec((None, HALO, width), lambda bb, i, j: (bb, jnp.maximum(i * nhalo - 1, 0), col0 + j)),
                  pl.BlockSpec((None, HALO, width), lambda bb, i, j: (bb, jnp.minimum((i + 1) * nhalo, last_blk), col0 + j)),
                  pl.BlockSpec((CONV_W, width), lambda bb, i, j: (0, col0 + j))],
        out_specs=pl.BlockSpec((None, nh, tt, LANES), lambda bb, i, j: (bb, j, i, 0)),
        out_shape=jax.ShapeDtypeStruct((b, nheads, s, LANES), BF16),
        scratch_shapes=[pltpu.VMEM((tt + 2 * HALO, width), F32)],
        compiler_params=_cparams(("parallel", "parallel", "parallel")), name=name,
    )(q_arr, q_arr, q_arr, cw)


def _gates_kernel(x_ref, na_ref, dt_ref, oc_ref, or_ref):
    x = x_ref[...].astype(F32)
    lane = lax.broadcasted_iota(jnp.int32, x.shape, 1)
    z = x + dt_ref[...]
    sp = jnp.maximum(z, 0.0) + jnp.log1p(jnp.exp(-jnp.abs(z)))
    val = jnp.where(lane < 2 * NV, jax.nn.sigmoid(x), na_ref[...] * sp)
    oc_ref[...] = val
    or_ref[...] = val.T


def _gates(g_arr, na, dtb, *, tt):
    b, s, _ = g_arr.shape
    return pl.pallas_call(
        _gates_kernel, grid=(b, s // tt),
        in_specs=[pl.BlockSpec((None, tt, LANES), lambda bb, i: (bb, i, 0)),
                  pl.BlockSpec((1, LANES), lambda bb, i: (0, 0)),
                  pl.BlockSpec((1, LANES), lambda bb, i: (0, 0))],
        out_specs=[pl.BlockSpec((None, tt, LANES), lambda bb, i: (bb, i, 0)),
                   pl.BlockSpec((None, LANES, tt), lambda bb, i: (bb, 0, i))],
        out_shape=[jax.ShapeDtypeStruct((b, s, LANES), F32), jax.ShapeDtypeStruct((b, LANES, s), F32)],
        compiler_params=_cparams(("parallel", "parallel")), name="gdn_gates")(g_arr, na, dtb)


def _gdn_kernel(q_ref, k_ref, v_ref, gc_ref, gr_ref, s0_ref, *rest, nt, tb, reverse, write_out, wave):
    if write_out:
        o_ref, st_ref, st = rest
    else:
        st_ref, st = rest
    i = pl.program_id(1)
    nch = tb // CHUNK
    nkh = q_ref.shape[0]
    lane_b = NV if reverse else 0
    lane_g = 2 * NV + lane_b

    @pl.when(i == 0)
    def _():
        st[...] = s0_ref[...]

    c2 = 2 * CHUNK
    row = lax.broadcasted_iota(jnp.int32, (CHUNK, c2), 0)
    lane = lax.broadcasted_iota(jnp.int32, (CHUNK, c2), 1)
    colm = lane % CHUNK
    left = lane < CHUNK
    if reverse:
        incl, strict, tl = row <= colm, row < colm, 0
    else:
        incl, strict, tl = row >= colm, row > colm, CHUNK - 1
    eye2 = (row == colm).astype(F32)
    r64 = lax.broadcasted_iota(jnp.int32, (CHUNK, CHUNK), 0)
    c64 = lax.broadcasted_iota(jnp.int32, (CHUNK, CHUNK), 1)
    tri = (r64 <= c64 if reverse else r64 >= c64).astype(F32)
    tri_t = (c64 <= r64 if reverse else c64 >= r64).astype(F32)
    z64 = jnp.zeros((CHUNK, CHUNK), F32)
    tri_t_l = jnp.concatenate([tri_t, z64], axis=1)
    tri_t_r = jnp.concatenate([z64, tri_t], axis=1)
    r128 = lax.broadcasted_iota(jnp.int32, (c2, c2), 0)
    l128 = lax.broadcasted_iota(jnp.int32, (c2, c2), 1)
    diag_blocks = (r128 < CHUNK) == (l128 < CHUNK)

    def split_stack(x2):
        return jnp.where(diag_blocks, jnp.concatenate([x2, x2], axis=0), 0.0).astype(BF16)

    def chunk_body(cc, carry):
        c = nch - 1 - cc if reverse else cc
        rows = pl.ds(pl.multiple_of(c * CHUNK, CHUNK), CHUNK)
        gcol = gc_ref[rows, :]
        cs_col = jnp.dot(tri, gcol, precision=HIGHEST, preferred_element_type=F32)
        grow = gr_ref[:, pl.ds(c, 1), :].reshape(2 * nkh, CHUNK)
        gc_row2 = (jnp.dot(grow[:nkh], tri_t_l, precision=HIGHEST, preferred_element_type=F32)
                   + jnp.dot(grow[nkh:], tri_t_r, precision=HIGHEST, preferred_element_type=F32))

        for p0 in range(0, nkh, wave):
            prs = range(p0, min(p0 + wave, nkh))
            ks, qs, pp, tt_, aa, cols = {}, {}, {}, {}, {}, {}
            for p in prs:
                k = k_ref[p, rows, :]
                q = q_ref[p, rows, :]
                kq = _nt_dot(jnp.concatenate([k, q], axis=0), jnp.concatenate([k, k], axis=0))
                ha, hb = 2 * p, 2 * p + 1
                gca, gcb = cs_col[:, lane_g + ha:lane_g + ha + 1], cs_col[:, lane_g + hb:lane_g + hb + 1]
                bca, bcb = gcol[:, lane_b + ha:lane_b + ha + 1], gcol[:, lane_b + hb:lane_b + hb + 1]
                gcc2 = jnp.where(left, gca, gcb)
                bc2 = jnp.where(left, bca, bcb)
                decay2 = jnp.where(incl, jnp.exp(jnp.where(incl, gcc2 - gc_row2[p:p + 1, :], 0.0)), 0.0)
                pn = jnp.where(strict, -(kq[:CHUNK] * bc2) * decay2, 0.0)
                ks[p] = k.astype(F32)
                qs[p] = q.astype(F32)
                pp[p] = pn
                tt_[p] = eye2 + pn
                aa[p] = kq[CHUNK:] * decay2
                cols[p] = ((gca, bca), (gcb, bcb))

            for p in prs:
                pp[p] = jnp.dot(pp[p].astype(BF16), split_stack(pp[p]), preferred_element_type=F32)
            for _ in range(4):
                for p in prs:
                    y = jnp.dot(jnp.concatenate([pp[p], tt_[p]], axis=0).astype(BF16), split_stack(pp[p]),
                                preferred_element_type=F32)
                    pp[p] = y[:CHUNK]
                    tt_[p] = tt_[p] + y[CHUNK:]
            for p in prs:
                tt_[p] = tt_[p] + jnp.dot(tt_[p].astype(BF16), split_stack(pp[p]), preferred_element_type=F32)

            sols = {}
            for p in prs:
                rhs = []
                for e in range(2):
                    gch, bch = cols[p][e]
                    vf = v_ref[2 * p + e, rows, :].astype(F32)
                    rhs.append(jnp.concatenate([vf * bch, ks[p] * (bch * jnp.exp(gch))], axis=1))
                sols[p] = jnp.dot(split_stack(tt_[p]), jnp.concatenate(rhs, axis=0).astype(BF16),
                                  preferred_element_type=F32)

            wss, vns = {}, {}
            for p in prs:
                for e in range(2):
                    m = 2 * p + e
                    gch, _ = cols[p][e]
                    sol = sols[p][e * CHUNK:(e + 1) * CHUNK]
                    lhs = jnp.concatenate([sol[:, DV:], qs[p] * jnp.exp(gch)], axis=0)
                    wss[m] = _bdot(lhs, st[m])
                    vns[m] = sol[:, :DV] - wss[m][:CHUNK]

            for p in prs:
                if write_out:
                    oi = jnp.dot(split_stack(aa[p]), jnp.concatenate([vns[2 * p], vns[2 * p + 1]], axis=0).astype(BF16),
                                 preferred_element_type=F32)
                for e in range(2):
                    m = 2 * p + e
                    gch, _ = cols[p][e]
                    if write_out:
                        o_ref[m, rows, :] = (wss[m][CHUNK:] + oi[e * CHUNK:(e + 1) * CHUNK]).astype(o_ref.dtype)
                    gl = gch[tl:tl + 1, :]
                    kd = ks[p] * jnp.exp(gl - gch)
                    st[m] = st[m] * jnp.exp(gl) + _tn_dot(kd.astype(BF16), vns[m].astype(BF16))
        return carry

    lax.fori_loop(0, nch, chunk_body, 0)

    @pl.when(i == nt - 1)
    def _():
        st_ref[...] = st[...]


def _gdn(qh, kh, vh, gc, gr, s0, *, reverse, tb, write_out, name, wave=8):
    b, nk, s, _ = qh.shape
    nv = vh.shape[1]
    nt = s // tb
    nch = tb // CHUNK
    d = 1 if reverse else 0
    tb_of = (lambda i: nt - 1 - i) if reverse else (lambda i: i)
    in_specs = [
        pl.BlockSpec((None, nk, tb, DK), lambda bb, i: (bb, 0, tb_of(i), 0)),
        pl.BlockSpec((None, nk, tb, DK), lambda bb, i: (bb, 0, tb_of(i), 0)),
        pl.BlockSpec((None, nv, tb, DV), lambda bb, i: (bb, 0, tb_of(i), 0)),
        pl.BlockSpec((None, tb, LANES), lambda bb, i: (bb, tb_of(i), 0)),
        pl.BlockSpec((None, nv, nch, CHUNK), lambda bb, i: (bb, 2 + d, tb_of(i), 0)),
        pl.BlockSpec((None, nv, DK, DV), lambda bb, i: (bb, 0, 0, 0)),
    ]
    st_spec = pl.BlockSpec((None, nv, DK, DV), lambda bb, i: (bb, 0, 0, 0))
    st_shape = jax.ShapeDtypeStruct((b, nv, DK, DV), F32)
    if write_out:
        out_specs = [pl.BlockSpec((None, nv, tb, DV), lambda bb, i: (bb, 0, tb_of(i), 0)), st_spec]
        out_shape = [jax.ShapeDtypeStruct((b, nv, s, DV), BF16), st_shape]
    else:
        out_specs, out_shape = [st_spec], [st_shape]
    return pl.pallas_call(
        functools.partial(_gdn_kernel, nt=nt, tb=tb, reverse=reverse, write_out=write_out, wave=wave),
        grid=(b, nt), in_specs=in_specs, out_specs=out_specs, out_shape=out_shape,
        scratch_shapes=[pltpu.VMEM((nv, DK, DV), F32)],
        compiler_params=_cparams(("parallel", "arbitrary")), name=name,
    )(qh, kh, vh, gc, gr, s0)


def _gelu_tanh(x):
    return 0.5 * x * (1.0 + jnp.tanh(0.7978845608028654 * (x + 0.044715 * (x * x * x))))


def _aout_kernel(yf_ref, yb_ref, ag_ref, w_ref, o_ref, a_scr):
    @pl.when(pl.program_id(2) == 0)
    def _():
        y = yf_ref[...].astype(F32) + yb_ref[...].astype(F32)
        a_scr[...] = (y * _gelu_tanh(ag_ref[...].astype(F32))).astype(a_scr.dtype)

    o_ref[...] = jnp.dot(a_scr[...], w_ref[...], preferred_element_type=F32)


def _aout(yf, yb, ag, w, *, tm, tn):
    b, s, wa = yf.shape
    n = w.shape[1]
    act = pl.BlockSpec((None, tm, wa), lambda bb, i, j: (bb, i, 0))
    return pl.pallas_call(
        _aout_kernel, grid=(b, s // tm, n // tn),
        in_specs=[act, act, act, pl.BlockSpec((wa, tn), lambda bb, i, j: (0, j))],
        out_specs=pl.BlockSpec((None, tm, tn), lambda bb, i, j: (bb, i, j)),
        out_shape=jax.ShapeDtypeStruct((b, s, n), F32),
        scratch_shapes=[pltpu.VMEM((tm, wa), BF16)],
        compiler_params=_cparams(("parallel", "parallel", "arbitrary")), name="aout")(yf, yb, ag, w)


def _bout_kernel(of_ref, ob_ref, zl_ref, zh_ref, nw_ref, ya_ref, ga_ref, gb_ref, w_ref, o_ref, y_scr):
    ncol, tr = o_ref.shape[0], o_ref.shape[1]
    tm = ncol * tr

    @pl.when(pl.program_id(3) == 0)
    def _():
        nw = nw_ref[...]
        nvh = of_ref.shape[0]
        half = nvh // 2
        for h in range(nvh):
            o = (of_ref[h].astype(F32) + ob_ref[h].astype(F32)).reshape(tm, DV)
            z_ref = zl_ref if h < half else zh_ref
            hh = h % half
            z = z_ref[:, :, hh * DV:(hh + 1) * DV].astype(F32).reshape(tm, DV)
            y = o * lax.rsqrt(jnp.mean(o * o, axis=-1, keepdims=True) + EPS)
            y = y * nw * (z * jax.nn.sigmoid(z))
            y_scr[:, h * DV:(h + 1) * DV] = y.astype(y_scr.dtype)

    yb = jnp.dot(y_scr[...], w_ref[...], preferred_element_type=F32)
    for k in range(ncol):
        ga = jax.nn.sigmoid(ga_ref[k].astype(F32))
        gb = jax.nn.sigmoid(gb_ref[k].astype(F32))
        ya = ya_ref[:, k].reshape(tr, ya_ref.shape[-1])
        o_ref[k] = (ga * ya + gb * yb[k * tr:(k + 1) * tr]).astype(o_ref.dtype)


def _bout(of, ob, big, mg, nw, ya, w, *, z_col, tn):
    b, nv, s, dv = of.shape
    n = w.shape[1]
    rows = s // GRID_W
    tr = min(TILE_R, rows)
    hw = nv * dv // 2
    of5 = of.reshape(b, nv, GRID_W, rows, dv)
    ob5 = ob.reshape(b, nv, GRID_W, rows, dv)
    big4 = big.reshape(b, GRID_W, rows, big.shape[-1])
    mg4 = mg.reshape(b, GRID_W, rows, mg.shape[-1])
    ya5 = ya.reshape(b, rows // SUBLANES, GRID_W, SUBLANES, n)
    o_spec = pl.BlockSpec((None, nv, SUBLANES, tr, dv), lambda bb, wi, ri, j: (bb, 0, wi, ri, 0))
    out = pl.pallas_call(
        _bout_kernel, grid=(b, GRID_W // SUBLANES, rows // tr, n // tn),
        in_specs=[o_spec, o_spec,
                  pl.BlockSpec((None, SUBLANES, tr, hw), lambda bb, wi, ri, j: (bb, wi, ri, z_col // hw)),
                  pl.BlockSpec((None, SUBLANES, tr, hw), lambda bb, wi, ri, j: (bb, wi, ri, z_col // hw + 1)),
                  pl.BlockSpec((1, dv), lambda bb, wi, ri, j: (0, 0)),
                  pl.BlockSpec((None, tr // SUBLANES, SUBLANES, SUBLANES, tn), lambda bb, wi, ri, j: (bb, ri, wi, 0, j)),
                  pl.BlockSpec((None, SUBLANES, tr, tn), lambda bb, wi, ri, j: (bb, wi, ri, j)),
                  pl.BlockSpec((None, SUBLANES, tr, tn), lambda bb, wi, ri, j: (bb, wi, ri, n // tn + j)),
                  pl.BlockSpec((nv * dv, tn), lambda bb, wi, ri, j: (0, j))],
        out_specs=pl.BlockSpec((None, SUBLANES, tr, tn), lambda bb, wi, ri, j: (bb, wi, ri, j)),
        out_shape=jax.ShapeDtypeStruct((b, GRID_W, rows, n), BF16),
        scratch_shapes=[pltpu.VMEM((SUBLANES * tr, nv * dv), BF16)],
        compiler_params=_cparams(("parallel", "parallel", "parallel", "arbitrary")), name="bout",
    )(of5, ob5, big4, big4, nw, ya5, mg4, mg4, w)
    return out


def _ffn_kernel(m_ref, x_ref, gt1_ref, wo_ref, lnw_ref, sh_ref, sc_ref, gt2_ref, wg_ref, wu_ref, wd_ref, fw_ref,
                o_ref, x1_scr, h_scr, acc, *, nf):
    f = pl.program_id(3)
    ncol, tr = m_ref.shape[0], m_ref.shape[1]

    @pl.when(f == 0)
    def _():
        for k in range(ncol):
            rs = slice(k * tr, (k + 1) * tr)
            x1 = x_ref[:, k, :] + gt1_ref[...] * jnp.dot(m_ref[k], wo_ref[...], preferred_element_type=F32)
            x1_scr[rs, :] = x1
            h_scr[rs, :] = _rms_mod(x1, lnw_ref[...], 1.0 + sc_ref[...], sh_ref[...]).astype(h_scr.dtype)
        acc[...] = jnp.zeros_like(acc)

    h = h_scr[...]
    g = jnp.dot(h, wg_ref[...], preferred_element_type=F32)
    u = jnp.dot(h, wu_ref[...], preferred_element_type=F32)
    a = (g * jax.nn.sigmoid(g) * u).astype(BF16)
    acc[...] += jnp.dot(a, wd_ref[...], preferred_element_type=F32)

    @pl.when(f == nf - 1)
    def _():
        for k in range(ncol):
            rs = slice(k * tr, (k + 1) * tr)
            x2 = x1_scr[rs, :] + gt2_ref[...] * acc[rs, :]
            ms = jnp.mean(x2 * x2, axis=-1, keepdims=True)
            o_ref[:, k, :] = x2 * lax.rsqrt(ms + EPS) * fw_ref[...]


def _ffn(m4, x, gt1, w_o, lnw, sh, sc, gt2, w_in, w_dn, fw, *, tf):
    b, s, d = x.shape
    rows = s // GRID_W
    tr = min(TILE_R, rows)
    tm = SUBLANES * tr
    fh = w_dn.shape[0]
    nf = fh // tf
    x4 = x.reshape(b, rows, GRID_W, d)
    vec = pl.BlockSpec((None, 1, d), lambda bb, wi, ri, f: (bb, 0, 0))
    par = pl.BlockSpec((1, d), lambda bb, wi, ri, f: (0, 0))
    out = pl.pallas_call(
        functools.partial(_ffn_kernel, nf=nf), grid=(b, GRID_W // SUBLANES, rows // tr, nf),
        in_specs=[pl.BlockSpec((None, SUBLANES, tr, d), lambda bb, wi, ri, f: (bb, wi, ri, 0)),
                  pl.BlockSpec((None, tr, SUBLANES, d), lambda bb, wi, ri, f: (bb, ri, wi, 0)),
                  vec,
                  pl.BlockSpec((d, d), lambda bb, wi, ri, f: (0, 0), pipeline_mode=pl.Buffered(1)),
                  par, vec, vec, vec,
                  pl.BlockSpec((d, tf), lambda bb, wi, ri, f: (0, f)),
                  pl.BlockSpec((d, tf), lambda bb, wi, ri, f: (0, nf + f)),
                  pl.BlockSpec((tf, d), lambda bb, wi, ri, f: (f, 0)),
                  par],
        out_specs=pl.BlockSpec((None, tr, SUBLANES, d), lambda bb, wi, ri, f: (bb, ri, wi, 0)),
        out_shape=jax.ShapeDtypeStruct((b, rows, GRID_W, d), F32),
        scratch_shapes=[pltpu.VMEM((tm, d), F32), pltpu.VMEM((tm, d), BF16), pltpu.VMEM((tm, d), F32)],
        compiler_params=_cparams(("parallel", "parallel", "parallel", "arbitrary")), name="ffn",
    )(m4, x4, gt1, w_o, lnw, sh, sc, gt2, w_in, w_in, w_dn, fw)
    return out.reshape(b, s, d)


def _pad_pairs(t, axis):
    n = t.shape[axis]
    bw = n // NH_A
    shp = t.shape[:axis] + (NH_A // 2, 2 * bw) + t.shape[axis + 1:]
    t = t.reshape(shp)
    padw = [(0, 0)] * t.ndim
    padw[axis + 1] = (0, PAIR_W - 2 * bw)
    t = jnp.pad(t, padw)
    return t.reshape(t.shape[:axis] + (NH_A // 2 * PAIR_W,) + t.shape[axis + 2:])


def _pair_gate_weights(gw, gb):
    bw = gw.shape[1]
    npair = NH_A // 2
    g = gw.reshape(npair, 2, bw, 2, bw)
    wp = jnp.zeros((npair, PAIR_W, 2, PAIR_W), gw.dtype)
    for e in range(2):
        wp = wp.at[:, e * bw:(e + 1) * bw, :, e * bw:(e + 1) * bw].set(g[:, e])
    b = gb.reshape(npair, 2, 2, bw).transpose(0, 2, 1, 3).reshape(npair, 2, 2 * bw)
    b = jnp.pad(b, ((0, 0), (0, 0), (0, PAIR_W - 2 * bw)))
    return wp.reshape(npair, PAIR_W, 2 * PAIR_W).astype(BF16), b.reshape(npair, 1, 2 * PAIR_W)


def kernel(x, c, ctx, c_ctx, w_mod, b_mod, ln1_w, ln2_w, w_in, conv_a_w, conv_a_b, lru_gate_w, lru_gate_b, lru_lambda, conv_qkv_w, gdn_a_log, gdn_dt_bias, gdn_norm_w, w_a_out, w_b_out, w_out, w_ffn_in, w_ffn_out, final_norm_w):
    depth = w_mod.shape[0]
    assert depth == 1, "context stream updates are only needed for depth > 1"
    bsz, seq, d = x.shape
    lctx = ctx.shape[1]
    rows = seq // GRID_W
    assert seq % STRIP == 0 and rows % CHUNK == 0 and lctx % CHUNK == 0
    wa = lru_lambda.shape[-1]
    wap = NH_A // 2 * PAIR_W
    qk_dim, v_dim = NK * DK, NV * DV
    qkv_dim = 2 * qk_dim + v_dim
    l = 0
    tn = 512

    wi = w_in[l]
    o_qkv, o_z, o_g, o_mg = 2 * wa, 2 * wa + qkv_dim, 2 * wa + qkv_dim + v_dim, 2 * wa + qkv_dim + v_dim + 4 * NV
    w_axp = _pad_pairs(wi[:, :wa], 1).astype(BF16)
    w_agp = _pad_pairs(wi[:, wa:2 * wa], 1).astype(BF16)
    w_qz = wi[:, o_qkv:o_g].astype(BF16)
    w_gt = wi[:, o_g:o_mg].astype(BF16)
    w_mg = wi[:, o_mg:].astype(BF16)
    cw_a = _pad_pairs(conv_a_w[l], 1)
    cb_a = _pad_pairs(conv_a_b[l][None], 1)
    nla = _pad_pairs(-LRU_C * jax.nn.softplus(-lru_lambda[l]), 1)
    gate_w = [_pair_gate_weights(lru_gate_w[l, dd], lru_gate_b[l, dd]) for dd in range(2)]
    neg_a = -jnp.exp(gdn_a_log[l]).reshape(1, 2 * NV)
    na_l = jnp.concatenate([jnp.zeros((1, 2 * NV), F32), neg_a], axis=1)
    dt_l = jnp.concatenate([jnp.zeros((1, 2 * NV), F32), gdn_dt_bias[l].reshape(1, 2 * NV)], axis=1)
    w_a = _pad_pairs(w_a_out[l], 0).astype(BF16)
    w_b = w_b_out[l].astype(BF16)
    w_o = w_out[l].astype(BF16)
    w_f1 = w_ffn_in[l].astype(BF16)
    w_f2 = w_ffn_out[l].astype(BF16)

    nrow = -(-(bsz + 1) // SUBLANES) * SUBLANES
    cc = jnp.concatenate([c, c_ctx[None], jnp.zeros((nrow - bsz - 1, d), F32)], axis=0)
    mod = _mod(cc, w_mod[l], b_mod[l][None]).reshape(nrow, N_MOD, 1, d)
    sh1, sc1, gt1, sh2, sc2, gt2 = (mod[:bsz, k] for k in range(N_MOD))
    csh1 = jnp.broadcast_to(mod[bsz, 0], (bsz, 1, d))
    csc1 = jnp.broadcast_to(mod[bsz, 1], (bsz, 1, d))
    ln1 = ln1_w[l][None]

    nax = wap // tn
    ax, ag = _inproj(x, ln1, sh1, sc1, [(w_axp, tn, nax, BF16), (w_agp, tn, nax, BF16)],
                     order="strip", tm=min(2 * STRIP, seq), tt=STRIP, name="inproj_a")
    big, mg, gts = _inproj(x, ln1, sh1, sc1,
                           [(w_qz, tn, (qkv_dim + v_dim) // tn, BF16), (w_mg, tn, 2 * d // tn, BF16), (w_gt, LANES, 1, F32)],
                           order="colmajor", name="inproj_b")
    (cax,) = _inproj(ctx, ln1, csh1, csc1, [(w_axp, tn, nax, BF16)], order="strip", tm=lctx, tt=lctx, name="inproj_ca")
    cbig, cgts = _inproj(ctx, ln1, csh1, csc1, [(w_qz, tn, qkv_dim // tn, BF16), (w_gt, LANES, 1, F32)],
                         order="raster", tm=lctx, name="inproj_cb")

    ys = []
    for dd in range(2):
        wg, bg = gate_w[dd]
        h0 = jnp.zeros((bsz, 1, wap), F32)
        _, hc = _lru(cax, cw_a, cb_a, wg, bg, nla[dd:dd + 1], h0, reverse=bool(dd), tt=lctx, name=f"lru_c{dd}")
        y, _ = _lru(ax, cw_a, cb_a, wg, bg, nla[dd:dd + 1], hc, reverse=bool(dd), tt=STRIP, name=f"lru_x{dd}")
        ys.append(y)

    cw_q = conv_qkv_w[l]

    def prep_all(q_arr, g_arr, tt, tag):
        qh = _prep(q_arr, cw_q, mode="q", col0=0, nheads=NK, tt=tt, name="prep_q" + tag)
        kh = _prep(q_arr, cw_q, mode="k", col0=qk_dim // 512, nheads=NK, tt=tt, name="prep_k" + tag)
        vh = _prep(q_arr, cw_q, mode="v", col0=2 * qk_dim // 512, nheads=NV, tt=tt, name="prep_v" + tag)
        gc, gr = _gates(g_arr, na_l, dt_l, tt=tt)
        s_len = q_arr.shape[1]
        gr = gr.reshape(bsz, 4, NK, 2, s_len).transpose(0, 1, 3, 2, 4)
        return qh, kh, vh, gc, gr.reshape(bsz, LANES, s_len // CHUNK, CHUNK)

    pc = prep_all(cbig, cgts, lctx, "_c")
    px = prep_all(big, gts, min(512, seq), "_x")
    os_ = []
    for dd in range(2):
        s0 = jnp.zeros((bsz, NV, DK, DV), F32)
        (sc_state,) = _gdn(*pc, s0, reverse=bool(dd), tb=lctx, write_out=False, name=f"gdn_c{dd}")
        o, _ = _gdn(*px, sc_state, reverse=bool(dd), tb=min(512, seq), write_out=True, name=f"gdn_x{dd}")
        os_.append(o)

    ya = _aout(ys[0], ys[1], ag, w_a, tm=min(512, seq), tn=tn)
    m4 = _bout(os_[0], os_[1], big, mg, gdn_norm_w[l][None], ya, w_b, z_col=qkv_dim, tn=tn)
    return _ffn(m4, x, gt1, w_o, ln2_w[l][None], sh2, sc2, gt2, w_f1, w_f2, final_norm_w[None], tf=512)
```

```python
import functools

import jax
import jax.numpy as jnp
import numpy as np
from jax import lax
from jax.experimental import pallas as pl
from jax.experimental.pallas import tpu as pltpu

F32 = jnp.float32
BF16 = jnp.bfloat16
HIGHEST = lax.Precision.HIGHEST

EPS = 1e-6
GRID_W = 64
CONV_W = 4
N_MOD = 6
NH_A = 16
LRU_C = 8.0
NK, DK, NV, DV = 16, 128, 32, 128
CHUNK = 64
LANES = 128
SUBLANES = 8
HALO = 16
PAIR_W = 384
STRIP = SUBLANES * GRID_W
TILE_R = 64
VMEM_LIMIT = 60 * 1024 * 1024


def _cparams(sem):
    return pltpu.CompilerParams(dimension_semantics=sem, vmem_limit_bytes=VMEM_LIMIT)


def _nt_dot(a, b, **kw):
    return lax.dot_general(a, b, (((1,), (1,)), ((), ())), preferred_element_type=F32, **kw)


def _tn_dot(a, b):
    return lax.dot_general(a, b, (((0,), (0,)), ((), ())), preferred_element_type=F32)


def _bdot(a, b):
    return jnp.dot(a.astype(BF16), b.astype(BF16), preferred_element_type=F32)


def _mod_kernel(c_ref, w_ref, b_ref, o_ref):
    s = c_ref[...]
    s = s * jax.nn.sigmoid(s)
    o_ref[...] = jnp.dot(s, w_ref[...], precision=HIGHEST, preferred_element_type=F32) + b_ref[...]


def _mod(cc, w, b, tn=1024):
    m, d = cc.shape
    n = w.shape[1]
    return pl.pallas_call(
        _mod_kernel, grid=(n // tn,),
        in_specs=[pl.BlockSpec((m, d), lambda j: (0, 0)),
                  pl.BlockSpec((d, tn), lambda j: (0, j)),
                  pl.BlockSpec((1, tn), lambda j: (0, j))],
        out_specs=pl.BlockSpec((m, tn), lambda j: (0, j)),
        out_shape=jax.ShapeDtypeStruct((m, n), F32),
        compiler_params=_cparams(("parallel",)), name="mod")(cc, w, b)


def _rms_mod(xb, lnw, sc1, sh):
    xb = xb.astype(F32)
    ms = jnp.mean(xb * xb, axis=-1, keepdims=True)
    return (xb * lax.rsqrt(ms + EPS) * lnw) * sc1 + sh


def _strip_perm(tt):
    seg = tt // SUBLANES
    i = np.arange(tt)
    p = np.zeros((tt, tt), np.float32)
    p[i, (i % SUBLANES) * seg + i // SUBLANES] = 1.0
    return jnp.asarray(p, BF16)


def _colmajor_perm(rows):
    n = SUBLANES * rows
    i = np.arange(n)
    p = np.zeros((n, n), np.float32)
    p[i, (i % rows) * SUBLANES + i // rows] = 1.0
    return jnp.asarray(p, BF16)


def _inproj_kernel(*refs, order, tiles, tt):
    nseg = len(tiles)
    x_ref, lnw_ref, sh_ref, sc_ref = refs[:4]
    pos = 4 if order == "raster" else 5
    w_refs = refs[pos:pos + nseg]
    o_refs = refs[pos + nseg:pos + 2 * nseg]
    scr = refs[pos + 2 * nseg:]
    h_scr = scr[0]
    tm, d = h_scr.shape
    j = pl.program_id(2)

    @pl.when(j == 0)
    def _():
        lnw = lnw_ref[...]
        sc1 = 1.0 + sc_ref[...]
        sh = sh_ref[...]
        dst = h_scr if order == "raster" else scr[1]
        ch = min(tm, 256)
        for r0 in range(0, tm, ch):
            if order == "colmajor":
                xb = x_ref[r0 // SUBLANES:(r0 + ch) // SUBLANES].reshape(ch, d)
            else:
                xb = x_ref[r0:r0 + ch, :]
            dst[r0:r0 + ch, :] = _rms_mod(xb, lnw, sc1, sh).astype(dst.dtype)
        if order != "raster":
            perm_ref = refs[4]
            for r0 in range(0, tm, tt):
                for q0 in range(0, tt, ch):
                    h_scr[r0 + q0:r0 + q0 + ch, :] = jnp.dot(
                        perm_ref[q0:q0 + ch, :], dst[r0:r0 + tt, :], preferred_element_type=F32).astype(h_scr.dtype)

    lo = 0
    for k in range(nseg):
        @pl.when((j >= lo) & (j < lo + tiles[k]))
        def _(k=k):
            o_refs[k][...] = jnp.dot(h_scr[...], w_refs[k][...], preferred_element_type=F32).astype(o_refs[k].dtype)
        lo += tiles[k]


def _inproj(x, lnw, sh, sc, segs, *, order, tm=None, tt=None, name="inproj"):
    b, s, d = x.shape
    ins = [None, lnw, sh, sc]
    if order == "colmajor":
        rows = s // GRID_W
        tm = SUBLANES * rows
        ins[0] = x.reshape(b, rows, GRID_W, d)
        x_spec = pl.BlockSpec((None, rows, SUBLANES, d), lambda bb, i, j: (bb, 0, i, 0))
    else:
        ins[0] = x
        x_spec = pl.BlockSpec((None, tm, d), lambda bb, i, j: (bb, i, 0))
    in_specs = [x_spec,
                pl.BlockSpec((1, d), lambda bb, i, j: (0, 0)),
                pl.BlockSpec((None, 1, d), lambda bb, i, j: (bb, 0, 0)),
                pl.BlockSpec((None, 1, d), lambda bb, i, j: (bb, 0, 0))]
    scratch = [pltpu.VMEM((tm, d), BF16)]
    if order != "raster":
        if order == "colmajor":
            tt = tm
        ins.append(_strip_perm(tt) if order == "strip" else _colmajor_perm(s // GRID_W))
        in_specs.append(pl.BlockSpec((tt, tt), lambda bb, i, j: (0, 0)))
        scratch.append(pltpu.VMEM((tm, d), BF16))
    tiles = tuple(sg[2] for sg in segs)
    out_specs, out_shape = [], []
    lo = 0
    for w, tn, nt, dt in segs:
        clip = functools.partial(lambda j, lo, nt: jnp.clip(j - lo, 0, nt - 1), lo=lo, nt=nt)
        ins.append(w)
        in_specs.append(pl.BlockSpec((d, tn), lambda bb, i, j, clip=clip: (0, clip(j))))
        out_specs.append(pl.BlockSpec((None, tm, tn), lambda bb, i, j, clip=clip: (bb, i, clip(j))))
        out_shape.append(jax.ShapeDtypeStruct((b, s, tn * nt), dt))
        lo += nt
    return pl.pallas_call(
        functools.partial(_inproj_kernel, order=order, tiles=tiles, tt=tt),
        grid=(b, s // tm, lo), in_specs=in_specs, out_specs=out_specs, out_shape=out_shape,
        scratch_shapes=scratch,
        compiler_params=_cparams(("parallel", "parallel", "arbitrary")), name=name)(*ins)


def _lru_kernel(xm_ref, xp_ref, xn_ref, cw_ref, cb_ref, wg_ref, bg_ref, nla_ref, h0_ref,
                y_ref, ht_ref, pad, a_scr, b_scr, carry, *, nt, tt, reverse):
    i = pl.program_id(2)
    tb = nt - 1 - i if reverse else i
    seg = tt // SUBLANES
    gm = min(tt, 256)

    @pl.when(i == 0)
    def _():
        carry[...] = h0_ref[...]

    sub = lax.broadcasted_iota(jnp.int32, (SUBLANES, PAIR_W), 0)
    first, last = tb == 0, tb == nt - 1
    prev_last = jnp.where(first, 0.0, xp_ref[SUBLANES:2 * SUBLANES, :].astype(F32))
    nxt0 = jnp.where(last, 0.0, xn_ref[0:SUBLANES, :].astype(F32))
    nxt1 = jnp.where(last, 0.0, xn_ref[SUBLANES:2 * SUBLANES, :].astype(F32))
    x_end = xm_ref[tt - SUBLANES:tt, :].astype(F32)
    x_0 = xm_ref[0:SUBLANES, :].astype(F32)
    x_1 = xm_ref[SUBLANES:2 * SUBLANES, :].astype(F32)
    up = SUBLANES - 1
    pad[0:SUBLANES, :] = jnp.where(sub == 0, pltpu.roll(prev_last, 1, 0), pltpu.roll(x_end, 1, 0))
    pad[SUBLANES:SUBLANES + tt, :] = xm_ref[...].astype(F32)
    pad[SUBLANES + tt:2 * SUBLANES + tt, :] = jnp.where(sub == up, pltpu.roll(nxt0, up, 0), pltpu.roll(x_0, up, 0))
    pad[2 * SUBLANES + tt:3 * SUBLANES + tt, :] = jnp.where(sub == up, pltpu.roll(nxt1, up, 0), pltpu.roll(x_1, up, 0))

    for r0 in range(0, tt, gm):
        xc = cb_ref[...] + cw_ref[0:1, :] * pad[r0:r0 + gm, :]
        for k in range(1, CONV_W):
            xc = xc + cw_ref[k:k + 1, :] * pad[SUBLANES * k + r0:SUBLANES * k + r0 + gm, :]
        gates = _bdot(xc, wg_ref[...]) + bg_ref[...]
        r = jax.nn.sigmoid(gates[:, :PAIR_W])
        ig = jax.nn.sigmoid(gates[:, PAIR_W:])
        la = nla_ref[...] * r
        a = jnp.exp(la)
        a_scr[r0:r0 + gm, :] = a
        b_scr[r0:r0 + gm, :] = jnp.sqrt(-jnp.tanh(la) * (a * a + 1.0)) * (ig * xc)

    def scan_body(jj, hp):
        j = seg - 1 - jj if reverse else jj
        rows = pl.ds(pl.multiple_of(j * SUBLANES, SUBLANES), SUBLANES)
        a = a_scr[rows, :]
        h = a * hp[0] + b_scr[rows, :]
        p = a * hp[1]
        b_scr[rows, :] = h
        a_scr[rows, :] = p
        return h, p

    h, p = lax.fori_loop(0, seg, scan_body,
                         (jnp.zeros((SUBLANES, PAIR_W), F32), jnp.ones((SUBLANES, PAIR_W), F32)), unroll=8)

    cin = carry[...]
    cvec = jnp.zeros((SUBLANES, PAIR_W), F32)
    for s in (range(SUBLANES - 1, -1, -1) if reverse else range(SUBLANES)):
        cvec = jnp.where(sub == s, jnp.broadcast_to(cin, (SUBLANES, PAIR_W)), cvec)
        cin = p[s:s + 1, :] * cin + h[s:s + 1, :]
    carry[...] = cin
    cvec2 = jnp.concatenate([cvec, cvec], axis=0)

    def fix_body(j, _):
        rows = pl.ds(pl.multiple_of(j * 2 * SUBLANES, 2 * SUBLANES), 2 * SUBLANES)
        y_ref[rows, :] = (b_scr[rows, :] + a_scr[rows, :] * cvec2).astype(y_ref.dtype)
        return 0

    lax.fori_loop(0, seg // 2, fix_body, 0, unroll=4)

    @pl.when(i == nt - 1)
    def _():
        ht_ref[...] = carry[...]


def _lru(ax, cw, cb, wg, bg, nla, h0, *, reverse, tt, name):
    b, s, _ = ax.shape
    npair = cw.shape[1] // PAIR_W
    nt = s // tt
    nh = tt // HALO
    last_blk = s // HALO - 1
    tb_of = (lambda i: nt - 1 - i) if reverse else (lambda i: i)
    par = lambda bb, p, i: (0, p)
    return pl.pallas_call(
        functools.partial(_lru_kernel, nt=nt, tt=tt, reverse=reverse),
        grid=(b, npair, nt),
        in_specs=[
            pl.BlockSpec((None, tt, PAIR_W), lambda bb, p, i: (bb, tb_of(i), p)),
            pl.BlockSpec((None, HALO, PAIR_W), lambda bb, p, i: (bb, jnp.maximum(tb_of(i) * nh - 1, 0), p)),
            pl.BlockSpec((None, HALO, PAIR_W), lambda bb, p, i: (bb, jnp.minimum((tb_of(i) + 1) * nh, last_blk), p)),
            pl.BlockSpec((CONV_W, PAIR_W), par),
            pl.BlockSpec((1, PAIR_W), par),
            pl.BlockSpec((None, PAIR_W, 2 * PAIR_W), lambda bb, p, i: (p, 0, 0)),
            pl.BlockSpec((None, 1, 2 * PAIR_W), lambda bb, p, i: (p, 0, 0)),
            pl.BlockSpec((1, PAIR_W), par),
            pl.BlockSpec((None, 1, PAIR_W), lambda bb, p, i: (bb, 0, p)),
        ],
        out_specs=[pl.BlockSpec((None, tt, PAIR_W), lambda bb, p, i: (bb, tb_of(i), p)),
                   pl.BlockSpec((None, 1, PAIR_W), lambda bb, p, i: (bb, 0, p))],
        out_shape=[jax.ShapeDtypeStruct((b, s, npair * PAIR_W), BF16),
                   jax.ShapeDtypeStruct((b, 1, npair * PAIR_W), F32)],
        scratch_shapes=[pltpu.VMEM((tt + 3 * SUBLANES, PAIR_W), F32),
                        pltpu.VMEM((tt, PAIR_W), F32),
                        pltpu.VMEM((tt, PAIR_W), F32),
                        pltpu.VMEM((1, PAIR_W), F32)],
        compiler_params=_cparams(("parallel", "parallel", "arbitrary")), name=name,
    )(ax, ax, ax, cw, cb, wg, bg, nla, h0)


def _prep_kernel(xm_ref, xp_ref, xn_ref, cw_ref, o_ref, pad, *, nt, tt, mode):
    i = pl.program_id(1)
    pad[0:HALO, :] = jnp.where(i == 0, 0.0, xp_ref[...].astype(F32))
    pad[HALO:HALO + tt, :] = xm_ref[...].astype(F32)
    pad[HALO + tt:2 * HALO + tt, :] = jnp.where(i == nt - 1, 0.0, xn_ref[...].astype(F32))
    y = cw_ref[0:1, :] * pad[HALO - 1:HALO - 1 + tt, :]
    for k in range(1, CONV_W):
        y = y + cw_ref[k:k + 1, :] * pad[HALO - 1 + k:HALO - 1 + k + tt, :]
    y = y * jax.nn.sigmoid(y)
    for h in range(o_ref.shape[0]):
        yh = y[:, h * LANES:(h + 1) * LANES]
        if mode != "v":
            yh = yh * lax.rsqrt(jnp.sum(yh * yh, axis=-1, keepdims=True) + EPS)
        if mode == "q":
            yh = yh * (DK ** -0.5)
        o_ref[h] = yh.astype(o_ref.dtype)


def _prep(q_arr, cw, *, mode, col0, nheads, tt, name):
    b, s, _ = q_arr.shape
    nh = 4
    width = nh * LANES
    nt = s // tt
    nhalo = tt // HALO
    last_blk = s // HALO - 1
    return pl.pallas_call(
        functools.partial(_prep_kernel, nt=nt, tt=tt, mode=mode),
        grid=(b, nt, nheads // nh),
        in_specs=[pl.BlockSpec((None, tt, width), lambda bb, i, j: (bb, i, col0 + j)),
                  pl.BlockSpec((None, HALO, width), lambda bb, i, j: (bb, jnp.maximum(i * nhalo - 1, 0), col0 + j)),
                  pl.BlockSpec((None, HALO, width), lambda bb, i, j: (bb, jnp.minimum((i + 1) * nhalo, last_blk), col0 + j)),
                  pl.BlockSpec((CONV_W, width), lambda bb, i, j: (0, col0 + j))],
        out_specs=pl.BlockSpec((None, nh, tt, LANES), lambda bb, i, j: (bb, j, i, 0)),
        out_shape=jax.ShapeDtypeStruct((b, nheads, s, LANES), BF16),
        scratch_shapes=[pltpu.VMEM((tt + 2 * HALO, width), F32)],
        compiler_params=_cparams(("parallel", "parallel", "parallel")), name=name,
    )(q_arr, q_arr, q_arr, cw)


def _gates_kernel(x_ref, na_ref, dt_ref, oc_ref, or_ref):
    x = x_ref[...].astype(F32)
    lane = lax.broadcasted_iota(jnp.int32, x.shape, 1)
    z = x + dt_ref[...]
    sp = jnp.maximum(z, 0.0) + jnp.log1p(jnp.exp(-jnp.abs(z)))
    val = jnp.where(lane < 2 * NV, jax.nn.sigmoid(x), na_ref[...] * sp)
    oc_ref[...] = val
    or_ref[...] = val.T


def _gates(g_arr, na, dtb, *, tt):
    b, s, _ = g_arr.shape
    return pl.pallas_call(
        _gates_kernel, grid=(b, s // tt),
        in_specs=[pl.BlockSpec((None, tt, LANES), lambda bb, i: (bb, i, 0)),
                  pl.BlockSpec((1, LANES), lambda bb, i: (0, 0)),
                  pl.BlockSpec((1, LANES), lambda bb, i: (0, 0))],
        out_specs=[pl.BlockSpec((None, tt, LANES), lambda bb, i: (bb, i, 0)),
                   pl.BlockSpec((None, LANES, tt), lambda bb, i: (bb, 0, i))],
        out_shape=[jax.ShapeDtypeStruct((b, s, LANES), F32), jax.ShapeDtypeStruct((b, LANES, s), F32)],
        compiler_params=_cparams(("parallel", "parallel")), name="gdn_gates")(g_arr, na, dtb)


def _gdn_kernel(q_ref, k_ref, v_ref, gc_ref, gr_ref, s0_ref, *rest, nt, tb, reverse, write_out, wave):
    if write_out:
        o_ref, st_ref, st = rest
    else:
        st_ref, st = rest
    i = pl.program_id(1)
    nch = tb // CHUNK
    nkh = q_ref.shape[0]
    lane_b = NV if reverse else 0
    lane_g = 2 * NV + lane_b

    @pl.when(i == 0)
    def _():
        st[...] = s0_ref[...]

    c2 = 2 * CHUNK
    row = lax.broadcasted_iota(jnp.int32, (CHUNK, c2), 0)
    lane = lax.broadcasted_iota(jnp.int32, (CHUNK, c2), 1)
    colm = lane % CHUNK
    left = lane < CHUNK
    if reverse:
        incl, strict, tl = row <= colm, row < colm, 0
    else:
        incl, strict, tl = row >= colm, row > colm, CHUNK - 1
    eye2 = (row == colm).astype(F32)
    r64 = lax.broadcasted_iota(jnp.int32, (CHUNK, CHUNK), 0)
    c64 = lax.broadcasted_iota(jnp.int32, (CHUNK, CHUNK), 1)
    tri = (r64 <= c64 if reverse else r64 >= c64).astype(F32)
    tri_t = (c64 <= r64 if reverse else c64 >= r64).astype(F32)
    z64 = jnp.zeros((CHUNK, CHUNK), F32)
    tri_t_l = jnp.concatenate([tri_t, z64], axis=1)
    tri_t_r = jnp.concatenate([z64, tri_t], axis=1)
    r128 = lax.broadcasted_iota(jnp.int32, (c2, c2), 0)
    l128 = lax.broadcasted_iota(jnp.int32, (c2, c2), 1)
    diag_blocks = (r128 < CHUNK) == (l128 < CHUNK)

    def split_stack(x2):
        return jnp.where(diag_blocks, jnp.concatenate([x2, x2], axis=0), 0.0).astype(BF16)

    def chunk_body(cc, carry):
        c = nch - 1 - cc if reverse else cc
        rows = pl.ds(pl.multiple_of(c * CHUNK, CHUNK), CHUNK)
        gcol = gc_ref[rows, :]
        cs_col = jnp.dot(tri, gcol, precision=HIGHEST, preferred_element_type=F32)
        grow = gr_ref[:, pl.ds(c, 1), :].reshape(2 * nkh, CHUNK)
        gc_row2 = (jnp.dot(grow[:nkh], tri_t_l, precision=HIGHEST, preferred_element_type=F32)
                   + jnp.dot(grow[nkh:], tri_t_r, precision=HIGHEST, preferred_element_type=F32))

        for p0 in range(0, nkh, wave):
            prs = range(p0, min(p0 + wave, nkh))
            ks, qs, pp, tt_, aa, cols = {}, {}, {}, {}, {}, {}
            for p in prs:
                k = k_ref[p, rows, :]
                q = q_ref[p, rows, :]
                kq = _nt_dot(jnp.concatenate([k, q], axis=0), jnp.concatenate([k, k], axis=0))
                ha, hb = 2 * p, 2 * p + 1
                gca, gcb = cs_col[:, lane_g + ha:lane_g + ha + 1], cs_col[:, lane_g + hb:lane_g + hb + 1]
                bca, bcb = gcol[:, lane_b + ha:lane_b + ha + 1], gcol[:, lane_b + hb:lane_b + hb + 1]
                gcc2 = jnp.where(left, gca, gcb)
                bc2 = jnp.where(left, bca, bcb)
                decay2 = jnp.where(incl, jnp.exp(jnp.where(incl, gcc2 - gc_row2[p:p + 1, :], 0.0)), 0.0)
                pn = jnp.where(strict, -(kq[:CHUNK] * bc2) * decay2, 0.0)
                ks[p] = k.astype(F32)
                qs[p] = q.astype(F32)
                pp[p] = pn
                tt_[p] = eye2 + pn
                aa[p] = kq[CHUNK:] * decay2
                cols[p] = ((gca, bca), (gcb, bcb))

            for p in prs:
                pp[p] = jnp.dot(pp[p].astype(BF16), split_stack(pp[p]), preferred_element_type=F32)
            for _ in range(4):
                for p in prs:
                    y = jnp.dot(jnp.concatenate([pp[p], tt_[p]], axis=0).astype(BF16), split_stack(pp[p]),
                                preferred_element_type=F32)
                    pp[p] = y[:CHUNK]
                    tt_[p] = tt_[p] + y[CHUNK:]
            for p in prs:
                tt_[p] = tt_[p] + jnp.dot(tt_[p].astype(BF16), split_stack(pp[p]), preferred_element_type=F32)

            sols = {}
            for p in prs:
                rhs = []
                for e in range(2):
                    gch, bch = cols[p][e]
                    vf = v_ref[2 * p + e, rows, :].astype(F32)
                    rhs.append(jnp.concatenate([vf * bch, ks[p] * (bch * jnp.exp(gch))], axis=1))
                sols[p] = jnp.dot(split_stack(tt_[p]), jnp.concatenate(rhs, axis=0).astype(BF16),
                                  preferred_element_type=F32)

            wss, vns = {}, {}
            for p in prs:
                for e in range(2):
                    m = 2 * p + e
                    gch, _ = cols[p][e]
                    sol = sols[p][e * CHUNK:(e + 1) * CHUNK]
                    lhs = jnp.concatenate([sol[:, DV:], qs[p] * jnp.exp(gch)], axis=0)
                    wss[m] = _bdot(lhs, st[m])
                    vns[m] = sol[:, :DV] - wss[m][:CHUNK]

            for p in prs:
                if write_out:
                    oi = jnp.dot(split_stack(aa[p]), jnp.concatenate([vns[2 * p], vns[2 * p + 1]], axis=0).astype(BF16),
                                 preferred_element_type=F32)
                for e in range(2):
                    m = 2 * p + e
                    gch, _ = cols[p][e]
                    if write_out:
                        o_ref[m, rows, :] = (wss[m][CHUNK:] + oi[e * CHUNK:(e + 1) * CHUNK]).astype(o_ref.dtype)
                    gl = gch[tl:tl + 1, :]
                    kd = ks[p] * jnp.exp(gl - gch)
                    st[m] = st[m] * jnp.exp(gl) + _tn_dot(kd.astype(BF16), vns[m].astype(BF16))
        return carry

    lax.fori_loop(0, nch, chunk_body, 0)

    @pl.when(i == nt - 1)
    def _():
        st_ref[...] = st[...]


def _gdn(qh, kh, vh, gc, gr, s0, *, reverse, tb, write_out, name, wave=8):
    b, nk, s, _ = qh.shape
    nv = vh.shape[1]
    nt = s // tb
    nch = tb // CHUNK
    d = 1 if reverse else 0
    tb_of = (lambda i: nt - 1 - i) if reverse else (lambda i: i)
    in_specs = [
        pl.BlockSpec((None, nk, tb, DK), lambda bb, i: (bb, 0, tb_of(i), 0)),
        pl.BlockSpec((None, nk, tb, DK), lambda bb, i: (bb, 0, tb_of(i), 0)),
        pl.BlockSpec((None, nv, tb, DV), lambda bb, i: (bb, 0, tb_of(i), 0)),
        pl.BlockSpec((None, tb, LANES), lambda bb, i: (bb, tb_of(i), 0)),
        pl.BlockSpec((None, nv, nch, CHUNK), lambda bb, i: (bb, 2 + d, tb_of(i), 0)),
        pl.BlockSpec((None, nv, DK, DV), lambda bb, i: (bb, 0, 0, 0)),
    ]
    st_spec = pl.BlockSpec((None, nv, DK, DV), lambda bb, i: (bb, 0, 0, 0))
    st_shape = jax.ShapeDtypeStruct((b, nv, DK, DV), F32)
    if write_out:
        out_specs = [pl.BlockSpec((None, nv, tb, DV), lambda bb, i: (bb, 0, tb_of(i), 0)), st_spec]
        out_shape = [jax.ShapeDtypeStruct((b, nv, s, DV), BF16), st_shape]
    else:
        out_specs, out_shape = [st_spec], [st_shape]
    return pl.pallas_call(
        functools.partial(_gdn_kernel, nt=nt, tb=tb, reverse=reverse, write_out=write_out, wave=wave),
        grid=(b, nt), in_specs=in_specs, out_specs=out_specs, out_shape=out_shape,
        scratch_shapes=[pltpu.VMEM((nv, DK, DV), F32)],
        compiler_params=_cparams(("parallel", "arbitrary")), name=name,
    )(qh, kh, vh, gc, gr, s0)


def _gelu_tanh(x):
    return 0.5 * x * (1.0 + jnp.tanh(0.7978845608028654 * (x + 0.044715 * (x * x * x))))


def _aout_kernel(yf_ref, yb_ref, ag_ref, w_ref, o_ref, a_scr):
    @pl.when(pl.program_id(2) == 0)
    def _():
        y = yf_ref[...].astype(F32) + yb_ref[...].astype(F32)
        a_scr[...] = (y * _gelu_tanh(ag_ref[...].astype(F32))).astype(a_scr.dtype)

    o_ref[...] = jnp.dot(a_scr[...], w_ref[...], preferred_element_type=F32)


def _aout(yf, yb, ag, w, *, tm, tn):
    b, s, wa = yf.shape
    n = w.shape[1]
    act = pl.BlockSpec((None, tm, wa), lambda bb, i, j: (bb, i, 0))
    return pl.pallas_call(
        _aout_kernel, grid=(b, s // tm, n // tn),
        in_specs=[act, act, act, pl.BlockSpec((wa, tn), lambda bb, i, j: (0, j))],
        out_specs=pl.BlockSpec((None, tm, tn), lambda bb, i, j: (bb, i, j)),
        out_shape=jax.ShapeDtypeStruct((b, s, n), F32),
        scratch_shapes=[pltpu.VMEM((tm, wa), BF16)],
        compiler_params=_cparams(("parallel", "parallel", "arbitrary")), name="aout")(yf, yb, ag, w)


def _bout_kernel(of_ref, ob_ref, zl_ref, zh_ref, nw_ref, ya_ref, ga_ref, gb_ref, w_ref, o_ref, y_scr):
    ncol, tr = o_ref.shape[0], o_ref.shape[1]
    tm = ncol * tr

    @pl.when(pl.program_id(3) == 0)
    def _():
        nw = nw_ref[...]
        nvh = of_ref.shape[0]
        half = nvh // 2
        for h in range(nvh):
            o = (of_ref[h].astype(F32) + ob_ref[h].astype(F32)).reshape(tm, DV)
            z_ref = zl_ref if h < half else zh_ref
            hh = h % half
            z = z_ref[:, :, hh * DV:(hh + 1) * DV].astype(F32).reshape(tm, DV)
            y = o * lax.rsqrt(jnp.mean(o * o, axis=-1, keepdims=True) + EPS)
            y = y * nw * (z * jax.nn.sigmoid(z))
            y_scr[:, h * DV:(h + 1) * DV] = y.astype(y_scr.dtype)

    yb = jnp.dot(y_scr[...], w_ref[...], preferred_element_type=F32)
    for k in range(ncol):
        ga = jax.nn.sigmoid(ga_ref[k].astype(F32))
        gb = jax.nn.sigmoid(gb_ref[k].astype(F32))
        ya = ya_ref[:, k].reshape(tr, ya_ref.shape[-1])
        o_ref[k] = (ga * ya + gb * yb[k * tr:(k + 1) * tr]).astype(o_ref.dtype)


def _bout(of, ob, big, mg, nw, ya, w, *, z_col, tn):
    b, nv, s, dv = of.shape
    n = w.shape[1]
    rows = s // GRID_W
    tr = min(TILE_R, rows)
    hw = nv * dv // 2
    of5 = of.reshape(b, nv, GRID_W, rows, dv)
    ob5 = ob.reshape(b, nv, GRID_W, rows, dv)
    big4 = big.reshape(b, GRID_W, rows, big.shape[-1])
    mg4 = mg.reshape(b, GRID_W, rows, mg.shape[-1])
    ya5 = ya.reshape(b, rows // SUBLANES, GRID_W, SUBLANES, n)
    o_spec = pl.BlockSpec((None, nv, SUBLANES, tr, dv), lambda bb, wi, ri, j: (bb, 0, wi, ri, 0))
    out = pl.pallas_call(
        _bout_kernel, grid=(b, GRID_W // SUBLANES, rows // tr, n // tn),
        in_specs=[o_spec, o_spec,
                  pl.BlockSpec((None, SUBLANES, tr, hw), lambda bb, wi, ri, j: (bb, wi, ri, z_col // hw)),
                  pl.BlockSpec((None, SUBLANES, tr, hw), lambda bb, wi, ri, j: (bb, wi, ri, z_col // hw + 1)),
                  pl.BlockSpec((1, dv), lambda bb, wi, ri, j: (0, 0)),
                  pl.BlockSpec((None, tr // SUBLANES, SUBLANES, SUBLANES, tn), lambda bb, wi, ri, j: (bb, ri, wi, 0, j)),
                  pl.BlockSpec((None, SUBLANES, tr, tn), lambda bb, wi, ri, j: (bb, wi, ri, j)),
                  pl.BlockSpec((None, SUBLANES, tr, tn), lambda bb, wi, ri, j: (bb, wi, ri, n // tn + j)),
                  pl.BlockSpec((nv * dv, tn), lambda bb, wi, ri, j: (0, j))],
        out_specs=pl.BlockSpec((None, SUBLANES, tr, tn), lambda bb, wi, ri, j: (bb, wi, ri, j)),
        out_shape=jax.ShapeDtypeStruct((b, GRID_W, rows, n), BF16),
        scratch_shapes=[pltpu.VMEM((SUBLANES * tr, nv * dv), BF16)],
        compiler_params=_cparams(("parallel", "parallel", "parallel", "arbitrary")), name="bout",
    )(of5, ob5, big4, big4, nw, ya5, mg4, mg4, w)
    return out


def _ffn_kernel(m_ref, x_ref, perm_ref, gt1_ref, wo_ref, lnw_ref, sh_ref, sc_ref, gt2_ref, wg_ref, wu_ref, wd_ref, fw_ref,
                o_ref, x1_scr, h_scr, acc, *, nf):
    f = pl.program_id(3)
    ncol, tr = m_ref.shape[0], m_ref.shape[1]
    tm, d = h_scr.shape
    ch = min(tm, 256)

    @pl.when(f == 0)
    def _():
        mr = jnp.dot(perm_ref[...], m_ref[...].reshape(tm, d), preferred_element_type=F32).astype(BF16)
        x1_scr[...] = jnp.dot(mr, wo_ref[...], preferred_element_type=F32)
        for r0 in range(0, tm, ch):
            xb = x_ref[r0 // ncol:(r0 + ch) // ncol].reshape(ch, d)
            x1 = xb + gt1_ref[...] * x1_scr[r0:r0 + ch, :]
            x1_scr[r0:r0 + ch, :] = x1
            h_scr[r0:r0 + ch, :] = _rms_mod(x1, lnw_ref[...], 1.0 + sc_ref[...], sh_ref[...]).astype(h_scr.dtype)
        acc[...] = jnp.zeros_like(acc)

    h = h_scr[...]
    g = jnp.dot(h, wg_ref[...], preferred_element_type=F32)
    u = jnp.dot(h, wu_ref[...], preferred_element_type=F32)
    a = (g * jax.nn.sigmoid(g) * u).astype(BF16)
    acc[...] += jnp.dot(a, wd_ref[...], preferred_element_type=F32)

    @pl.when(f == nf - 1)
    def _():
        for r0 in range(0, tm, ch):
            x2 = x1_scr[r0:r0 + ch, :] + gt2_ref[...] * acc[r0:r0 + ch, :]
            ms = jnp.mean(x2 * x2, axis=-1, keepdims=True)
            o_ref[r0 // ncol:(r0 + ch) // ncol] = (x2 * lax.rsqrt(ms + EPS) * fw_ref[...]).reshape(ch // ncol, ncol, d)


def _ffn(m4, x, gt1, w_o, lnw, sh, sc, gt2, w_in, w_dn, fw, *, tf):
    b, s, d = x.shape
    rows = s // GRID_W
    tr = min(TILE_R, rows)
    tm = SUBLANES * tr
    fh = w_dn.shape[0]
    nf = fh // tf
    x4 = x.reshape(b, rows, GRID_W, d)
    vec = pl.BlockSpec((None, 1, d), lambda bb, wi, ri, f: (bb, 0, 0))
    par = pl.BlockSpec((1, d), lambda bb, wi, ri, f: (0, 0))
    out = pl.pallas_call(
        functools.partial(_ffn_kernel, nf=nf), grid=(b, GRID_W // SUBLANES, rows // tr, nf),
        in_specs=[pl.BlockSpec((None, SUBLANES, tr, d), lambda bb, wi, ri, f: (bb, wi, ri, 0)),
                  pl.BlockSpec((None, tr, SUBLANES, d), lambda bb, wi, ri, f: (bb, ri, wi, 0)),
                  pl.BlockSpec((tm, tm), lambda bb, wi, ri, f: (0, 0)),
                  vec,
                  pl.BlockSpec((d, d), lambda bb, wi, ri, f: (0, 0), pipeline_mode=pl.Buffered(1)),
                  par, vec, vec, vec,
                  pl.BlockSpec((d, tf), lambda bb, wi, ri, f: (0, f)),
                  pl.BlockSpec((d, tf), lambda bb, wi, ri, f: (0, nf + f)),
                  pl.BlockSpec((tf, d), lambda bb, wi, ri, f: (f, 0)),
                  par],
        out_specs=pl.BlockSpec((None, tr, SUBLANES, d), lambda bb, wi, ri, f: (bb, ri, wi, 0)),
        out_shape=jax.ShapeDtypeStruct((b, rows, GRID_W, d), F32),
        scratch_shapes=[pltpu.VMEM((tm, d), F32), pltpu.VMEM((tm, d), BF16), pltpu.VMEM((tm, d), F32)],
        compiler_params=_cparams(("parallel", "parallel", "parallel", "arbitrary")), name="ffn",
    )(m4, x4, _strip_perm(tm), gt1, w_o, lnw, sh, sc, gt2, w_in, w_in, w_dn, fw)
    return out.reshape(b, s, d)


def _pad_pairs(t, axis):
    n = t.shape[axis]
    bw = n // NH_A
    shp = t.shape[:axis] + (NH_A // 2, 2 * bw) + t.shape[axis + 1:]
    t = t.reshape(shp)
    padw = [(0, 0)] * t.ndim
    padw[axis + 1] = (0, PAIR_W - 2 * bw)
    t = jnp.pad(t, padw)
    return t.reshape(t.shape[:axis] + (NH_A // 2 * PAIR_W,) + t.shape[axis + 2:])


def _pair_gate_weights(gw, gb):
    bw = gw.shape[1]
    npair = NH_A // 2
    g = gw.reshape(npair, 2, bw, 2, bw)
    blocks = []
    for t in range(2):
        top = jnp.pad(g[:, 0, :, t, :], ((0, 0), (0, 0), (0, PAIR_W - bw)))
        bot = jnp.pad(g[:, 1, :, t, :], ((0, 0), (0, PAIR_W - 2 * bw), (bw, PAIR_W - 2 * bw)))
        blocks.append(jnp.concatenate([top, bot], axis=1))
    wp = jnp.concatenate(blocks, axis=2)
    b = gb.reshape(npair, 2, 2, bw).transpose(0, 2, 1, 3).reshape(npair, 2, 2 * bw)
    b = jnp.pad(b, ((0, 0), (0, 0), (0, PAIR_W - 2 * bw)))
    return wp.astype(BF16), b.reshape(npair, 1, 2 * PAIR_W)


def kernel(x, c, ctx, c_ctx, w_mod, b_mod, ln1_w, ln2_w, w_in, conv_a_w, conv_a_b, lru_gate_w, lru_gate_b, lru_lambda, conv_qkv_w, gdn_a_log, gdn_dt_bias, gdn_norm_w, w_a_out, w_b_out, w_out, w_ffn_in, w_ffn_out, final_norm_w):
    depth = w_mod.shape[0]
    assert depth == 1, "context stream updates are only needed for depth > 1"
    bsz, seq, d = x.shape
    lctx = ctx.shape[1]
    rows = seq // GRID_W
    assert seq % STRIP == 0 and rows % CHUNK == 0 and lctx % CHUNK == 0
    wa = lru_lambda.shape[-1]
    wap = NH_A // 2 * PAIR_W
    qk_dim, v_dim = NK * DK, NV * DV
    qkv_dim = 2 * qk_dim + v_dim
    l = 0
    tn = 512

    wi = w_in[l]
    o_qkv, o_z, o_g, o_mg = 2 * wa, 2 * wa + qkv_dim, 2 * wa + qkv_dim + v_dim, 2 * wa + qkv_dim + v_dim + 4 * NV
    w_axp = _pad_pairs(wi[:, :wa], 1).astype(BF16)
    w_agp = _pad_pairs(wi[:, wa:2 * wa], 1).astype(BF16)
    w_qz = wi[:, o_qkv:o_g].astype(BF16)
    w_gt = wi[:, o_g:o_mg].astype(BF16)
    w_mg = wi[:, o_mg:].astype(BF16)
    cw_a = _pad_pairs(conv_a_w[l], 1)
    cb_a = _pad_pairs(conv_a_b[l][None], 1)
    nla = _pad_pairs(-LRU_C * jax.nn.softplus(-lru_lambda[l]), 1)
    gate_w = [_pair_gate_weights(lru_gate_w[l, dd], lru_gate_b[l, dd]) for dd in range(2)]
    neg_a = -jnp.exp(gdn_a_log[l]).reshape(1, 2 * NV)
    na_l = jnp.concatenate([jnp.zeros((1, 2 * NV), F32), neg_a], axis=1)
    dt_l = jnp.concatenate([jnp.zeros((1, 2 * NV), F32), gdn_dt_bias[l].reshape(1, 2 * NV)], axis=1)
    w_a = _pad_pairs(w_a_out[l], 0).astype(BF16)
    w_b = w_b_out[l].astype(BF16)
    w_o = w_out[l].astype(BF16)
    w_f1 = w_ffn_in[l].astype(BF16)
    w_f2 = w_ffn_out[l].astype(BF16)

    nrow = -(-(bsz + 1) // SUBLANES) * SUBLANES
    cc = jnp.concatenate([c, c_ctx[None], jnp.zeros((nrow - bsz - 1, d), F32)], axis=0)
    mod = _mod(cc, w_mod[l], b_mod[l][None]).reshape(nrow, N_MOD, 1, d)
    sh1, sc1, gt1, sh2, sc2, gt2 = (mod[:bsz, k] for k in range(N_MOD))
    csh1 = jnp.broadcast_to(mod[bsz, 0], (bsz, 1, d))
    csc1 = jnp.broadcast_to(mod[bsz, 1], (bsz, 1, d))
    ln1 = ln1_w[l][None]

    nax = wap // tn
    ax, ag = _inproj(x, ln1, sh1, sc1, [(w_axp, tn, nax, BF16), (w_agp, tn, nax, BF16)],
                     order="strip", tm=min(2 * STRIP, seq), tt=STRIP, name="inproj_a")
    big, mg, gts = _inproj(x, ln1, sh1, sc1,
                           [(w_qz, tn, (qkv_dim + v_dim) // tn, BF16), (w_mg, tn, 2 * d // tn, BF16), (w_gt, LANES, 1, F32)],
                           order="colmajor", name="inproj_b")
    (cax,) = _inproj(ctx, ln1, csh1, csc1, [(w_axp, tn, nax, BF16)], order="strip", tm=lctx, tt=lctx, name="inproj_ca")
    cbig, cgts = _inproj(ctx, ln1, csh1, csc1, [(w_qz, tn, qkv_dim // tn, BF16), (w_gt, LANES, 1, F32)],
                         order="raster", tm=lctx, name="inproj_cb")

    ys = []
    for dd in range(2):
        wg, bg = gate_w[dd]
        h0 = jnp.zeros((bsz, 1, wap), F32)
        _, hc = _lru(cax, cw_a, cb_a, wg, bg, nla[dd:dd + 1], h0, reverse=bool(dd), tt=lctx, name=f"lru_c{dd}")
        y, _ = _lru(ax, cw_a, cb_a, wg, bg, nla[dd:dd + 1], hc, reverse=bool(dd), tt=STRIP, name=f"lru_x{dd}")
        ys.append(y)

    cw_q = conv_qkv_w[l]

    def prep_all(q_arr, g_arr, tt, tag):
        qh = _prep(q_arr, cw_q, mode="q", col0=0, nheads=NK, tt=tt, name="prep_q" + tag)
        kh = _prep(q_arr, cw_q, mode="k", col0=qk_dim // 512, nheads=NK, tt=tt, name="prep_k" + tag)
        vh = _prep(q_arr, cw_q, mode="v", col0=2 * qk_dim // 512, nheads=NV, tt=tt, name="prep_v" + tag)
        gc, gr = _gates(g_arr, na_l, dt_l, tt=tt)
        s_len = q_arr.shape[1]
        gr = gr.reshape(bsz, 4, NK, 2, s_len).transpose(0, 1, 3, 2, 4)
        return qh, kh, vh, gc, gr.reshape(bsz, LANES, s_len // CHUNK, CHUNK)

    pc = prep_all(cbig, cgts, lctx, "_c")
    px = prep_all(big, gts, min(512, seq), "_x")
    os_ = []
    for dd in range(2):
        s0 = jnp.zeros((bsz, NV, DK, DV), F32)
        (sc_state,) = _gdn(*pc, s0, reverse=bool(dd), tb=lctx, write_out=False, name=f"gdn_c{dd}")
        o, _ = _gdn(*px, sc_state, reverse=bool(dd), tb=min(512, seq), write_out=True, name=f"gdn_x{dd}")
        os_.append(o)

    ya = _aout(ys[0], ys[1], ag, w_a, tm=min(512, seq), tn=tn)
    m4 = _bout(os_[0], os_[1], big, mg, gdn_norm_w[l][None], ya, w_b, z_col=qkv_dim, tn=tn)
    return _ffn(m4, x, gt1, w_o, ln2_w[l][None], sh2, sc2, gt2, w_f1, w_f2, final_norm_w[None], tf=512)
```

```python
import functools

import jax
import jax.numpy as jnp
import numpy as np
from jax import lax
from jax.experimental import pallas as pl
from jax.experimental.pallas import tpu as pltpu

F32 = jnp.float32
BF16 = jnp.bfloat16
HIGHEST = lax.Precision.HIGHEST

EPS = 1e-6
GRID_W = 64
CONV_W = 4
N_MOD = 6
NH_A = 16
LRU_C = 8.0
NK, DK, NV, DV = 16, 128, 32, 128
CHUNK = 64
LANES = 128
SUBLANES = 8
HALO = 16
PAIR_W = 384
STRIP = SUBLANES * GRID_W
TILE_R = 64
FFN_TILE_R = 128
VMEM_LIMIT = 63 * 1024 * 1024


def _cparams(sem):
    return pltpu.CompilerParams(dimension_semantics=sem, vmem_limit_bytes=VMEM_LIMIT)


def _nt_dot(a, b, **kw):
    return lax.dot_general(a, b, (((1,), (1,)), ((), ())), preferred_element_type=F32, **kw)


def _tn_dot(a, b):
    return lax.dot_general(a, b, (((0,), (0,)), ((), ())), preferred_element_type=F32)


def _bdot(a, b):
    return jnp.dot(a.astype(BF16), b.astype(BF16), preferred_element_type=F32)


def _mod_kernel(c_ref, w_ref, b_ref, o_ref):
    s = c_ref[...]
    s = s * jax.nn.sigmoid(s)
    o_ref[...] = jnp.dot(s, w_ref[...], precision=HIGHEST, preferred_element_type=F32) + b_ref[...]


def _mod(cc, w, b, tn=1024):
    m, d = cc.shape
    n = w.shape[1]
    return pl.pallas_call(
        _mod_kernel, grid=(n // tn,),
        in_specs=[pl.BlockSpec((m, d), lambda j: (0, 0)),
                  pl.BlockSpec((d, tn), lambda j: (0, j)),
                  pl.BlockSpec((1, tn), lambda j: (0, j))],
        out_specs=pl.BlockSpec((m, tn), lambda j: (0, j)),
        out_shape=jax.ShapeDtypeStruct((m, n), F32),
        compiler_params=_cparams(("parallel",)), name="mod")(cc, w, b)


def _rms_mod(xb, lnw, sc1, sh):
    xb = xb.astype(F32)
    ms = jnp.mean(xb * xb, axis=-1, keepdims=True)
    return (xb * lax.rsqrt(ms + EPS) * lnw) * sc1 + sh


def _strip_perm(tt):
    seg = tt // SUBLANES
    i = np.arange(tt)
    p = np.zeros((tt, tt), np.float32)
    p[i, (i % SUBLANES) * seg + i // SUBLANES] = 1.0
    return jnp.asarray(p, BF16)


def _colmajor_perm(rows):
    n = SUBLANES * rows
    i = np.arange(n)
    p = np.zeros((n, n), np.float32)
    p[i, (i % rows) * SUBLANES + i // rows] = 1.0
    return jnp.asarray(p, BF16)


def _inproj_kernel(*refs, order, tiles, tt):
    nseg = len(tiles)
    x_ref, lnw_ref, sh_ref, sc_ref = refs[:4]
    pos = 4 if order == "raster" else 5
    w_refs = refs[pos:pos + nseg]
    o_refs = refs[pos + nseg:pos + 2 * nseg]
    scr = refs[pos + 2 * nseg:]
    h_scr = scr[0]
    tm, d = h_scr.shape
    j = pl.program_id(2)

    @pl.when(j == 0)
    def _():
        lnw = lnw_ref[...]
        sc1 = 1.0 + sc_ref[...]
        sh = sh_ref[...]
        dst = h_scr if order == "raster" else scr[1]
        ch = min(tm, 256)
        for r0 in range(0, tm, ch):
            if order == "colmajor":
                xb = x_ref[r0 // SUBLANES:(r0 + ch) // SUBLANES].reshape(ch, d)
            else:
                xb = x_ref[r0:r0 + ch, :]
            dst[r0:r0 + ch, :] = _rms_mod(xb, lnw, sc1, sh).astype(dst.dtype)
        if order != "raster":
            perm_ref = refs[4]
            for r0 in range(0, tm, tt):
                for q0 in range(0, tt, ch):
                    h_scr[r0 + q0:r0 + q0 + ch, :] = jnp.dot(
                        perm_ref[q0:q0 + ch, :], dst[r0:r0 + tt, :], preferred_element_type=F32).astype(h_scr.dtype)

    lo = 0
    for k in range(nseg):
        @pl.when((j >= lo) & (j < lo + tiles[k]))
        def _(k=k):
            o_refs[k][...] = jnp.dot(h_scr[...], w_refs[k][...], preferred_element_type=F32).astype(o_refs[k].dtype)
        lo += tiles[k]


def _inproj(x, lnw, sh, sc, segs, *, order, tm=None, tt=None, name="inproj"):
    b, s, d = x.shape
    ins = [None, lnw, sh, sc]
    if order == "colmajor":
        rows = s // GRID_W
        tm = SUBLANES * rows
        ins[0] = x.reshape(b, rows, GRID_W, d)
        x_spec = pl.BlockSpec((None, rows, SUBLANES, d), lambda bb, i, j: (bb, 0, i, 0))
    else:
        ins[0] = x
        x_spec = pl.BlockSpec((None, tm, d), lambda bb, i, j: (bb, i, 0))
    in_specs = [x_spec,
                pl.BlockSpec((1, d), lambda bb, i, j: (0, 0)),
                pl.BlockSpec((None, 1, d), lambda bb, i, j: (bb, 0, 0)),
                pl.BlockSpec((None, 1, d), lambda bb, i, j: (bb, 0, 0))]
    scratch = [pltpu.VMEM((tm, d), BF16)]
    if order != "raster":
        if order == "colmajor":
            tt = tm
        ins.append(_strip_perm(tt) if order == "strip" else _colmajor_perm(s // GRID_W))
        in_specs.append(pl.BlockSpec((tt, tt), lambda bb, i, j: (0, 0)))
        scratch.append(pltpu.VMEM((tm, d), BF16))
    tiles = tuple(sg[2] for sg in segs)
    out_specs, out_shape = [], []
    lo = 0
    for w, tn, nt, dt in segs:
        clip = functools.partial(lambda j, lo, nt: jnp.clip(j - lo, 0, nt - 1), lo=lo, nt=nt)
        ins.append(w)
        in_specs.append(pl.BlockSpec((d, tn), lambda bb, i, j, clip=clip: (0, clip(j))))
        out_specs.append(pl.BlockSpec((None, tm, tn), lambda bb, i, j, clip=clip: (bb, i, clip(j))))
        out_shape.append(jax.ShapeDtypeStruct((b, s, tn * nt), dt))
        lo += nt
    return pl.pallas_call(
        functools.partial(_inproj_kernel, order=order, tiles=tiles, tt=tt),
        grid=(b, s // tm, lo), in_specs=in_specs, out_specs=out_specs, out_shape=out_shape,
        scratch_shapes=scratch,
        compiler_params=_cparams(("parallel", "parallel", "arbitrary")), name=name)(*ins)


def _lru_kernel(xm_ref, xp_ref, xn_ref, cw_ref, cb_ref, wg_ref, bg_ref, nla_ref, h0_ref,
                y_ref, ht_ref, pad, a_scr, b_scr, carry, *, nt, tt, reverse):
    i = pl.program_id(2)
    tb = nt - 1 - i if reverse else i
    seg = tt // SUBLANES
    gm = min(tt, 256)

    @pl.when(i == 0)
    def _():
        carry[...] = h0_ref[...]

    sub = lax.broadcasted_iota(jnp.int32, (SUBLANES, PAIR_W), 0)
    first, last = tb == 0, tb == nt - 1
    prev_last = jnp.where(first, 0.0, xp_ref[SUBLANES:2 * SUBLANES, :].astype(F32))
    nxt0 = jnp.where(last, 0.0, xn_ref[0:SUBLANES, :].astype(F32))
    nxt1 = jnp.where(last, 0.0, xn_ref[SUBLANES:2 * SUBLANES, :].astype(F32))
    x_end = xm_ref[tt - SUBLANES:tt, :].astype(F32)
    x_0 = xm_ref[0:SUBLANES, :].astype(F32)
    x_1 = xm_ref[SUBLANES:2 * SUBLANES, :].astype(F32)
    up = SUBLANES - 1
    pad[0:SUBLANES, :] = jnp.where(sub == 0, pltpu.roll(prev_last, 1, 0), pltpu.roll(x_end, 1, 0))
    pad[SUBLANES:SUBLANES + tt, :] = xm_ref[...].astype(F32)
    pad[SUBLANES + tt:2 * SUBLANES + tt, :] = jnp.where(sub == up, pltpu.roll(nxt0, up, 0), pltpu.roll(x_0, up, 0))
    pad[2 * SUBLANES + tt:3 * SUBLANES + tt, :] = jnp.where(sub == up, pltpu.roll(nxt1, up, 0), pltpu.roll(x_1, up, 0))

    for r0 in range(0, tt, gm):
        xc = cb_ref[...] + cw_ref[0:1, :] * pad[r0:r0 + gm, :]
        for k in range(1, CONV_W):
            xc = xc + cw_ref[k:k + 1, :] * pad[SUBLANES * k + r0:SUBLANES * k + r0 + gm, :]
        gates = _bdot(xc, wg_ref[...]) + bg_ref[...]
        r = jax.nn.sigmoid(gates[:, :PAIR_W])
        ig = jax.nn.sigmoid(gates[:, PAIR_W:])
        la = nla_ref[...] * r
        a = jnp.exp(la)
        a_scr[r0:r0 + gm, :] = a
        b_scr[r0:r0 + gm, :] = jnp.sqrt(-jnp.tanh(la) * (a * a + 1.0)) * (ig * xc)

    def scan_body(jj, hp):
        j = seg - 1 - jj if reverse else jj
        rows = pl.ds(pl.multiple_of(j * SUBLANES, SUBLANES), SUBLANES)
        a = a_scr[rows, :]
        h = a * hp[0] + b_scr[rows, :]
        p = a * hp[1]
        b_scr[rows, :] = h
        a_scr[rows, :] = p
        return h, p

    h, p = lax.fori_loop(0, seg, scan_body,
                         (jnp.zeros((SUBLANES, PAIR_W), F32), jnp.ones((SUBLANES, PAIR_W), F32)), unroll=8)

    cin = carry[...]
    cvec = jnp.zeros((SUBLANES, PAIR_W), F32)
    for s in (range(SUBLANES - 1, -1, -1) if reverse else range(SUBLANES)):
        cvec = jnp.where(sub == s, jnp.broadcast_to(cin, (SUBLANES, PAIR_W)), cvec)
        cin = p[s:s + 1, :] * cin + h[s:s + 1, :]
    carry[...] = cin
    cvec2 = jnp.concatenate([cvec, cvec], axis=0)

    def fix_body(j, _):
        rows = pl.ds(pl.multiple_of(j * 2 * SUBLANES, 2 * SUBLANES), 2 * SUBLANES)
        y_ref[rows, :] = (b_scr[rows, :] + a_scr[rows, :] * cvec2).astype(y_ref.dtype)
        return 0

    lax.fori_loop(0, seg // 2, fix_body, 0, unroll=4)

    @pl.when(i == nt - 1)
    def _():
        ht_ref[...] = carry[...]


def _lru(ax, cw, cb, wg, bg, nla, h0, *, reverse, tt, name):
    b, s, _ = ax.shape
    npair = cw.shape[1] // PAIR_W
    nt = s // tt
    nh = tt // HALO
    last_blk = s // HALO - 1
    tb_of = (lambda i: nt - 1 - i) if reverse else (lambda i: i)
    par = lambda bb, p, i: (0, p)
    return pl.pallas_call(
        functools.partial(_lru_kernel, nt=nt, tt=tt, reverse=reverse),
        grid=(b, npair, nt),
        in_specs=[
            pl.BlockSpec((None, tt, PAIR_W), lambda bb, p, i: (bb, tb_of(i), p)),
            pl.BlockSpec((None, HALO, PAIR_W), lambda bb, p, i: (bb, jnp.maximum(tb_of(i) * nh - 1, 0), p)),
            pl.BlockSpec((None, HALO, PAIR_W), lambda bb, p, i: (bb, jnp.minimum((tb_of(i) + 1) * nh, last_blk), p)),
            pl.BlockSpec((CONV_W, PAIR_W), par),
            pl.BlockSpec((1, PAIR_W), par),
            pl.BlockSpec((None, PAIR_W, 2 * PAIR_W), lambda bb, p, i: (p, 0, 0)),
            pl.BlockSpec((None, 1, 2 * PAIR_W), lambda bb, p, i: (p, 0, 0)),
            pl.BlockSpec((1, PAIR_W), par),
            pl.BlockSpec((None, 1, PAIR_W), lambda bb, p, i: (bb, 0, p)),
        ],
        out_specs=[pl.BlockSpec((None, tt, PAIR_W), lambda bb, p, i: (bb, tb_of(i), p)),
                   pl.BlockSpec((None, 1, PAIR_W), lambda bb, p, i: (bb, 0, p))],
        out_shape=[jax.ShapeDtypeStruct((b, s, npair * PAIR_W), BF16),
                   jax.ShapeDtypeStruct((b, 1, npair * PAIR_W), F32)],
        scratch_shapes=[pltpu.VMEM((tt + 3 * SUBLANES, PAIR_W), F32),
                        pltpu.VMEM((tt, PAIR_W), F32),
                        pltpu.VMEM((tt, PAIR_W), F32),
                        pltpu.VMEM((1, PAIR_W), F32)],
        compiler_params=_cparams(("parallel", "parallel", "arbitrary")), name=name,
    )(ax, ax, ax, cw, cb, wg, bg, nla, h0)


def _prep_kernel(xm_ref, xp_ref, xn_ref, cw_ref, o_ref, pad, *, nt, tt, mode):
    i = pl.program_id(1)
    pad[0:HALO, :] = jnp.where(i == 0, 0.0, xp_ref[...].astype(F32))
    pad[HALO:HALO + tt, :] = xm_ref[...].astype(F32)
    pad[HALO + tt:2 * HALO + tt, :] = jnp.where(i == nt - 1, 0.0, xn_ref[...].astype(F32))
    y = cw_ref[0:1, :] * pad[HALO - 1:HALO - 1 + tt, :]
    for k in range(1, CONV_W):
        y = y + cw_ref[k:k + 1, :] * pad[HALO - 1 + k:HALO - 1 + k + tt, :]
    y = y * jax.nn.sigmoid(y)
    for h in range(o_ref.shape[0]):
        yh = y[:, h * LANES:(h + 1) * LANES]
        if mode != "v":
            yh = yh * lax.rsqrt(jnp.sum(yh * yh, axis=-1, keepdims=True) + EPS)
        if mode == "q":
            yh = yh * (DK ** -0.5)
        o_ref[h] = yh.astype(o_ref.dtype)


def _prep(q_arr, cw, *, mode, col0, nheads, tt, name):
    b, s, _ = q_arr.shape
    nh = 4
    width = nh * LANES
    nt = s // tt
    nhalo = tt // HALO
    last_blk = s // HALO - 1
    return pl.pallas_call(
        functools.partial(_prep_kernel, nt=nt, tt=tt, mode=mode),
        grid=(b, nt, nheads // nh),
        in_specs=[pl.BlockSpec((None, tt, width), lambda bb, i, j: (bb, i, col0 + j)),
                  pl.BlockSpec((None, HALO, width), lambda bb, i, j: (bb, jnp.maximum(i * nhalo - 1, 0), col0 + j)),
                  pl.BlockSpec((None, HALO, width), lambda bb, i, j: (bb, jnp.minimum((i + 1) * nhalo, last_blk), col0 + j)),
                  pl.BlockSpec((CONV_W, width), lambda bb, i, j: (0, col0 + j))],
        out_specs=pl.BlockSpec((None, nh, tt, LANES), lambda bb, i, j: (bb, j, i, 0)),
        out_shape=jax.ShapeDtypeStruct((b, nheads, s, LANES), BF16),
        scratch_shapes=[pltpu.VMEM((tt + 2 * HALO, width), F32)],
        compiler_params=_cparams(("parallel", "parallel", "parallel")), name=name,
    )(q_arr, q_arr, q_arr, cw)


def _gates_kernel(x_ref, na_ref, dt_ref, oc_ref, or_ref):
    x = x_ref[...].astype(F32)
    lane = lax.broadcasted_iota(jnp.int32, x.shape, 1)
    z = x + dt_ref[...]
    sp = jnp.maximum(z, 0.0) + jnp.log1p(jnp.exp(-jnp.abs(z)))
    val = jnp.where(lane < 2 * NV, jax.nn.sigmoid(x), na_ref[...] * sp)
    oc_ref[...] = val
    or_ref[...] = val.T


def _gates(g_arr, na, dtb, *, tt):
    b, s, _ = g_arr.shape
    return pl.pallas_call(
        _gates_kernel, grid=(b, s // tt),
        in_specs=[pl.BlockSpec((None, tt, LANES), lambda bb, i: (bb, i, 0)),
                  pl.BlockSpec((1, LANES), lambda bb, i: (0, 0)),
                  pl.BlockSpec((1, LANES), lambda bb, i: (0, 0))],
        out_specs=[pl.BlockSpec((None, tt, LANES), lambda bb, i: (bb, i, 0)),
                   pl.BlockSpec((None, LANES, tt), lambda bb, i: (bb, 0, i))],
        out_shape=[jax.ShapeDtypeStruct((b, s, LANES), F32), jax.ShapeDtypeStruct((b, LANES, s), F32)],
        compiler_params=_cparams(("parallel", "parallel")), name="gdn_gates")(g_arr, na, dtb)


def _gdn_kernel(q_ref, k_ref, v_ref, gc_ref, gr_ref, s0_ref, *rest, nt, tb, reverse, write_out, wave):
    if write_out:
        o_ref, st_ref, st = rest
    else:
        st_ref, st = rest
    i = pl.program_id(1)
    nch = tb // CHUNK
    nkh = q_ref.shape[0]
    lane_b = NV if reverse else 0
    lane_g = 2 * NV + lane_b

    @pl.when(i == 0)
    def _():
        st[...] = s0_ref[...]

    c2 = 2 * CHUNK
    row = lax.broadcasted_iota(jnp.int32, (CHUNK, c2), 0)
    lane = lax.broadcasted_iota(jnp.int32, (CHUNK, c2), 1)
    colm = lane % CHUNK
    left = lane < CHUNK
    if reverse:
        incl, strict, tl = row <= colm, row < colm, 0
    else:
        incl, strict, tl = row >= colm, row > colm, CHUNK - 1
    eye2 = (row == colm).astype(F32)
    r64 = lax.broadcasted_iota(jnp.int32, (CHUNK, CHUNK), 0)
    c64 = lax.broadcasted_iota(jnp.int32, (CHUNK, CHUNK), 1)
    tri = (r64 <= c64 if reverse else r64 >= c64).astype(F32)
    tri_t = (c64 <= r64 if reverse else c64 >= r64).astype(F32)
    z64 = jnp.zeros((CHUNK, CHUNK), F32)
    tri_t_l = jnp.concatenate([tri_t, z64], axis=1)
    tri_t_r = jnp.concatenate([z64, tri_t], axis=1)
    r128 = lax.broadcasted_iota(jnp.int32, (c2, c2), 0)
    l128 = lax.broadcasted_iota(jnp.int32, (c2, c2), 1)
    diag_blocks = (r128 < CHUNK) == (l128 < CHUNK)

    def split_stack(x2):
        return jnp.where(diag_blocks, jnp.concatenate([x2, x2], axis=0), 0.0).astype(BF16)

    def chunk_body(cc, carry):
        c = nch - 1 - cc if reverse else cc
        rows = pl.ds(pl.multiple_of(c * CHUNK, CHUNK), CHUNK)
        gcol = gc_ref[rows, :]
        cs_col = jnp.dot(tri, gcol, precision=HIGHEST, preferred_element_type=F32)
        grow = gr_ref[:, pl.ds(c, 1), :].reshape(2 * nkh, CHUNK)
        gc_row2 = (jnp.dot(grow[:nkh], tri_t_l, precision=HIGHEST, preferred_element_type=F32)
                   + jnp.dot(grow[nkh:], tri_t_r, precision=HIGHEST, preferred_element_type=F32))

        for p0 in range(0, nkh, wave):
            prs = range(p0, min(p0 + wave, nkh))
            ks, qs, pp, tt_, aa, cols = {}, {}, {}, {}, {}, {}
            for p in prs:
                k = k_ref[p, rows, :]
                q = q_ref[p, rows, :]
                kq = _nt_dot(jnp.concatenate([k, q], axis=0), jnp.concatenate([k, k], axis=0))
                ha, hb = 2 * p, 2 * p + 1
                gca, gcb = cs_col[:, lane_g + ha:lane_g + ha + 1], cs_col[:, lane_g + hb:lane_g + hb + 1]
                bca, bcb = gcol[:, lane_b + ha:lane_b + ha + 1], gcol[:, lane_b + hb:lane_b + hb + 1]
                gcc2 = jnp.where(left, gca, gcb)
                bc2 = jnp.where(left, bca, bcb)
                decay2 = jnp.where(incl, jnp.exp(jnp.where(incl, gcc2 - gc_row2[p:p + 1, :], 0.0)), 0.0)
                pn = jnp.where(strict, -(kq[:CHUNK] * bc2) * decay2, 0.0)
                ks[p] = k.astype(F32)
                qs[p] = q.astype(F32)
                pp[p] = pn
                tt_[p] = eye2 + pn
                aa[p] = kq[CHUNK:] * decay2
                cols[p] = ((gca, bca), (gcb, bcb))

            for p in prs:
                pp[p] = jnp.dot(pp[p].astype(BF16), split_stack(pp[p]), preferred_element_type=F32)
            for _ in range(4):
                for p in prs:
                    y = jnp.dot(jnp.concatenate([pp[p], tt_[p]], axis=0).astype(BF16), split_stack(pp[p]),
                                preferred_element_type=F32)
                    pp[p] = y[:CHUNK]
                    tt_[p] = tt_[p] + y[CHUNK:]
            for p in prs:
                tt_[p] = tt_[p] + jnp.dot(tt_[p].astype(BF16), split_stack(pp[p]), preferred_element_type=F32)

            sols = {}
            for p in prs:
                rhs = []
                for e in range(2):
                    gch, bch = cols[p][e]
                    vf = v_ref[2 * p + e, rows, :].astype(F32)
                    rhs.append(jnp.concatenate([vf * bch, ks[p] * (bch * jnp.exp(gch))], axis=1))
                sols[p] = jnp.dot(split_stack(tt_[p]), jnp.concatenate(rhs, axis=0).astype(BF16),
                                  preferred_element_type=F32)

            wss, vns = {}, {}
            for p in prs:
                for e in range(2):
                    m = 2 * p + e
                    gch, _ = cols[p][e]
                    sol = sols[p][e * CHUNK:(e + 1) * CHUNK]
                    lhs = jnp.concatenate([sol[:, DV:], qs[p] * jnp.exp(gch)], axis=0)
                    wss[m] = _bdot(lhs, st[m])
                    vns[m] = sol[:, :DV] - wss[m][:CHUNK]

            for p in prs:
                if write_out:
                    oi = jnp.dot(split_stack(aa[p]), jnp.concatenate([vns[2 * p], vns[2 * p + 1]], axis=0).astype(BF16),
                                 preferred_element_type=F32)
                for e in range(2):
                    m = 2 * p + e
                    gch, _ = cols[p][e]
                    if write_out:
                        o_ref[m, rows, :] = (wss[m][CHUNK:] + oi[e * CHUNK:(e + 1) * CHUNK]).astype(o_ref.dtype)
                    gl = gch[tl:tl + 1, :]
                    kd = ks[p] * jnp.exp(gl - gch)
                    st[m] = st[m] * jnp.exp(gl) + _tn_dot(kd.astype(BF16), vns[m].astype(BF16))
        return carry

    lax.fori_loop(0, nch, chunk_body, 0)

    @pl.when(i == nt - 1)
    def _():
        st_ref[...] = st[...]


def _gdn(qh, kh, vh, gc, gr, s0, *, reverse, tb, write_out, name, wave=8):
    b, nk, s, _ = qh.shape
    nv = vh.shape[1]
    nt = s // tb
    nch = tb // CHUNK
    d = 1 if reverse else 0
    tb_of = (lambda i: nt - 1 - i) if reverse else (lambda i: i)
    in_specs = [
        pl.BlockSpec((None, nk, tb, DK), lambda bb, i: (bb, 0, tb_of(i), 0)),
        pl.BlockSpec((None, nk, tb, DK), lambda bb, i: (bb, 0, tb_of(i), 0)),
        pl.BlockSpec((None, nv, tb, DV), lambda bb, i: (bb, 0, tb_of(i), 0)),
        pl.BlockSpec((None, tb, LANES), lambda bb, i: (bb, tb_of(i), 0)),
        pl.BlockSpec((None, nv, nch, CHUNK), lambda bb, i: (bb, 2 + d, tb_of(i), 0)),
        pl.BlockSpec((None, nv, DK, DV), lambda bb, i: (bb, 0, 0, 0)),
    ]
    st_spec = pl.BlockSpec((None, nv, DK, DV), lambda bb, i: (bb, 0, 0, 0))
    st_shape = jax.ShapeDtypeStruct((b, nv, DK, DV), F32)
    if write_out:
        out_specs = [pl.BlockSpec((None, nv, tb, DV), lambda bb, i: (bb, 0, tb_of(i), 0)), st_spec]
        out_shape = [jax.ShapeDtypeStruct((b, nv, s, DV), BF16), st_shape]
    else:
        out_specs, out_shape = [st_spec], [st_shape]
    return pl.pallas_call(
        functools.partial(_gdn_kernel, nt=nt, tb=tb, reverse=reverse, write_out=write_out, wave=wave),
        grid=(b, nt), in_specs=in_specs, out_specs=out_specs, out_shape=out_shape,
        scratch_shapes=[pltpu.VMEM((nv, DK, DV), F32)],
        compiler_params=_cparams(("parallel", "arbitrary")), name=name,
    )(qh, kh, vh, gc, gr, s0)


def _gelu_tanh(x):
    return 0.5 * x * (1.0 + jnp.tanh(0.7978845608028654 * (x + 0.044715 * (x * x * x))))


def _aout_kernel(yf_ref, yb_ref, ag_ref, w_ref, o_ref, a_scr):
    @pl.when(pl.program_id(2) == 0)
    def _():
        y = yf_ref[...].astype(F32) + yb_ref[...].astype(F32)
        a_scr[...] = (y * _gelu_tanh(ag_ref[...].astype(F32))).astype(a_scr.dtype)

    o_ref[...] = jnp.dot(a_scr[...], w_ref[...], preferred_element_type=F32)


def _aout(yf, yb, ag, w, *, tm, tn):
    b, s, wa = yf.shape
    n = w.shape[1]
    act = pl.BlockSpec((None, tm, wa), lambda bb, i, j: (bb, i, 0))
    return pl.pallas_call(
        _aout_kernel, grid=(b, s // tm, n // tn),
        in_specs=[act, act, act, pl.BlockSpec((wa, tn), lambda bb, i, j: (0, j))],
        out_specs=pl.BlockSpec((None, tm, tn), lambda bb, i, j: (bb, i, j)),
        out_shape=jax.ShapeDtypeStruct((b, s, n), F32),
        scratch_shapes=[pltpu.VMEM((tm, wa), BF16)],
        compiler_params=_cparams(("parallel", "parallel", "arbitrary")), name="aout")(yf, yb, ag, w)


def _bout_kernel(of_ref, ob_ref, zl_ref, zh_ref, nw_ref, ya_ref, ga_ref, gb_ref, w_ref, o_ref, y_scr):
    ncol, tr = o_ref.shape[0], o_ref.shape[1]
    tm = ncol * tr

    @pl.when(pl.program_id(3) == 0)
    def _():
        nw = nw_ref[...]
        nvh = of_ref.shape[0]
        half = nvh // 2
        for h in range(nvh):
            o = (of_ref[h].astype(F32) + ob_ref[h].astype(F32)).reshape(tm, DV)
            z_ref = zl_ref if h < half else zh_ref
            hh = h % half
            z = z_ref[:, :, hh * DV:(hh + 1) * DV].astype(F32).reshape(tm, DV)
            y = o * lax.rsqrt(jnp.mean(o * o, axis=-1, keepdims=True) + EPS)
            y = y * nw * (z * jax.nn.sigmoid(z))
            y_scr[:, h * DV:(h + 1) * DV] = y.astype(y_scr.dtype)

    yb = jnp.dot(y_scr[...], w_ref[...], preferred_element_type=F32)
    for k in range(ncol):
        ga = jax.nn.sigmoid(ga_ref[k].astype(F32))
        gb = jax.nn.sigmoid(gb_ref[k].astype(F32))
        ya = ya_ref[:, k].reshape(tr, ya_ref.shape[-1])
        o_ref[k] = (ga * ya + gb * yb[k * tr:(k + 1) * tr]).astype(o_ref.dtype)


def _bout(of, ob, big, mg, nw, ya, w, *, z_col, tn):
    b, nv, s, dv = of.shape
    n = w.shape[1]
    rows = s // GRID_W
    tr = min(TILE_R, rows)
    hw = nv * dv // 2
    of5 = of.reshape(b, nv, GRID_W, rows, dv)
    ob5 = ob.reshape(b, nv, GRID_W, rows, dv)
    big4 = big.reshape(b, GRID_W, rows, big.shape[-1])
    mg4 = mg.reshape(b, GRID_W, rows, mg.shape[-1])
    ya5 = ya.reshape(b, rows // SUBLANES, GRID_W, SUBLANES, n)
    o_spec = pl.BlockSpec((None, nv, SUBLANES, tr, dv), lambda bb, wi, ri, j: (bb, 0, wi, ri, 0))
    out = pl.pallas_call(
        _bout_kernel, grid=(b, GRID_W // SUBLANES, rows // tr, n // tn),
        in_specs=[o_spec, o_spec,
                  pl.BlockSpec((None, SUBLANES, tr, hw), lambda bb, wi, ri, j: (bb, wi, ri, z_col // hw)),
                  pl.BlockSpec((None, SUBLANES, tr, hw), lambda bb, wi, ri, j: (bb, wi, ri, z_col // hw + 1)),
                  pl.BlockSpec((1, dv), lambda bb, wi, ri, j: (0, 0)),
                  pl.BlockSpec((None, tr // SUBLANES, SUBLANES, SUBLANES, tn), lambda bb, wi, ri, j: (bb, ri, wi, 0, j)),
                  pl.BlockSpec((None, SUBLANES, tr, tn), lambda bb, wi, ri, j: (bb, wi, ri, j)),
                  pl.BlockSpec((None, SUBLANES, tr, tn), lambda bb, wi, ri, j: (bb, wi, ri, n // tn + j)),
                  pl.BlockSpec((nv * dv, tn), lambda bb, wi, ri, j: (0, j))],
        out_specs=pl.BlockSpec((None, SUBLANES, tr, tn), lambda bb, wi, ri, j: (bb, wi, ri, j)),
        out_shape=jax.ShapeDtypeStruct((b, GRID_W, rows, n), BF16),
        scratch_shapes=[pltpu.VMEM((SUBLANES * tr, nv * dv), BF16)],
        compiler_params=_cparams(("parallel", "parallel", "parallel", "arbitrary")), name="bout",
    )(of5, ob5, big4, big4, nw, ya5, mg4, mg4, w)
    return out


def _ffn_kernel(m_ref, x_ref, perm_ref, gt1_ref, wo_ref, lnw_ref, sh_ref, sc_ref, gt2_ref, wg_ref, wu_ref, wd_ref, fw_ref,
                o_ref, x1_scr, h_scr, *, nf):
    f = pl.program_id(3)
    ncol, tr = m_ref.shape[0], m_ref.shape[1]
    tm, d = h_scr.shape
    ch = min(tm, 256)
    nh = d // 2

    @pl.when(f == 0)
    def _():
        m2 = m_ref[...].reshape(tm, d)
        for q0 in range(0, tm, ch):
            h_scr[q0:q0 + ch, :] = jnp.dot(perm_ref[q0:q0 + ch, :], m2, preferred_element_type=F32).astype(BF16)
        for n0 in range(0, d, nh):
            x1_scr[:, n0:n0 + nh] = jnp.dot(h_scr[...], wo_ref[:, n0:n0 + nh], preferred_element_type=F32)
        for r0 in range(0, tm, ch):
            xb = x_ref[r0 // ncol:(r0 + ch) // ncol].reshape(ch, d)
            x1 = xb + gt1_ref[...] * x1_scr[r0:r0 + ch, :]
            x1_scr[r0:r0 + ch, :] = x1
            h_scr[r0:r0 + ch, :] = _rms_mod(x1, lnw_ref[...], 1.0 + sc_ref[...], sh_ref[...]).astype(h_scr.dtype)

    h = h_scr[...]
    g = jnp.dot(h, wg_ref[...], preferred_element_type=F32)
    u = jnp.dot(h, wu_ref[...], preferred_element_type=F32)
    a = (g * jax.nn.sigmoid(g) * u).astype(BF16)
    for n0 in range(0, d, nh):
        part = jnp.dot(a, wd_ref[:, n0:n0 + nh], preferred_element_type=F32).reshape(tm // ncol, ncol, nh)

        @pl.when(f == 0)
        def _(part=part, n0=n0):
            o_ref[:, :, n0:n0 + nh] = part

        @pl.when(f > 0)
        def _(part=part, n0=n0):
            o_ref[:, :, n0:n0 + nh] += part

    @pl.when(f == nf - 1)
    def _():
        for r0 in range(0, tm, ch):
            rr = slice(r0 // ncol, (r0 + ch) // ncol)
            x2 = x1_scr[r0:r0 + ch, :] + gt2_ref[...] * o_ref[rr].reshape(ch, d)
            ms = jnp.mean(x2 * x2, axis=-1, keepdims=True)
            o_ref[rr] = (x2 * lax.rsqrt(ms + EPS) * fw_ref[...]).reshape(ch // ncol, ncol, d)


def _ffn(m4, x, gt1, w_o, lnw, sh, sc, gt2, w_in, w_dn, fw, *, tf):
    b, s, d = x.shape
    rows = s // GRID_W
    tr = min(FFN_TILE_R, rows)
    tm = SUBLANES * tr
    fh = w_dn.shape[0]
    nf = fh // tf
    x4 = x.reshape(b, rows, GRID_W, d)
    vec = pl.BlockSpec((None, 1, d), lambda bb, wi, ri, f: (bb, 0, 0))
    par = pl.BlockSpec((1, d), lambda bb, wi, ri, f: (0, 0))
    once = pl.Buffered(1)
    out = pl.pallas_call(
        functools.partial(_ffn_kernel, nf=nf), grid=(b, GRID_W // SUBLANES, rows // tr, nf),
        in_specs=[pl.BlockSpec((None, SUBLANES, tr, d), lambda bb, wi, ri, f: (bb, wi, ri, 0), pipeline_mode=once),
                  pl.BlockSpec((None, tr, SUBLANES, d), lambda bb, wi, ri, f: (bb, ri, wi, 0), pipeline_mode=once),
                  pl.BlockSpec((tm, tm), lambda bb, wi, ri, f: (0, 0), pipeline_mode=once),
                  vec,
                  pl.BlockSpec((d, d), lambda bb, wi, ri, f: (0, 0), pipeline_mode=once),
                  par, vec, vec, vec,
                  pl.BlockSpec((d, tf), lambda bb, wi, ri, f: (0, f)),
                  pl.BlockSpec((d, tf), lambda bb, wi, ri, f: (0, nf + f)),
                  pl.BlockSpec((tf, d), lambda bb, wi, ri, f: (f, 0)),
                  par],
        out_specs=pl.BlockSpec((None, tr, SUBLANES, d), lambda bb, wi, ri, f: (bb, ri, wi, 0), pipeline_mode=once),
        out_shape=jax.ShapeDtypeStruct((b, rows, GRID_W, d), F32),
        scratch_shapes=[pltpu.VMEM((tm, d), F32), pltpu.VMEM((tm, d), BF16)],
        compiler_params=_cparams(("parallel", "parallel", "parallel", "arbitrary")), name="ffn",
    )(m4, x4, _strip_perm(tm), gt1, w_o, lnw, sh, sc, gt2, w_in, w_in, w_dn, fw)
    return out.reshape(b, s, d)


def _pad_pairs(t, axis):
    n = t.shape[axis]
    bw = n // NH_A
    shp = t.shape[:axis] + (NH_A // 2, 2 * bw) + t.shape[axis + 1:]
    t = t.reshape(shp)
    padw = [(0, 0)] * t.ndim
    padw[axis + 1] = (0, PAIR_W - 2 * bw)
    t = jnp.pad(t, padw)
    return t.reshape(t.shape[:axis] + (NH_A // 2 * PAIR_W,) + t.shape[axis + 2:])


def _pair_gate_weights(gw, gb):
    bw = gw.shape[1]
    npair = NH_A // 2
    g = gw.reshape(npair, 2, bw, 2, bw)
    blocks = []
    for t in range(2):
        top = jnp.pad(g[:, 0, :, t, :], ((0, 0), (0, 0), (0, PAIR_W - bw)))
        bot = jnp.pad(g[:, 1, :, t, :], ((0, 0), (0, PAIR_W - 2 * bw), (bw, PAIR_W - 2 * bw)))
        blocks.append(jnp.concatenate([top, bot], axis=1))
    wp = jnp.concatenate(blocks, axis=2)
    b = gb.reshape(npair, 2, 2, bw).transpose(0, 2, 1, 3).reshape(npair, 2, 2 * bw)
    b = jnp.pad(b, ((0, 0), (0, 0), (0, PAIR_W - 2 * bw)))
    return wp.astype(BF16), b.reshape(npair, 1, 2 * PAIR_W)


def kernel(x, c, ctx, c_ctx, w_mod, b_mod, ln1_w, ln2_w, w_in, conv_a_w, conv_a_b, lru_gate_w, lru_gate_b, lru_lambda, conv_qkv_w, gdn_a_log, gdn_dt_bias, gdn_norm_w, w_a_out, w_b_out, w_out, w_ffn_in, w_ffn_out, final_norm_w):
    depth = w_mod.shape[0]
    assert depth == 1, "context stream updates are only needed for depth > 1"
    bsz, seq, d = x.shape
    lctx = ctx.shape[1]
    rows = seq // GRID_W
    assert seq % STRIP == 0 and rows % CHUNK == 0 and lctx % CHUNK == 0
    wa = lru_lambda.shape[-1]
    wap = NH_A // 2 * PAIR_W
    qk_dim, v_dim = NK * DK, NV * DV
    qkv_dim = 2 * qk_dim + v_dim
    l = 0
    tn = 512

    wi = w_in[l]
    o_qkv, o_z, o_g, o_mg = 2 * wa, 2 * wa + qkv_dim, 2 * wa + qkv_dim + v_dim, 2 * wa + qkv_dim + v_dim + 4 * NV
    w_axp = _pad_pairs(wi[:, :wa], 1).astype(BF16)
    w_agp = _pad_pairs(wi[:, wa:2 * wa], 1).astype(BF16)
    w_qz = wi[:, o_qkv:o_g].astype(BF16)
    w_gt = wi[:, o_g:o_mg].astype(BF16)
    w_mg = wi[:, o_mg:].astype(BF16)
    cw_a = _pad_pairs(conv_a_w[l], 1)
    cb_a = _pad_pairs(conv_a_b[l][None], 1)
    nla = _pad_pairs(-LRU_C * jax.nn.softplus(-lru_lambda[l]), 1)
    gate_w = [_pair_gate_weights(lru_gate_w[l, dd], lru_gate_b[l, dd]) for dd in range(2)]
    neg_a = -jnp.exp(gdn_a_log[l]).reshape(1, 2 * NV)
    na_l = jnp.concatenate([jnp.zeros((1, 2 * NV), F32), neg_a], axis=1)
    dt_l = jnp.concatenate([jnp.zeros((1, 2 * NV), F32), gdn_dt_bias[l].reshape(1, 2 * NV)], axis=1)
    w_a = _pad_pairs(w_a_out[l], 0).astype(BF16)
    w_b = w_b_out[l].astype(BF16)
    w_o = w_out[l].astype(BF16)
    w_f1 = w_ffn_in[l].astype(BF16)
    w_f2 = w_ffn_out[l].astype(BF16)

    nrow = -(-(bsz + 1) // SUBLANES) * SUBLANES
    cc = jnp.concatenate([c, c_ctx[None], jnp.zeros((nrow - bsz - 1, d), F32)], axis=0)
    mod = _mod(cc, w_mod[l], b_mod[l][None]).reshape(nrow, N_MOD, 1, d)
    sh1, sc1, gt1, sh2, sc2, gt2 = (mod[:bsz, k] for k in range(N_MOD))
    csh1 = jnp.broadcast_to(mod[bsz, 0], (bsz, 1, d))
    csc1 = jnp.broadcast_to(mod[bsz, 1], (bsz, 1, d))
    ln1 = ln1_w[l][None]

    nax = wap // tn
    ax, ag = _inproj(x, ln1, sh1, sc1, [(w_axp, tn, nax, BF16), (w_agp, tn, nax, BF16)],
                     order="strip", tm=min(2 * STRIP, seq), tt=STRIP, name="inproj_a")
    big, mg, gts = _inproj(x, ln1, sh1, sc1,
                           [(w_qz, 2 * tn, (qkv_dim + v_dim) // (2 * tn), BF16), (w_mg, 2 * tn, d // tn, BF16), (w_gt, LANES, 1, F32)],
                           order="colmajor", name="inproj_b")
    (cax,) = _inproj(ctx, ln1, csh1, csc1, [(w_axp, tn, nax, BF16)], order="strip", tm=lctx, tt=lctx, name="inproj_ca")
    cbig, cgts = _inproj(ctx, ln1, csh1, csc1, [(w_qz, tn, qkv_dim // tn, BF16), (w_gt, LANES, 1, F32)],
                         order="raster", tm=lctx, name="inproj_cb")

    ys = []
    for dd in range(2):
        wg, bg = gate_w[dd]
        h0 = jnp.zeros((bsz, 1, wap), F32)
        _, hc = _lru(cax, cw_a, cb_a, wg, bg, nla[dd:dd + 1], h0, reverse=bool(dd), tt=lctx, name=f"lru_c{dd}")
        y, _ = _lru(ax, cw_a, cb_a, wg, bg, nla[dd:dd + 1], hc, reverse=bool(dd), tt=STRIP, name=f"lru_x{dd}")
        ys.append(y)

    cw_q = conv_qkv_w[l]

    def prep_all(q_arr, g_arr, tt, tag):
        qh = _prep(q_arr, cw_q, mode="q", col0=0, nheads=NK, tt=tt, name="prep_q" + tag)
        kh = _prep(q_arr, cw_q, mode="k", col0=qk_dim // 512, nheads=NK, tt=tt, name="prep_k" + tag)
        vh = _prep(q_arr, cw_q, mode="v", col0=2 * qk_dim // 512, nheads=NV, tt=tt, name="prep_v" + tag)
        gc, gr = _gates(g_arr, na_l, dt_l, tt=tt)
        s_len = q_arr.shape[1]
        gr = gr.reshape(bsz, 4, NK, 2, s_len).transpose(0, 1, 3, 2, 4)
        return qh, kh, vh, gc, gr.reshape(bsz, LANES, s_len // CHUNK, CHUNK)

    pc = prep_all(cbig, cgts, lctx, "_c")
    px = prep_all(big, gts, min(512, seq), "_x")
    os_ = []
    for dd in range(2):
        s0 = jnp.zeros((bsz, NV, DK, DV), F32)
        (sc_state,) = _gdn(*pc, s0, reverse=bool(dd), tb=lctx, write_out=False, name=f"gdn_c{dd}")
        o, _ = _gdn(*px, sc_state, reverse=bool(dd), tb=min(512, seq), write_out=True, name=f"gdn_x{dd}")
        os_.append(o)

    ya = _aout(ys[0], ys[1], ag, w_a, tm=min(512, seq), tn=tn)
    m4 = _bout(os_[0], os_[1], big, mg, gdn_norm_w[l][None], ya, w_b, z_col=qkv_dim, tn=tn)
    return _ffn(m4, x, gt1, w_o, ln2_w[l][None], sh2, sc2, gt2, w_f1, w_f2, final_norm_w[None], tf=256)
```

```python
import functools

import jax
import jax.numpy as jnp
import numpy as np
from jax import lax
from jax.experimental import pallas as pl
from jax.experimental.pallas import tpu as pltpu

F32 = jnp.float32
BF16 = jnp.bfloat16
HIGHEST = lax.Precision.HIGHEST

EPS = 1e-6
GRID_W = 64
CONV_W = 4
N_MOD = 6
NH_A = 16
LRU_C = 8.0
NK, DK, NV, DV = 16, 128, 32, 128
CHUNK = 64
LANES = 128
SUBLANES = 8
HALO = 16
PAIR_W = 384
STRIP = SUBLANES * GRID_W
TILE_R = 64
VMEM_LIMIT = 63 * 1024 * 1024


def _cparams(sem):
    return pltpu.CompilerParams(dimension_semantics=sem, vmem_limit_bytes=VMEM_LIMIT)


def _nt_dot(a, b, **kw):
    return lax.dot_general(a, b, (((1,), (1,)), ((), ())), preferred_element_type=F32, **kw)


def _tn_dot(a, b):
    return lax.dot_general(a, b, (((0,), (0,)), ((), ())), preferred_element_type=F32)


def _bdot(a, b):
    return jnp.dot(a.astype(BF16), b.astype(BF16), preferred_element_type=F32)


def _sigmoid(x):
    return 0.5 * jnp.tanh(0.5 * x) + 0.5


def _mod_kernel(c_ref, w_ref, b_ref, o_ref):
    s = c_ref[...]
    s = s * _sigmoid(s)
    o_ref[...] = jnp.dot(s, w_ref[...], precision=HIGHEST, preferred_element_type=F32) + b_ref[...]


def _mod(cc, w, b, tn=1024):
    m, d = cc.shape
    n = w.shape[1]
    return pl.pallas_call(
        _mod_kernel, grid=(n // tn,),
        in_specs=[pl.BlockSpec((m, d), lambda j: (0, 0)),
                  pl.BlockSpec((d, tn), lambda j: (0, j)),
                  pl.BlockSpec((1, tn), lambda j: (0, j))],
        out_specs=pl.BlockSpec((m, tn), lambda j: (0, j)),
        out_shape=jax.ShapeDtypeStruct((m, n), F32),
        compiler_params=_cparams(("parallel",)), name="mod")(cc, w, b)


def _rms_mod(xb, lnw, sc1, sh):
    xb = xb.astype(F32)
    ms = jnp.mean(xb * xb, axis=-1, keepdims=True)
    return (xb * lax.rsqrt(ms + EPS) * lnw) * sc1 + sh


def _strip_perm(tt):
    seg = tt // SUBLANES
    i = np.arange(tt)
    p = np.zeros((tt, tt), np.float32)
    p[i, (i % SUBLANES) * seg + i // SUBLANES] = 1.0
    return jnp.asarray(p, BF16)


def _colmajor_perm(rows):
    n = SUBLANES * rows
    i = np.arange(n)
    p = np.zeros((n, n), np.float32)
    p[i, (i % rows) * SUBLANES + i // rows] = 1.0
    return jnp.asarray(p, BF16)


def _inproj_kernel(*refs, order, tiles, tt):
    nseg = len(tiles)
    x_ref, lnw_ref, sh_ref, sc_ref = refs[:4]
    pos = 4 if order == "raster" else 5
    w_refs = refs[pos:pos + nseg]
    o_refs = refs[pos + nseg:pos + 2 * nseg]
    scr = refs[pos + 2 * nseg:]
    h_scr = scr[0]
    tm, d = h_scr.shape
    j = pl.program_id(2)

    @pl.when(j == 0)
    def _():
        lnw = lnw_ref[...]
        sc1 = 1.0 + sc_ref[...]
        sh = sh_ref[...]
        dst = h_scr if order == "raster" else scr[1]
        ch = min(tm, 256)
        for r0 in range(0, tm, ch):
            if order == "colmajor":
                xb = x_ref[r0 // SUBLANES:(r0 + ch) // SUBLANES].reshape(ch, d)
            else:
                xb = x_ref[r0:r0 + ch, :]
            dst[r0:r0 + ch, :] = _rms_mod(xb, lnw, sc1, sh).astype(dst.dtype)
        if order != "raster":
            perm_ref = refs[4]
            for r0 in range(0, tm, tt):
                for q0 in range(0, tt, ch):
                    h_scr[r0 + q0:r0 + q0 + ch, :] = jnp.dot(
                        perm_ref[q0:q0 + ch, :], dst[r0:r0 + tt, :], preferred_element_type=F32).astype(h_scr.dtype)

    lo = 0
    for k in range(nseg):
        @pl.when((j >= lo) & (j < lo + tiles[k]))
        def _(k=k):
            o_refs[k][...] = jnp.dot(h_scr[...], w_refs[k][...], preferred_element_type=F32).astype(o_refs[k].dtype)
        lo += tiles[k]


def _inproj(x, lnw, sh, sc, segs, *, order, tm=None, tt=None, name="inproj"):
    b, s, d = x.shape
    ins = [None, lnw, sh, sc]
    if order == "colmajor":
        rows = s // GRID_W
        tm = SUBLANES * rows
        ins[0] = x.reshape(b, rows, GRID_W, d)
        x_spec = pl.BlockSpec((None, rows, SUBLANES, d), lambda bb, i, j: (bb, 0, i, 0))
    else:
        ins[0] = x
        x_spec = pl.BlockSpec((None, tm, d), lambda bb, i, j: (bb, i, 0))
    in_specs = [x_spec,
                pl.BlockSpec((1, d), lambda bb, i, j: (0, 0)),
                pl.BlockSpec((None, 1, d), lambda bb, i, j: (bb, 0, 0)),
                pl.BlockSpec((None, 1, d), lambda bb, i, j: (bb, 0, 0))]
    scratch = [pltpu.VMEM((tm, d), BF16)]
    if order != "raster":
        if order == "colmajor":
            tt = tm
        ins.append(_strip_perm(tt) if order == "strip" else _colmajor_perm(s // GRID_W))
        in_specs.append(pl.BlockSpec((tt, tt), lambda bb, i, j: (0, 0)))
        scratch.append(pltpu.VMEM((tm, d), BF16))
    tiles = tuple(sg[2] for sg in segs)
    out_specs, out_shape = [], []
    lo = 0
    for w, tn, nt, dt in segs:
        clip = functools.partial(lambda j, lo, nt: jnp.clip(j - lo, 0, nt - 1), lo=lo, nt=nt)
        ins.append(w)
        in_specs.append(pl.BlockSpec((d, tn), lambda bb, i, j, clip=clip: (0, clip(j))))
        out_specs.append(pl.BlockSpec((None, tm, tn), lambda bb, i, j, clip=clip: (bb, i, clip(j))))
        out_shape.append(jax.ShapeDtypeStruct((b, s, tn * nt), dt))
        lo += nt
    return pl.pallas_call(
        functools.partial(_inproj_kernel, order=order, tiles=tiles, tt=tt),
        grid=(b, s // tm, lo), in_specs=in_specs, out_specs=out_specs, out_shape=out_shape,
        scratch_shapes=scratch,
        compiler_params=_cparams(("parallel", "parallel", "arbitrary")), name=name)(*ins)


def _lru_kernel(xm_ref, xp_ref, xn_ref, cw_ref, cb_ref, wg_ref, bg_ref, nla_ref, h0_ref,
                y_ref, ht_ref, pad, a_scr, b_scr, carry, *, nt, tt, reverse):
    i = pl.program_id(2)
    tb = nt - 1 - i if reverse else i
    seg = tt // SUBLANES
    gm = min(tt, 256)

    @pl.when(i == 0)
    def _():
        carry[...] = h0_ref[...]

    sub = lax.broadcasted_iota(jnp.int32, (SUBLANES, PAIR_W), 0)
    first, last = tb == 0, tb == nt - 1
    prev_last = jnp.where(first, 0.0, xp_ref[SUBLANES:2 * SUBLANES, :].astype(F32))
    nxt0 = jnp.where(last, 0.0, xn_ref[0:SUBLANES, :].astype(F32))
    nxt1 = jnp.where(last, 0.0, xn_ref[SUBLANES:2 * SUBLANES, :].astype(F32))
    x_end = xm_ref[tt - SUBLANES:tt, :].astype(F32)
    x_0 = xm_ref[0:SUBLANES, :].astype(F32)
    x_1 = xm_ref[SUBLANES:2 * SUBLANES, :].astype(F32)
    up = SUBLANES - 1
    pad[0:SUBLANES, :] = jnp.where(sub == 0, pltpu.roll(prev_last, 1, 0), pltpu.roll(x_end, 1, 0))
    pad[SUBLANES:SUBLANES + tt, :] = xm_ref[...].astype(F32)
    pad[SUBLANES + tt:2 * SUBLANES + tt, :] = jnp.where(sub == up, pltpu.roll(nxt0, up, 0), pltpu.roll(x_0, up, 0))
    pad[2 * SUBLANES + tt:3 * SUBLANES + tt, :] = jnp.where(sub == up, pltpu.roll(nxt1, up, 0), pltpu.roll(x_1, up, 0))

    for r0 in range(0, tt, gm):
        xc = cb_ref[...] + cw_ref[0:1, :] * pad[r0:r0 + gm, :]
        for k in range(1, CONV_W):
            xc = xc + cw_ref[k:k + 1, :] * pad[SUBLANES * k + r0:SUBLANES * k + r0 + gm, :]
        gates = _bdot(xc, wg_ref[...]) + bg_ref[...]
        r = _sigmoid(gates[:, :PAIR_W])
        ig = _sigmoid(gates[:, PAIR_W:])
        la = nla_ref[...] * r
        a = jnp.exp(la)
        a_scr[r0:r0 + gm, :] = a
        b_scr[r0:r0 + gm, :] = jnp.sqrt(-jnp.tanh(la) * (a * a + 1.0)) * (ig * xc)

    def scan_body(jj, hp):
        j = seg - 1 - jj if reverse else jj
        rows = pl.ds(pl.multiple_of(j * SUBLANES, SUBLANES), SUBLANES)
        a = a_scr[rows, :]
        h = a * hp[0] + b_scr[rows, :]
        p = a * hp[1]
        b_scr[rows, :] = h
        a_scr[rows, :] = p
        return h, p

    h, p = lax.fori_loop(0, seg, scan_body,
                         (jnp.zeros((SUBLANES, PAIR_W), F32), jnp.ones((SUBLANES, PAIR_W), F32)), unroll=8)

    cin = carry[...]
    cvec = jnp.zeros((SUBLANES, PAIR_W), F32)
    for s in (range(SUBLANES - 1, -1, -1) if reverse else range(SUBLANES)):
        cvec = jnp.where(sub == s, jnp.broadcast_to(cin, (SUBLANES, PAIR_W)), cvec)
        cin = p[s:s + 1, :] * cin + h[s:s + 1, :]
    carry[...] = cin
    cvec2 = jnp.concatenate([cvec, cvec], axis=0)

    def fix_body(j, _):
        rows = pl.ds(pl.multiple_of(j * 2 * SUBLANES, 2 * SUBLANES), 2 * SUBLANES)
        y_ref[rows, :] = (b_scr[rows, :] + a_scr[rows, :] * cvec2).astype(y_ref.dtype)
        return 0

    lax.fori_loop(0, seg // 2, fix_body, 0, unroll=4)

    @pl.when(i == nt - 1)
    def _():
        ht_ref[...] = carry[...]


def _lru(ax, cw, cb, wg, bg, nla, h0, *, reverse, tt, name):
    b, s, _ = ax.shape
    npair = cw.shape[1] // PAIR_W
    nt = s // tt
    nh = tt // HALO
    last_blk = s // HALO - 1
    tb_of = (lambda i: nt - 1 - i) if reverse else (lambda i: i)
    par = lambda bb, p, i: (0, p)
    return pl.pallas_call(
        functools.partial(_lru_kernel, nt=nt, tt=tt, reverse=reverse),
        grid=(b, npair, nt),
        in_specs=[
            pl.BlockSpec((None, tt, PAIR_W), lambda bb, p, i: (bb, tb_of(i), p)),
            pl.BlockSpec((None, HALO, PAIR_W), lambda bb, p, i: (bb, jnp.maximum(tb_of(i) * nh - 1, 0), p)),
            pl.BlockSpec((None, HALO, PAIR_W), lambda bb, p, i: (bb, jnp.minimum((tb_of(i) + 1) * nh, last_blk), p)),
            pl.BlockSpec((CONV_W, PAIR_W), par),
            pl.BlockSpec((1, PAIR_W), par),
            pl.BlockSpec((None, PAIR_W, 2 * PAIR_W), lambda bb, p, i: (p, 0, 0)),
            pl.BlockSpec((None, 1, 2 * PAIR_W), lambda bb, p, i: (p, 0, 0)),
            pl.BlockSpec((1, PAIR_W), par),
            pl.BlockSpec((None, 1, PAIR_W), lambda bb, p, i: (bb, 0, p)),
        ],
        out_specs=[pl.BlockSpec((None, tt, PAIR_W), lambda bb, p, i: (bb, tb_of(i), p)),
                   pl.BlockSpec((None, 1, PAIR_W), lambda bb, p, i: (bb, 0, p))],
        out_shape=[jax.ShapeDtypeStruct((b, s, npair * PAIR_W), BF16),
                   jax.ShapeDtypeStruct((b, 1, npair * PAIR_W), F32)],
        scratch_shapes=[pltpu.VMEM((tt + 3 * SUBLANES, PAIR_W), F32),
                        pltpu.VMEM((tt, PAIR_W), F32),
                        pltpu.VMEM((tt, PAIR_W), F32),
                        pltpu.VMEM((1, PAIR_W), F32)],
        compiler_params=_cparams(("parallel", "parallel", "arbitrary")), name=name,
    )(ax, ax, ax, cw, cb, wg, bg, nla, h0)


def _prep_kernel(xm_ref, xp_ref, xn_ref, cw_ref, o_ref, pad, *, nt, tt, mode):
    i = pl.program_id(1)
    pad[0:HALO, :] = jnp.where(i == 0, 0.0, xp_ref[...].astype(F32))
    pad[HALO:HALO + tt, :] = xm_ref[...].astype(F32)
    pad[HALO + tt:2 * HALO + tt, :] = jnp.where(i == nt - 1, 0.0, xn_ref[...].astype(F32))
    y = cw_ref[0:1, :] * pad[HALO - 1:HALO - 1 + tt, :]
    for k in range(1, CONV_W):
        y = y + cw_ref[k:k + 1, :] * pad[HALO - 1 + k:HALO - 1 + k + tt, :]
    y = y * _sigmoid(y)
    for h in range(o_ref.shape[0]):
        yh = y[:, h * LANES:(h + 1) * LANES]
        if mode != "v":
            yh = yh * lax.rsqrt(jnp.sum(yh * yh, axis=-1, keepdims=True) + EPS)
        if mode == "q":
            yh = yh * (DK ** -0.5)
        o_ref[h] = yh.astype(o_ref.dtype)


def _prep(q_arr, cw, *, mode, col0, nheads, tt, name):
    b, s, _ = q_arr.shape
    nh = 4
    width = nh * LANES
    nt = s // tt
    nhalo = tt // HALO
    last_blk = s // HALO - 1
    return pl.pallas_call(
        functools.partial(_prep_kernel, nt=nt, tt=tt, mode=mode),
        grid=(b, nt, nheads // nh),
        in_specs=[pl.BlockSpec((None, tt, width), lambda bb, i, j: (bb, i, col0 + j)),
                  pl.BlockSpec((None, HALO, width), lambda bb, i, j: (bb, jnp.maximum(i * nhalo - 1, 0), col0 + j)),
                  pl.BlockSpec((None, HALO, width), lambda bb, i, j: (bb, jnp.minimum((i + 1) * nhalo, last_blk), col0 + j)),
                  pl.BlockSpec((CONV_W, width), lambda bb, i, j: (0, col0 + j))],
        out_specs=pl.BlockSpec((None, nh, tt, LANES), lambda bb, i, j: (bb, j, i, 0)),
        out_shape=jax.ShapeDtypeStruct((b, nheads, s, LANES), BF16),
        scratch_shapes=[pltpu.VMEM((tt + 2 * HALO, width), F32)],
        compiler_params=_cparams(("parallel", "parallel", "parallel")), name=name,
    )(q_arr, q_arr, q_arr, cw)


def _gates_kernel(x_ref, na_ref, dt_ref, oc_ref, or_ref):
    x = x_ref[...].astype(F32)
    lane = lax.broadcasted_iota(jnp.int32, x.shape, 1)
    z = x + dt_ref[...]
    sp = jnp.maximum(z, 0.0) + jnp.log1p(jnp.exp(-jnp.abs(z)))
    val = jnp.where(lane < 2 * NV, _sigmoid(x), na_ref[...] * sp)
    oc_ref[...] = val
    or_ref[...] = val.T


def _gates(g_arr, na, dtb, *, tt):
    b, s, _ = g_arr.shape
    return pl.pallas_call(
        _gates_kernel, grid=(b, s // tt),
        in_specs=[pl.BlockSpec((None, tt, LANES), lambda bb, i: (bb, i, 0)),
                  pl.BlockSpec((1, LANES), lambda bb, i: (0, 0)),
                  pl.BlockSpec((1, LANES), lambda bb, i: (0, 0))],
        out_specs=[pl.BlockSpec((None, tt, LANES), lambda bb, i: (bb, i, 0)),
                   pl.BlockSpec((None, LANES, tt), lambda bb, i: (bb, 0, i))],
        out_shape=[jax.ShapeDtypeStruct((b, s, LANES), F32), jax.ShapeDtypeStruct((b, LANES, s), F32)],
        compiler_params=_cparams(("parallel", "parallel")), name="gdn_gates")(g_arr, na, dtb)


def _gdn_kernel(q_ref, k_ref, v_ref, gc_ref, gr_ref, s0_ref, *rest, nt, tb, reverse, write_out, wave):
    if write_out:
        o_ref, st_ref, st = rest
    else:
        st_ref, st = rest
    i = pl.program_id(1)
    nch = tb // CHUNK
    nkh = q_ref.shape[0]
    lane_b = NV if reverse else 0
    lane_g = 2 * NV + lane_b

    @pl.when(i == 0)
    def _():
        st[...] = s0_ref[...]

    c2 = 2 * CHUNK
    row = lax.broadcasted_iota(jnp.int32, (CHUNK, c2), 0)
    lane = lax.broadcasted_iota(jnp.int32, (CHUNK, c2), 1)
    colm = lane % CHUNK
    left = lane < CHUNK
    if reverse:
        incl, strict, tl = row <= colm, row < colm, 0
    else:
        incl, strict, tl = row >= colm, row > colm, CHUNK - 1
    eye2 = (row == colm).astype(F32)
    r64 = lax.broadcasted_iota(jnp.int32, (CHUNK, CHUNK), 0)
    c64 = lax.broadcasted_iota(jnp.int32, (CHUNK, CHUNK), 1)
    tri = (r64 <= c64 if reverse else r64 >= c64).astype(F32)
    tri_t = (c64 <= r64 if reverse else c64 >= r64).astype(F32)
    z64 = jnp.zeros((CHUNK, CHUNK), F32)
    tri_t_l = jnp.concatenate([tri_t, z64], axis=1)
    tri_t_r = jnp.concatenate([z64, tri_t], axis=1)
    r128 = lax.broadcasted_iota(jnp.int32, (c2, c2), 0)
    l128 = lax.broadcasted_iota(jnp.int32, (c2, c2), 1)
    diag_blocks = (r128 < CHUNK) == (l128 < CHUNK)

    def split_stack(x2):
        return jnp.where(diag_blocks, jnp.concatenate([x2, x2], axis=0), 0.0).astype(BF16)

    def chunk_body(cc, carry):
        c = nch - 1 - cc if reverse else cc
        rows = pl.ds(pl.multiple_of(c * CHUNK, CHUNK), CHUNK)
        gcol = gc_ref[rows, :]
        cs_col = jnp.dot(tri, gcol, precision=HIGHEST, preferred_element_type=F32)
        grow = gr_ref[:, pl.ds(c, 1), :].reshape(2 * nkh, CHUNK)
        gc_row2 = (jnp.dot(grow[:nkh], tri_t_l, precision=HIGHEST, preferred_element_type=F32)
                   + jnp.dot(grow[nkh:], tri_t_r, precision=HIGHEST, preferred_element_type=F32))

        for p0 in range(0, nkh, wave):
            prs = range(p0, min(p0 + wave, nkh))
            ks, qs, pp, tt_, aa, cols = {}, {}, {}, {}, {}, {}
            for p in prs:
                k = k_ref[p, rows, :]
                q = q_ref[p, rows, :]
                kq = _nt_dot(jnp.concatenate([k, q], axis=0), jnp.concatenate([k, k], axis=0))
                ha, hb = 2 * p, 2 * p + 1
                gca, gcb = cs_col[:, lane_g + ha:lane_g + ha + 1], cs_col[:, lane_g + hb:lane_g + hb + 1]
                bca, bcb = gcol[:, lane_b + ha:lane_b + ha + 1], gcol[:, lane_b + hb:lane_b + hb + 1]
                gcc2 = jnp.where(left, gca, gcb)
                bc2 = jnp.where(left, bca, bcb)
                decay2 = jnp.where(incl, jnp.exp(jnp.where(incl, gcc2 - gc_row2[p:p + 1, :], 0.0)), 0.0)
                pn = jnp.where(strict, -(kq[:CHUNK] * bc2) * decay2, 0.0)
                ks[p] = k.astype(F32)
                qs[p] = q.astype(F32)
                pp[p] = pn
                tt_[p] = eye2 + pn
                aa[p] = kq[CHUNK:] * decay2
                cols[p] = ((gca, bca), (gcb, bcb))

            for p in prs:
                pp[p] = jnp.dot(pp[p].astype(BF16), split_stack(pp[p]), preferred_element_type=F32)
            for _ in range(4):
                for p in prs:
                    y = jnp.dot(jnp.concatenate([pp[p], tt_[p]], axis=0).astype(BF16), split_stack(pp[p]),
                                preferred_element_type=F32)
                    pp[p] = y[:CHUNK]
                    tt_[p] = tt_[p] + y[CHUNK:]
            for p in prs:
                tt_[p] = tt_[p] + jnp.dot(tt_[p].astype(BF16), split_stack(pp[p]), preferred_element_type=F32)

            sols = {}
            for p in prs:
                rhs = []
                for e in range(2):
                    gch, bch = cols[p][e]
                    vf = v_ref[2 * p + e, rows, :].astype(F32)
                    rhs.append(jnp.concatenate([vf * bch, ks[p] * (bch * jnp.exp(gch))], axis=1))
                sols[p] = jnp.dot(split_stack(tt_[p]), jnp.concatenate(rhs, axis=0).astype(BF16),
                                  preferred_element_type=F32)

            wss, vns = {}, {}
            for p in prs:
                for e in range(2):
                    m = 2 * p + e
                    gch, _ = cols[p][e]
                    sol = sols[p][e * CHUNK:(e + 1) * CHUNK]
                    lhs = jnp.concatenate([sol[:, DV:], qs[p] * jnp.exp(gch)], axis=0)
                    wss[m] = _bdot(lhs, st[m])
                    vns[m] = sol[:, :DV] - wss[m][:CHUNK]

            for p in prs:
                if write_out:
                    oi = jnp.dot(split_stack(aa[p]), jnp.concatenate([vns[2 * p], vns[2 * p + 1]], axis=0).astype(BF16),
                                 preferred_element_type=F32)
                for e in range(2):
                    m = 2 * p + e
                    gch, _ = cols[p][e]
                    if write_out:
                        o_ref[m, rows, :] = (wss[m][CHUNK:] + oi[e * CHUNK:(e + 1) * CHUNK]).astype(o_ref.dtype)
                    gl = gch[tl:tl + 1, :]
                    kd = ks[p] * jnp.exp(gl - gch)
                    st[m] = st[m] * jnp.exp(gl) + _tn_dot(kd.astype(BF16), vns[m].astype(BF16))
        return carry

    lax.fori_loop(0, nch, chunk_body, 0)

    @pl.when(i == nt - 1)
    def _():
        st_ref[...] = st[...]


def _gdn(qh, kh, vh, gc, gr, s0, *, reverse, tb, write_out, name, wave=8):
    b, nk, s, _ = qh.shape
    nv = vh.shape[1]
    nt = s // tb
    nch = tb // CHUNK
    d = 1 if reverse else 0
    tb_of = (lambda i: nt - 1 - i) if reverse else (lambda i: i)
    in_specs = [
        pl.BlockSpec((None, nk, tb, DK), lambda bb, i: (bb, 0, tb_of(i), 0)),
        pl.BlockSpec((None, nk, tb, DK), lambda bb, i: (bb, 0, tb_of(i), 0)),
        pl.BlockSpec((None, nv, tb, DV), lambda bb, i: (bb, 0, tb_of(i), 0)),
        pl.BlockSpec((None, tb, LANES), lambda bb, i: (bb, tb_of(i), 0)),
        pl.BlockSpec((None, nv, nch, CHUNK), lambda bb, i: (bb, 2 + d, tb_of(i), 0)),
        pl.BlockSpec((None, nv, DK, DV), lambda bb, i: (bb, 0, 0, 0)),
    ]
    st_spec = pl.BlockSpec((None, nv, DK, DV), lambda bb, i: (bb, 0, 0, 0))
    st_shape = jax.ShapeDtypeStruct((b, nv, DK, DV), F32)
    if write_out:
        out_specs = [pl.BlockSpec((None, nv, tb, DV), lambda bb, i: (bb, 0, tb_of(i), 0)), st_spec]
        out_shape = [jax.ShapeDtypeStruct((b, nv, s, DV), BF16), st_shape]
    else:
        out_specs, out_shape = [st_spec], [st_shape]
    return pl.pallas_call(
        functools.partial(_gdn_kernel, nt=nt, tb=tb, reverse=reverse, write_out=write_out, wave=wave),
        grid=(b, nt), in_specs=in_specs, out_specs=out_specs, out_shape=out_shape,
        scratch_shapes=[pltpu.VMEM((nv, DK, DV), F32)],
        compiler_params=_cparams(("parallel", "arbitrary")), name=name,
    )(qh, kh, vh, gc, gr, s0)


def _gelu_tanh(x):
    return 0.5 * x * (1.0 + jnp.tanh(0.7978845608028654 * (x + 0.044715 * (x * x * x))))


def _aout_kernel(yf_ref, yb_ref, ag_ref, w_ref, o_ref, a_scr):
    @pl.when(pl.program_id(2) == 0)
    def _():
        y = yf_ref[...].astype(F32) + yb_ref[...].astype(F32)
        a_scr[...] = (y * _gelu_tanh(ag_ref[...].astype(F32))).astype(a_scr.dtype)

    o_ref[...] = jnp.dot(a_scr[...], w_ref[...], preferred_element_type=F32)


def _aout(yf, yb, ag, w, *, tm, tn):
    b, s, wa = yf.shape
    n = w.shape[1]
    act = pl.BlockSpec((None, tm, wa), lambda bb, i, j: (bb, i, 0))
    return pl.pallas_call(
        _aout_kernel, grid=(b, s // tm, n // tn),
        in_specs=[act, act, act, pl.BlockSpec((wa, tn), lambda bb, i, j: (0, j))],
        out_specs=pl.BlockSpec((None, tm, tn), lambda bb, i, j: (bb, i, j)),
        out_shape=jax.ShapeDtypeStruct((b, s, n), F32),
        scratch_shapes=[pltpu.VMEM((tm, wa), BF16)],
        compiler_params=_cparams(("parallel", "parallel", "arbitrary")), name="aout")(yf, yb, ag, w)


def _bout_kernel(of_ref, ob_ref, zl_ref, zh_ref, nw_ref, ya_ref, ga_ref, gb_ref, w_ref, o_ref, y_scr):
    ncol, tr = o_ref.shape[0], o_ref.shape[1]
    tm = ncol * tr

    @pl.when(pl.program_id(3) == 0)
    def _():
        nw = nw_ref[...]
        nvh = of_ref.shape[0]
        half = nvh // 2
        avg = jnp.full((DV, DV), 1.0 / DV, BF16)
        for h in range(nvh):
            o = (of_ref[h].astype(F32) + ob_ref[h].astype(F32)).reshape(tm, DV)
            z_ref = zl_ref if h < half else zh_ref
            hh = h % half
            z = z_ref[:, :, hh * DV:(hh + 1) * DV].astype(F32).reshape(tm, DV)
            ms = jnp.dot((o * o).astype(BF16), avg, preferred_element_type=F32)
            y = o * lax.rsqrt(ms + EPS)
            y = y * nw * (z * _sigmoid(z))
            y_scr[:, h * DV:(h + 1) * DV] = y.astype(y_scr.dtype)

    yb = jnp.dot(y_scr[...], w_ref[...], preferred_element_type=F32)
    for k in range(ncol):
        ga = _sigmoid(ga_ref[k].astype(F32))
        gb = _sigmoid(gb_ref[k].astype(F32))
        ya = ya_ref[:, k].reshape(tr, ya_ref.shape[-1])
        o_ref[k] = (ga * ya + gb * yb[k * tr:(k + 1) * tr]).astype(o_ref.dtype)


def _bout(of, ob, big, mg, nw, ya, w, *, z_col, tn):
    b, nv, s, dv = of.shape
    n = w.shape[1]
    rows = s // GRID_W
    tr = min(TILE_R, rows)
    hw = nv * dv // 2
    of5 = of.reshape(b, nv, GRID_W, rows, dv)
    ob5 = ob.reshape(b, nv, GRID_W, rows, dv)
    big4 = big.reshape(b, GRID_W, rows, big.shape[-1])
    mg4 = mg.reshape(b, GRID_W, rows, mg.shape[-1])
    ya5 = ya.reshape(b, rows // SUBLANES, GRID_W, SUBLANES, n)
    o_spec = pl.BlockSpec((None, nv, SUBLANES, tr, dv), lambda bb, wi, ri, j: (bb, 0, wi, ri, 0))
    out = pl.pallas_call(
        _bout_kernel, grid=(b, GRID_W // SUBLANES, rows // tr, n // tn),
        in_specs=[o_spec, o_spec,
                  pl.BlockSpec((None, SUBLANES, tr, hw), lambda bb, wi, ri, j: (bb, wi, ri, z_col // hw)),
                  pl.BlockSpec((None, SUBLANES, tr, hw), lambda bb, wi, ri, j: (bb, wi, ri, z_col // hw + 1)),
                  pl.BlockSpec((1, dv), lambda bb, wi, ri, j: (0, 0)),
                  pl.BlockSpec((None, tr // SUBLANES, SUBLANES, SUBLANES, tn), lambda bb, wi, ri, j: (bb, ri, wi, 0, j)),
                  pl.BlockSpec((None, SUBLANES, tr, tn), lambda bb, wi, ri, j: (bb, wi, ri, j)),
                  pl.BlockSpec((None, SUBLANES, tr, tn), lambda bb, wi, ri, j: (bb, wi, ri, n // tn + j)),
                  pl.BlockSpec((nv * dv, tn), lambda bb, wi, ri, j: (0, j))],
        out_specs=pl.BlockSpec((None, SUBLANES, tr, tn), lambda bb, wi, ri, j: (bb, wi, ri, j)),
        out_shape=jax.ShapeDtypeStruct((b, GRID_W, rows, n), BF16),
        scratch_shapes=[pltpu.VMEM((SUBLANES * tr, nv * dv), BF16)],
        compiler_params=_cparams(("parallel", "parallel", "parallel", "arbitrary")), name="bout",
    )(of5, ob5, big4, big4, nw, ya5, mg4, mg4, w)
    return out


def _ffn_up_kernel(m_ref, x_ref, perm_ref, gt1_ref, wo_ref, lnw_ref, sh_ref, sc_ref, wg_ref, wu_ref,
                   x1_ref, a_ref, h_scr):
    ncol = m_ref.shape[0]
    tm, d = h_scr.shape
    ch = min(tm, 256)

    @pl.when(pl.program_id(3) == 0)
    def _():
        mr = jnp.dot(perm_ref[...], m_ref[...].reshape(tm, d), preferred_element_type=F32).astype(BF16)
        x1_ref[...] = jnp.dot(mr, wo_ref[...], preferred_element_type=F32)
        for r0 in range(0, tm, ch):
            xb = x_ref[r0 // ncol:(r0 + ch) // ncol].reshape(ch, d)
            x1 = xb + gt1_ref[...] * x1_ref[r0:r0 + ch, :]
            x1_ref[r0:r0 + ch, :] = x1
            h_scr[r0:r0 + ch, :] = _rms_mod(x1, lnw_ref[...], 1.0 + sc_ref[...], sh_ref[...]).astype(h_scr.dtype)

    h = h_scr[...]
    g = jnp.dot(h, wg_ref[...], preferred_element_type=F32)
    u = jnp.dot(h, wu_ref[...], preferred_element_type=F32)
    a_ref[...] = (g * _sigmoid(g) * u).astype(a_ref.dtype)


def _ffn_down_kernel(a_ref, x1_ref, gt2_ref, wd_ref, fw_ref, o_ref):
    ncol = o_ref.shape[1]
    tm, d = x1_ref.shape
    x2 = x1_ref[...] + gt2_ref[...] * jnp.dot(a_ref[...], wd_ref[...], preferred_element_type=F32)
    ms = jnp.mean(x2 * x2, axis=-1, keepdims=True)
    o_ref[...] = (x2 * lax.rsqrt(ms + EPS) * fw_ref[...]).reshape(tm // ncol, ncol, d)


def _ffn(m4, x, gt1, w_o, lnw, sh, sc, gt2, w_in, w_dn, fw, *, tf):
    b, s, d = x.shape
    rows = s // GRID_W
    tr = min(TILE_R, rows)
    tm = SUBLANES * tr
    fh = w_dn.shape[0]
    nf = fh // tf
    ncb, nrb = GRID_W // SUBLANES, rows // tr
    x4 = x.reshape(b, rows, GRID_W, d)
    vec = pl.BlockSpec((None, 1, d), lambda bb, wi, ri, f: (bb, 0, 0))
    par = pl.BlockSpec((1, d), lambda bb, wi, ri, f: (0, 0))
    once = pl.Buffered(1)
    x1, a = pl.pallas_call(
        _ffn_up_kernel, grid=(b, ncb, nrb, nf),
        in_specs=[pl.BlockSpec((None, SUBLANES, tr, d), lambda bb, wi, ri, f: (bb, wi, ri, 0)),
                  pl.BlockSpec((None, tr, SUBLANES, d), lambda bb, wi, ri, f: (bb, ri, wi, 0)),
                  pl.BlockSpec((tm, tm), lambda bb, wi, ri, f: (0, 0)),
                  vec,
                  pl.BlockSpec((d, d), lambda bb, wi, ri, f: (0, 0), pipeline_mode=once),
                  par, vec, vec,
                  pl.BlockSpec((d, tf), lambda bb, wi, ri, f: (0, f)),
                  pl.BlockSpec((d, tf), lambda bb, wi, ri, f: (0, nf + f))],
        out_specs=[pl.BlockSpec((None, None, None, tm, d), lambda bb, wi, ri, f: (bb, wi, ri, 0, 0)),
                   pl.BlockSpec((None, None, None, tm, tf), lambda bb, wi, ri, f: (bb, wi, ri, 0, f))],
        out_shape=[jax.ShapeDtypeStruct((b, ncb, nrb, tm, d), F32), jax.ShapeDtypeStruct((b, ncb, nrb, tm, fh), BF16)],
        scratch_shapes=[pltpu.VMEM((tm, d), BF16)],
        compiler_params=_cparams(("parallel", "parallel", "parallel", "arbitrary")), name="ffn_up",
    )(m4, x4, _strip_perm(tm), gt1, w_o, lnw, sh, sc, w_in, w_in)
    vec3 = pl.BlockSpec((None, 1, d), lambda bb, wi, ri: (bb, 0, 0))
    out = pl.pallas_call(
        _ffn_down_kernel, grid=(b, ncb, nrb),
        in_specs=[pl.BlockSpec((None, None, None, tm, fh), lambda bb, wi, ri: (bb, wi, ri, 0, 0)),
                  pl.BlockSpec((None, None, None, tm, d), lambda bb, wi, ri: (bb, wi, ri, 0, 0)),
                  vec3,
                  pl.BlockSpec((fh, d), lambda bb, wi, ri: (0, 0), pipeline_mode=once),
                  pl.BlockSpec((1, d), lambda bb, wi, ri: (0, 0))],
        out_specs=pl.BlockSpec((None, tr, SUBLANES, d), lambda bb, wi, ri: (bb, ri, wi, 0)),
        out_shape=jax.ShapeDtypeStruct((b, rows, GRID_W, d), F32),
        compiler_params=_cparams(("parallel", "parallel", "parallel")), name="ffn_down",
    )(a, x1, gt2, w_dn, fw)
    return out.reshape(b, s, d)


def _pad_pairs(t, axis):
    n = t.shape[axis]
    bw = n // NH_A
    shp = t.shape[:axis] + (NH_A // 2, 2 * bw) + t.shape[axis + 1:]
    t = t.reshape(shp)
    padw = [(0, 0)] * t.ndim
    padw[axis + 1] = (0, PAIR_W - 2 * bw)
    t = jnp.pad(t, padw)
    return t.reshape(t.shape[:axis] + (NH_A // 2 * PAIR_W,) + t.shape[axis + 2:])


def _pair_gate_weights(gw, gb):
    bw = gw.shape[1]
    npair = NH_A // 2
    g = gw.reshape(npair, 2, bw, 2, bw)
    blocks = []
    for t in range(2):
        top = jnp.pad(g[:, 0, :, t, :], ((0, 0), (0, 0), (0, PAIR_W - bw)))
        bot = jnp.pad(g[:, 1, :, t, :], ((0, 0), (0, PAIR_W - 2 * bw), (bw, PAIR_W - 2 * bw)))
        blocks.append(jnp.concatenate([top, bot], axis=1))
    wp = jnp.concatenate(blocks, axis=2)
    b = gb.reshape(npair, 2, 2, bw).transpose(0, 2, 1, 3).reshape(npair, 2, 2 * bw)
    b = jnp.pad(b, ((0, 0), (0, 0), (0, PAIR_W - 2 * bw)))
    return wp.astype(BF16), b.reshape(npair, 1, 2 * PAIR_W)


def kernel(x, c, ctx, c_ctx, w_mod, b_mod, ln1_w, ln2_w, w_in, conv_a_w, conv_a_b, lru_gate_w, lru_gate_b, lru_lambda, conv_qkv_w, gdn_a_log, gdn_dt_bias, gdn_norm_w, w_a_out, w_b_out, w_out, w_ffn_in, w_ffn_out, final_norm_w):
    depth = w_mod.shape[0]
    assert depth == 1, "context stream updates are only needed for depth > 1"
    bsz, seq, d = x.shape
    lctx = ctx.shape[1]
    rows = seq // GRID_W
    assert seq % STRIP == 0 and rows % CHUNK == 0 and lctx % CHUNK == 0
    wa = lru_lambda.shape[-1]
    wap = NH_A // 2 * PAIR_W
    qk_dim, v_dim = NK * DK, NV * DV
    qkv_dim = 2 * qk_dim + v_dim
    l = 0
    tn = 512

    wi = w_in[l]
    o_qkv, o_z, o_g, o_mg = 2 * wa, 2 * wa + qkv_dim, 2 * wa + qkv_dim + v_dim, 2 * wa + qkv_dim + v_dim + 4 * NV
    w_axp = _pad_pairs(wi[:, :wa], 1).astype(BF16)
    w_agp = _pad_pairs(wi[:, wa:2 * wa], 1).astype(BF16)
    w_qz = wi[:, o_qkv:o_g].astype(BF16)
    w_gt = wi[:, o_g:o_mg].astype(BF16)
    w_mg = wi[:, o_mg:].astype(BF16)
    cw_a = _pad_pairs(conv_a_w[l], 1)
    cb_a = _pad_pairs(conv_a_b[l][None], 1)
    nla = _pad_pairs(-LRU_C * jax.nn.softplus(-lru_lambda[l]), 1)
    gate_w = [_pair_gate_weights(lru_gate_w[l, dd], lru_gate_b[l, dd]) for dd in range(2)]
    neg_a = -jnp.exp(gdn_a_log[l]).reshape(1, 2 * NV)
    na_l = jnp.concatenate([jnp.zeros((1, 2 * NV), F32), neg_a], axis=1)
    dt_l = jnp.concatenate([jnp.zeros((1, 2 * NV), F32), gdn_dt_bias[l].reshape(1, 2 * NV)], axis=1)
    w_a = _pad_pairs(w_a_out[l], 0).astype(BF16)
    w_b = w_b_out[l].astype(BF16)
    w_o = w_out[l].astype(BF16)
    w_f1 = w_ffn_in[l].astype(BF16)
    w_f2 = w_ffn_out[l].astype(BF16)

    nrow = -(-(bsz + 1) // SUBLANES) * SUBLANES
    cc = jnp.concatenate([c, c_ctx[None], jnp.zeros((nrow - bsz - 1, d), F32)], axis=0)
    mod = _mod(cc, w_mod[l], b_mod[l][None]).reshape(nrow, N_MOD, 1, d)
    sh1, sc1, gt1, sh2, sc2, gt2 = (mod[:bsz, k] for k in range(N_MOD))
    csh1 = jnp.broadcast_to(mod[bsz, 0], (bsz, 1, d))
    csc1 = jnp.broadcast_to(mod[bsz, 1], (bsz, 1, d))
    ln1 = ln1_w[l][None]

    nax = wap // tn
    ax, ag = _inproj(x, ln1, sh1, sc1, [(w_axp, tn, nax, BF16), (w_agp, tn, nax, BF16)],
                     order="strip", tm=min(2 * STRIP, seq), tt=STRIP, name="inproj_a")
    big, mg, gts = _inproj(x, ln1, sh1, sc1,
                           [(w_qz, 2 * tn, (qkv_dim + v_dim) // (2 * tn), BF16), (w_mg, 2 * tn, d // tn, BF16), (w_gt, LANES, 1, F32)],
                           order="colmajor", name="inproj_b")
    (cax,) = _inproj(ctx, ln1, csh1, csc1, [(w_axp, tn, nax, BF16)], order="strip", tm=lctx, tt=lctx, name="inproj_ca")
    cbig, cgts = _inproj(ctx, ln1, csh1, csc1, [(w_qz, tn, qkv_dim // tn, BF16), (w_gt, LANES, 1, F32)],
                         order="raster", tm=lctx, name="inproj_cb")

    ys = []
    for dd in range(2):
        wg, bg = gate_w[dd]
        h0 = jnp.zeros((bsz, 1, wap), F32)
        _, hc = _lru(cax, cw_a, cb_a, wg, bg, nla[dd:dd + 1], h0, reverse=bool(dd), tt=lctx, name=f"lru_c{dd}")
        y, _ = _lru(ax, cw_a, cb_a, wg, bg, nla[dd:dd + 1], hc, reverse=bool(dd), tt=STRIP, name=f"lru_x{dd}")
        ys.append(y)

    cw_q = conv_qkv_w[l]

    def prep_all(q_arr, g_arr, tt, tag):
        qh = _prep(q_arr, cw_q, mode="q", col0=0, nheads=NK, tt=tt, name="prep_q" + tag)
        kh = _prep(q_arr, cw_q, mode="k", col0=qk_dim // 512, nheads=NK, tt=tt, name="prep_k" + tag)
        vh = _prep(q_arr, cw_q, mode="v", col0=2 * qk_dim // 512, nheads=NV, tt=tt, name="prep_v" + tag)
        gc, gr = _gates(g_arr, na_l, dt_l, tt=tt)
        s_len = q_arr.shape[1]
        gr = gr.reshape(bsz, 4, NK, 2, s_len).transpose(0, 1, 3, 2, 4)
        return qh, kh, vh, gc, gr.reshape(bsz, LANES, s_len // CHUNK, CHUNK)

    pc = prep_all(cbig, cgts, lctx, "_c")
    px = prep_all(big, gts, min(512, seq), "_x")
    os_ = []
    for dd in range(2):
        s0 = jnp.zeros((bsz, NV, DK, DV), F32)
        (sc_state,) = _gdn(*pc, s0, reverse=bool(dd), tb=lctx, write_out=False, name=f"gdn_c{dd}")
        o, _ = _gdn(*px, sc_state, reverse=bool(dd), tb=min(512, seq), write_out=True, name=f"gdn_x{dd}")
        os_.append(o)

    ya = _aout(ys[0], ys[1], ag, w_a, tm=min(512, seq), tn=tn)
    m4 = _bout(os_[0], os_[1], big, mg, gdn_norm_w[l][None], ya, w_b, z_col=qkv_dim, tn=tn)
    return _ffn(m4, x, gt1, w_o, ln2_w[l][None], sh2, sc2, gt2, w_f1, w_f2, final_norm_w[None], tf=512)
```

```python
import functools

import jax
import jax.numpy as jnp
import numpy as np
from jax import lax
from jax.experimental import pallas as pl
from jax.experimental.pallas import tpu as pltpu

F32 = jnp.float32
BF16 = jnp.bfloat16
HIGHEST = lax.Precision.HIGHEST

EPS = 1e-6
GRID_W = 64
CONV_W = 4
N_MOD = 6
NH_A = 16
LRU_C = 8.0
NK, DK, NV, DV = 16, 128, 32, 128
CHUNK = 64
LANES = 128
SUBLANES = 8
HALO = 16
PAIR_W = 384
STRIP = SUBLANES * GRID_W
TILE_R = 64
VMEM_LIMIT = 63 * 1024 * 1024


def _cparams(sem):
    return pltpu.CompilerParams(dimension_semantics=sem, vmem_limit_bytes=VMEM_LIMIT)


def _nt_dot(a, b, **kw):
    return lax.dot_general(a, b, (((1,), (1,)), ((), ())), preferred_element_type=F32, **kw)


def _tn_dot(a, b):
    return lax.dot_general(a, b, (((0,), (0,)), ((), ())), preferred_element_type=F32)


def _bdot(a, b):
    return jnp.dot(a.astype(BF16), b.astype(BF16), preferred_element_type=F32)


def _sigmoid(x):
    return 0.5 * jnp.tanh(0.5 * x) + 0.5


def _mod_kernel(c_ref, w_ref, b_ref, o_ref):
    s = c_ref[...]
    s = s * _sigmoid(s)
    o_ref[...] = jnp.dot(s, w_ref[...], precision=HIGHEST, preferred_element_type=F32) + b_ref[...]


def _mod(cc, w, b, tn=1024):
    m, d = cc.shape
    n = w.shape[1]
    return pl.pallas_call(
        _mod_kernel, grid=(n // tn,),
        in_specs=[pl.BlockSpec((m, d), lambda j: (0, 0)),
                  pl.BlockSpec((d, tn), lambda j: (0, j)),
                  pl.BlockSpec((1, tn), lambda j: (0, j))],
        out_specs=pl.BlockSpec((m, tn), lambda j: (0, j)),
        out_shape=jax.ShapeDtypeStruct((m, n), F32),
        compiler_params=_cparams(("parallel",)), name="mod")(cc, w, b)


def _rms_mod(xb, lnw, sc1, sh):
    xb = xb.astype(F32)
    ms = jnp.mean(xb * xb, axis=-1, keepdims=True)
    return (xb * lax.rsqrt(ms + EPS) * lnw) * sc1 + sh


def _strip_perm(tt):
    seg = tt // SUBLANES
    i = np.arange(tt)
    p = np.zeros((tt, tt), np.float32)
    p[i, (i % SUBLANES) * seg + i // SUBLANES] = 1.0
    return jnp.asarray(p, BF16)


def _colmajor_perm(rows):
    n = SUBLANES * rows
    i = np.arange(n)
    p = np.zeros((n, n), np.float32)
    p[i, (i % rows) * SUBLANES + i // rows] = 1.0
    return jnp.asarray(p, BF16)


def _inproj_kernel(*refs, order, tiles, acts, tt):
    nseg = len(tiles)
    x_ref, lnw_ref, sh_ref, sc_ref = refs[:4]
    pos = 4 if order == "raster" else 5
    w_refs = refs[pos:pos + nseg]
    o_refs = refs[pos + nseg:pos + 2 * nseg]
    scr = refs[pos + 2 * nseg:]
    h_scr = scr[0]
    tm, d = h_scr.shape
    j = pl.program_id(2)

    @pl.when(j == 0)
    def _():
        lnw = lnw_ref[...]
        sc1 = 1.0 + sc_ref[...]
        sh = sh_ref[...]
        dst = h_scr if order == "raster" else scr[1]
        ch = min(tm, 256)
        for r0 in range(0, tm, ch):
            if order == "colmajor":
                xb = x_ref[r0 // SUBLANES:(r0 + ch) // SUBLANES].reshape(ch, d)
            else:
                xb = x_ref[r0:r0 + ch, :]
            dst[r0:r0 + ch, :] = _rms_mod(xb, lnw, sc1, sh).astype(dst.dtype)
        if order != "raster":
            perm_ref = refs[4]
            for r0 in range(0, tm, tt):
                for q0 in range(0, tt, ch):
                    h_scr[r0 + q0:r0 + q0 + ch, :] = jnp.dot(
                        perm_ref[q0:q0 + ch, :], dst[r0:r0 + tt, :], preferred_element_type=F32).astype(h_scr.dtype)

    lo = 0
    for k in range(nseg):
        @pl.when((j >= lo) & (j < lo + tiles[k]))
        def _(k=k):
            y = jnp.dot(h_scr[...], w_refs[k][...], preferred_element_type=F32)
            if acts[k] is not None:
                y = acts[k](y)
            o_refs[k][...] = y.astype(o_refs[k].dtype)
        lo += tiles[k]


def _inproj(x, lnw, sh, sc, segs, *, order, tm=None, tt=None, name="inproj"):
    b, s, d = x.shape
    ins = [None, lnw, sh, sc]
    if order == "colmajor":
        rows = s // GRID_W
        tm = SUBLANES * rows
        ins[0] = x.reshape(b, rows, GRID_W, d)
        x_spec = pl.BlockSpec((None, rows, SUBLANES, d), lambda bb, i, j: (bb, 0, i, 0))
    else:
        ins[0] = x
        x_spec = pl.BlockSpec((None, tm, d), lambda bb, i, j: (bb, i, 0))
    in_specs = [x_spec,
                pl.BlockSpec((1, d), lambda bb, i, j: (0, 0)),
                pl.BlockSpec((None, 1, d), lambda bb, i, j: (bb, 0, 0)),
                pl.BlockSpec((None, 1, d), lambda bb, i, j: (bb, 0, 0))]
    scratch = [pltpu.VMEM((tm, d), BF16)]
    if order != "raster":
        if order == "colmajor":
            tt = tm
        ins.append(_strip_perm(tt) if order == "strip" else _colmajor_perm(s // GRID_W))
        in_specs.append(pl.BlockSpec((tt, tt), lambda bb, i, j: (0, 0), pipeline_mode=pl.Buffered(1)))
        scratch.append(pltpu.VMEM((tm, d), BF16))
    tiles = tuple(sg[2] for sg in segs)
    acts = tuple(sg[4] if len(sg) > 4 else None for sg in segs)
    out_specs, out_shape = [], []
    lo = 0
    for w, tn, nt, dt in (sg[:4] for sg in segs):
        clip = functools.partial(lambda j, lo, nt: jnp.clip(j - lo, 0, nt - 1), lo=lo, nt=nt)
        ins.append(w)
        in_specs.append(pl.BlockSpec((d, tn), lambda bb, i, j, clip=clip: (0, clip(j))))
        out_specs.append(pl.BlockSpec((None, tm, tn), lambda bb, i, j, clip=clip: (bb, i, clip(j))))
        out_shape.append(jax.ShapeDtypeStruct((b, s, tn * nt), dt))
        lo += nt
    return pl.pallas_call(
        functools.partial(_inproj_kernel, order=order, tiles=tiles, acts=acts, tt=tt),
        grid=(b, s // tm, lo), in_specs=in_specs, out_specs=out_specs, out_shape=out_shape,
        scratch_shapes=scratch,
        compiler_params=_cparams(("parallel", "parallel", "arbitrary")), name=name)(*ins)


def _lru_kernel(xm_ref, xp_ref, xn_ref, cw_ref, cb_ref, wg_ref, bg_ref, nla_ref, h0_ref,
                y_ref, ht_ref, pad, a_scr, b_scr, carry, *, nt, tt, reverse):
    i = pl.program_id(2)
    tb = nt - 1 - i if reverse else i
    seg = tt // SUBLANES
    gm = min(tt, 256)

    @pl.when(i == 0)
    def _():
        carry[...] = h0_ref[...]

    sub = lax.broadcasted_iota(jnp.int32, (SUBLANES, PAIR_W), 0)
    first, last = tb == 0, tb == nt - 1
    prev_last = jnp.where(first, 0.0, xp_ref[SUBLANES:2 * SUBLANES, :].astype(F32))
    nxt0 = jnp.where(last, 0.0, xn_ref[0:SUBLANES, :].astype(F32))
    nxt1 = jnp.where(last, 0.0, xn_ref[SUBLANES:2 * SUBLANES, :].astype(F32))
    x_end = xm_ref[tt - SUBLANES:tt, :].astype(F32)
    x_0 = xm_ref[0:SUBLANES, :].astype(F32)
    x_1 = xm_ref[SUBLANES:2 * SUBLANES, :].astype(F32)
    up = SUBLANES - 1
    pad[0:SUBLANES, :] = jnp.where(sub == 0, pltpu.roll(prev_last, 1, 0), pltpu.roll(x_end, 1, 0))
    pad[SUBLANES:SUBLANES + tt, :] = xm_ref[...].astype(F32)
    pad[SUBLANES + tt:2 * SUBLANES + tt, :] = jnp.where(sub == up, pltpu.roll(nxt0, up, 0), pltpu.roll(x_0, up, 0))
    pad[2 * SUBLANES + tt:3 * SUBLANES + tt, :] = jnp.where(sub == up, pltpu.roll(nxt1, up, 0), pltpu.roll(x_1, up, 0))

    for r0 in range(0, tt, gm):
        xc = cb_ref[...] + cw_ref[0:1, :] * pad[r0:r0 + gm, :]
        for k in range(1, CONV_W):
            xc = xc + cw_ref[k:k + 1, :] * pad[SUBLANES * k + r0:SUBLANES * k + r0 + gm, :]
        gates = _bdot(xc, wg_ref[...]) + bg_ref[...]
        r = _sigmoid(gates[:, :PAIR_W])
        ig = _sigmoid(gates[:, PAIR_W:])
        la = nla_ref[...] * r
        a = jnp.exp(la)
        a_scr[r0:r0 + gm, :] = a
        b_scr[r0:r0 + gm, :] = jnp.sqrt(-jnp.tanh(la) * (a * a + 1.0)) * (ig * xc)

    def scan_body(jj, hp):
        j = seg - 1 - jj if reverse else jj
        rows = pl.ds(pl.multiple_of(j * SUBLANES, SUBLANES), SUBLANES)
        a = a_scr[rows, :]
        h = a * hp[0] + b_scr[rows, :]
        p = a * hp[1]
        b_scr[rows, :] = h
        a_scr[rows, :] = p
        return h, p

    h, p = lax.fori_loop(0, seg, scan_body,
                         (jnp.zeros((SUBLANES, PAIR_W), F32), jnp.ones((SUBLANES, PAIR_W), F32)), unroll=8)

    cin = carry[...]
    cvec = jnp.zeros((SUBLANES, PAIR_W), F32)
    for s in (range(SUBLANES - 1, -1, -1) if reverse else range(SUBLANES)):
        cvec = jnp.where(sub == s, jnp.broadcast_to(cin, (SUBLANES, PAIR_W)), cvec)
        cin = p[s:s + 1, :] * cin + h[s:s + 1, :]
    carry[...] = cin
    cvec2 = jnp.concatenate([cvec, cvec], axis=0)

    def fix_body(j, _):
        rows = pl.ds(pl.multiple_of(j * 2 * SUBLANES, 2 * SUBLANES), 2 * SUBLANES)
        y_ref[rows, :] = (b_scr[rows, :] + a_scr[rows, :] * cvec2).astype(y_ref.dtype)
        return 0

    lax.fori_loop(0, seg // 2, fix_body, 0, unroll=4)

    @pl.when(i == nt - 1)
    def _():
        ht_ref[...] = carry[...]


def _lru(ax, cw, cb, wg, bg, nla, h0, *, reverse, tt, name):
    b, s, _ = ax.shape
    npair = cw.shape[1] // PAIR_W
    nt = s // tt
    nh = tt // HALO
    last_blk = s // HALO - 1
    tb_of = (lambda i: nt - 1 - i) if reverse else (lambda i: i)
    par = lambda bb, p, i: (0, p)
    return pl.pallas_call(
        functools.partial(_lru_kernel, nt=nt, tt=tt, reverse=reverse),
        grid=(b, npair, nt),
        in_specs=[
            pl.BlockSpec((None, tt, PAIR_W), lambda bb, p, i: (bb, tb_of(i), p)),
            pl.BlockSpec((None, HALO, PAIR_W), lambda bb, p, i: (bb, jnp.maximum(tb_of(i) * nh - 1, 0), p)),
            pl.BlockSpec((None, HALO, PAIR_W), lambda bb, p, i: (bb, jnp.minimum((tb_of(i) + 1) * nh, last_blk), p)),
            pl.BlockSpec((CONV_W, PAIR_W), par),
            pl.BlockSpec((1, PAIR_W), par),
            pl.BlockSpec((None, PAIR_W, 2 * PAIR_W), lambda bb, p, i: (p, 0, 0)),
            pl.BlockSpec((None, 1, 2 * PAIR_W), lambda bb, p, i: (p, 0, 0)),
            pl.BlockSpec((1, PAIR_W), par),
            pl.BlockSpec((None, 1, PAIR_W), lambda bb, p, i: (bb, 0, p)),
        ],
        out_specs=[pl.BlockSpec((None, tt, PAIR_W), lambda bb, p, i: (bb, tb_of(i), p)),
                   pl.BlockSpec((None, 1, PAIR_W), lambda bb, p, i: (bb, 0, p))],
        out_shape=[jax.ShapeDtypeStruct((b, s, npair * PAIR_W), BF16),
                   jax.ShapeDtypeStruct((b, 1, npair * PAIR_W), F32)],
        scratch_shapes=[pltpu.VMEM((tt + 3 * SUBLANES, PAIR_W), F32),
                        pltpu.VMEM((tt, PAIR_W), F32),
                        pltpu.VMEM((tt, PAIR_W), F32),
                        pltpu.VMEM((1, PAIR_W), F32)],
        compiler_params=_cparams(("parallel", "parallel", "arbitrary")), name=name,
    )(ax, ax, ax, cw, cb, wg, bg, nla, h0)


def _prep_kernel(xm_ref, xp_ref, xn_ref, cw_ref, o_ref, pad, *, nt, tt, mode):
    i = pl.program_id(1)
    pad[0:HALO, :] = jnp.where(i == 0, 0.0, xp_ref[...].astype(F32))
    pad[HALO:HALO + tt, :] = xm_ref[...].astype(F32)
    pad[HALO + tt:2 * HALO + tt, :] = jnp.where(i == nt - 1, 0.0, xn_ref[...].astype(F32))
    y = cw_ref[0:1, :] * pad[HALO - 1:HALO - 1 + tt, :]
    for k in range(1, CONV_W):
        y = y + cw_ref[k:k + 1, :] * pad[HALO - 1 + k:HALO - 1 + k + tt, :]
    y = y * _sigmoid(y)
    for h in range(o_ref.shape[0]):
        yh = y[:, h * LANES:(h + 1) * LANES]
        if mode != "v":
            yh = yh * lax.rsqrt(jnp.sum(yh * yh, axis=-1, keepdims=True) + EPS)
        if mode == "q":
            yh = yh * (DK ** -0.5)
        o_ref[h] = yh.astype(o_ref.dtype)


def _prep(q_arr, cw, *, mode, col0, nheads, tt, name):
    b, s, _ = q_arr.shape
    nh = 4
    width = nh * LANES
    nt = s // tt
    nhalo = tt // HALO
    last_blk = s // HALO - 1
    return pl.pallas_call(
        functools.partial(_prep_kernel, nt=nt, tt=tt, mode=mode),
        grid=(b, nt, nheads // nh),
        in_specs=[pl.BlockSpec((None, tt, width), lambda bb, i, j: (bb, i, col0 + j)),
                  pl.BlockSpec((None, HALO, width), lambda bb, i, j: (bb, jnp.maximum(i * nhalo - 1, 0), col0 + j)),
                  pl.BlockSpec((None, HALO, width), lambda bb, i, j: (bb, jnp.minimum((i + 1) * nhalo, last_blk), col0 + j)),
                  pl.BlockSpec((CONV_W, width), lambda bb, i, j: (0, col0 + j))],
        out_specs=pl.BlockSpec((None, nh, tt, LANES), lambda bb, i, j: (bb, j, i, 0)),
        out_shape=jax.ShapeDtypeStruct((b, nheads, s, LANES), BF16),
        scratch_shapes=[pltpu.VMEM((tt + 2 * HALO, width), F32)],
        compiler_params=_cparams(("parallel", "parallel", "parallel")), name=name,
    )(q_arr, q_arr, q_arr, cw)


def _gates_kernel(x_ref, na_ref, dt_ref, oc_ref, or_ref):
    x = x_ref[...].astype(F32)
    lane = lax.broadcasted_iota(jnp.int32, x.shape, 1)
    z = x + dt_ref[...]
    sp = jnp.maximum(z, 0.0) + jnp.log1p(jnp.exp(-jnp.abs(z)))
    val = jnp.where(lane < 2 * NV, _sigmoid(x), na_ref[...] * sp)
    oc_ref[...] = val
    or_ref[...] = val.T


def _gates(g_arr, na, dtb, *, tt):
    b, s, _ = g_arr.shape
    return pl.pallas_call(
        _gates_kernel, grid=(b, s // tt),
        in_specs=[pl.BlockSpec((None, tt, LANES), lambda bb, i: (bb, i, 0)),
                  pl.BlockSpec((1, LANES), lambda bb, i: (0, 0)),
                  pl.BlockSpec((1, LANES), lambda bb, i: (0, 0))],
        out_specs=[pl.BlockSpec((None, tt, LANES), lambda bb, i: (bb, i, 0)),
                   pl.BlockSpec((None, LANES, tt), lambda bb, i: (bb, 0, i))],
        out_shape=[jax.ShapeDtypeStruct((b, s, LANES), F32), jax.ShapeDtypeStruct((b, LANES, s), F32)],
        compiler_params=_cparams(("parallel", "parallel")), name="gdn_gates")(g_arr, na, dtb)


def _gdn_kernel(q_ref, k_ref, v_ref, gc_ref, gr_ref, s0_ref, *rest, nt, tb, reverse, write_out, wave):
    if write_out:
        o_ref, st_ref, st, sol_scr, cs_scr, a_scr = rest
    else:
        st_ref, st, sol_scr, cs_scr = rest
    i = pl.program_id(1)
    nch = tb // CHUNK
    nkh = q_ref.shape[0]
    lane_b = NV if reverse else 0
    lane_g = 2 * NV + lane_b

    @pl.when(i == 0)
    def _():
        st[...] = s0_ref[...]

    c2 = 2 * CHUNK
    row = lax.broadcasted_iota(jnp.int32, (CHUNK, c2), 0)
    lane = lax.broadcasted_iota(jnp.int32, (CHUNK, c2), 1)
    colm = lane % CHUNK
    left = lane < CHUNK
    if reverse:
        incl, strict, tl = row <= colm, row < colm, 0
    else:
        incl, strict, tl = row >= colm, row > colm, CHUNK - 1
    eye2 = (row == colm).astype(F32)
    r64 = lax.broadcasted_iota(jnp.int32, (CHUNK, CHUNK), 0)
    c64 = lax.broadcasted_iota(jnp.int32, (CHUNK, CHUNK), 1)
    tri = (r64 <= c64 if reverse else r64 >= c64).astype(F32)
    tri_t = (c64 <= r64 if reverse else c64 >= r64).astype(F32)
    z64 = jnp.zeros((CHUNK, CHUNK), F32)
    tri_t_l = jnp.concatenate([tri_t, z64], axis=1)
    tri_t_r = jnp.concatenate([z64, tri_t], axis=1)
    r128 = lax.broadcasted_iota(jnp.int32, (c2, c2), 0)
    l128 = lax.broadcasted_iota(jnp.int32, (c2, c2), 1)
    diag_blocks = (r128 < CHUNK) == (l128 < CHUNK)

    def split_stack(x2):
        return jnp.where(diag_blocks, jnp.concatenate([x2, x2], axis=0), 0.0).astype(BF16)

    def chunk_rows(c):
        return pl.ds(pl.multiple_of(c * CHUNK, CHUNK), CHUNK)

    def head(c, slot):
        rows = chunk_rows(c)
        gcol = gc_ref[rows, :]
        cs_col = jnp.dot(tri, gcol, precision=HIGHEST, preferred_element_type=F32)
        cs_scr[slot] = cs_col
        grow = gr_ref[:, pl.ds(c, 1), :].reshape(2 * nkh, CHUNK)
        gc_row2 = (jnp.dot(grow[:nkh], tri_t_l, precision=HIGHEST, preferred_element_type=F32)
                   + jnp.dot(grow[nkh:], tri_t_r, precision=HIGHEST, preferred_element_type=F32))

        for p0 in range(0, nkh, wave):
            prs = range(p0, min(p0 + wave, nkh))
            ks, pp, tt_, cols = {}, {}, {}, {}
            for p in prs:
                k = k_ref[p, rows, :]
                q = q_ref[p, rows, :]
                kq = _nt_dot(jnp.concatenate([k, q], axis=0), jnp.concatenate([k, k], axis=0))
                ha, hb = 2 * p, 2 * p + 1
                gca, gcb = cs_col[:, lane_g + ha:lane_g + ha + 1], cs_col[:, lane_g + hb:lane_g + hb + 1]
                bca, bcb = gcol[:, lane_b + ha:lane_b + ha + 1], gcol[:, lane_b + hb:lane_b + hb + 1]
                gcc2 = jnp.where(left, gca, gcb)
                bc2 = jnp.where(left, bca, bcb)
                decay2 = jnp.where(incl, jnp.exp(jnp.where(incl, gcc2 - gc_row2[p:p + 1, :], 0.0)), 0.0)
                pn = jnp.where(strict, -(kq[:CHUNK] * bc2) * decay2, 0.0)
                ks[p] = k.astype(F32)
                pp[p] = pn
                tt_[p] = eye2 + pn
                if write_out:
                    a_scr[slot, p] = split_stack(kq[CHUNK:] * decay2)
                cols[p] = ((gca, bca), (gcb, bcb))
            yield

            for p in prs:
                pp[p] = jnp.dot(pp[p].astype(BF16), split_stack(pp[p]), preferred_element_type=F32)
            yield
            for _ in range(4):
                for p in prs:
                    y = jnp.dot(jnp.concatenate([pp[p], tt_[p]], axis=0).astype(BF16), split_stack(pp[p]),
                                preferred_element_type=F32)
                    pp[p] = y[:CHUNK]
                    tt_[p] = tt_[p] + y[CHUNK:]
                yield
            for p in prs:
                tt_[p] = tt_[p] + jnp.dot(tt_[p].astype(BF16), split_stack(pp[p]), preferred_element_type=F32)
            yield

            for p in prs:
                rhs = []
                for e in range(2):
                    gch, bch = cols[p][e]
                    vf = v_ref[2 * p + e, rows, :].astype(F32)
                    rhs.append(jnp.concatenate([vf * bch, ks[p] * (bch * jnp.exp(gch))], axis=1))
                sol_scr[slot, p] = jnp.dot(split_stack(tt_[p]), jnp.concatenate(rhs, axis=0).astype(BF16),
                                           preferred_element_type=F32)

    def tail(c, slot):
        rows = chunk_rows(c)
        cs_col = cs_scr[slot]
        wss, vns, gcs = {}, {}, {}
        for p in range(nkh):
            qf = q_ref[p, rows, :].astype(F32)
            for e in range(2):
                m = 2 * p + e
                gcs[m] = cs_col[:, lane_g + m:lane_g + m + 1]
                sol = sol_scr[slot, p, e * CHUNK:(e + 1) * CHUNK, :]
                lhs = jnp.concatenate([sol[:, DV:], qf * jnp.exp(gcs[m])], axis=0)
                wss[m] = _bdot(lhs, st[m])
                vns[m] = sol[:, :DV] - wss[m][:CHUNK]
            if p % 4 == 3:
                yield

        for p in range(nkh):
            kf = k_ref[p, rows, :].astype(F32)
            if write_out:
                oi = jnp.dot(a_scr[slot, p], jnp.concatenate([vns[2 * p], vns[2 * p + 1]], axis=0).astype(BF16),
                             preferred_element_type=F32)
            for e in range(2):
                m = 2 * p + e
                if write_out:
                    o_ref[m, rows, :] = (wss[m][CHUNK:] + oi[e * CHUNK:(e + 1) * CHUNK]).astype(o_ref.dtype)
                gl = gcs[m][tl:tl + 1, :]
                kd = kf * jnp.exp(gl - gcs[m])
                st[m] = st[m] * jnp.exp(gl) + _tn_dot(kd.astype(BF16), vns[m].astype(BF16))
            if p % 4 == 3:
                yield

    def run(*gens):
        live = list(gens)
        while live:
            for g in list(live):
                try:
                    next(g)
                except StopIteration:
                    live.remove(g)

    def body(cc, carry):
        c = nch - 1 - cc if reverse else cc
        run(head(c, 0))
        run(tail(c, 0))
        return carry

    lax.fori_loop(0, nch, body, 0)

    @pl.when(i == nt - 1)
    def _():
        st_ref[...] = st[...]


def _gdn(qh, kh, vh, gc, gr, s0, *, reverse, tb, write_out, name, wave=16):
    b, nk, s, _ = qh.shape
    nv = vh.shape[1]
    nt = s // tb
    nch = tb // CHUNK
    d = 1 if reverse else 0
    tb_of = (lambda i: nt - 1 - i) if reverse else (lambda i: i)
    in_specs = [
        pl.BlockSpec((None, nk, tb, DK), lambda bb, i: (bb, 0, tb_of(i), 0)),
        pl.BlockSpec((None, nk, tb, DK), lambda bb, i: (bb, 0, tb_of(i), 0)),
        pl.BlockSpec((None, nv, tb, DV), lambda bb, i: (bb, 0, tb_of(i), 0)),
        pl.BlockSpec((None, tb, LANES), lambda bb, i: (bb, tb_of(i), 0)),
        pl.BlockSpec((None, nv, nch, CHUNK), lambda bb, i: (bb, 2 + d, tb_of(i), 0)),
        pl.BlockSpec((None, nv, DK, DV), lambda bb, i: (bb, 0, 0, 0)),
    ]
    st_spec = pl.BlockSpec((None, nv, DK, DV), lambda bb, i: (bb, 0, 0, 0))
    st_shape = jax.ShapeDtypeStruct((b, nv, DK, DV), F32)
    if write_out:
        out_specs = [pl.BlockSpec((None, nv, tb, DV), lambda bb, i: (bb, 0, tb_of(i), 0)), st_spec]
        out_shape = [jax.ShapeDtypeStruct((b, nv, s, DV), BF16), st_shape]
    else:
        out_specs, out_shape = [st_spec], [st_shape]
    scratch = [pltpu.VMEM((nv, DK, DV), F32),
               pltpu.VMEM((1, nk, 2 * CHUNK, DV + DK), F32),
               pltpu.VMEM((1, CHUNK, LANES), F32)]
    if write_out:
        scratch.append(pltpu.VMEM((1, nk, 2 * CHUNK, 2 * CHUNK), BF16))
    return pl.pallas_call(
        functools.partial(_gdn_kernel, nt=nt, tb=tb, reverse=reverse, write_out=write_out, wave=wave),
        grid=(b, nt), in_specs=in_specs, out_specs=out_specs, out_shape=out_shape,
        scratch_shapes=scratch,
        compiler_params=_cparams(("parallel", "arbitrary")), name=name,
    )(qh, kh, vh, gc, gr, s0)


def _gelu_tanh(x):
    return 0.5 * x * (1.0 + jnp.tanh(0.7978845608028654 * (x + 0.044715 * (x * x * x))))


def _aout_kernel(yf_ref, yb_ref, ag_ref, w_ref, o_ref, a_scr):
    @pl.when(pl.program_id(2) == 0)
    def _():
        y = yf_ref[...].astype(F32) + yb_ref[...].astype(F32)
        a_scr[...] = (y * ag_ref[...].astype(F32)).astype(a_scr.dtype)

    o_ref[...] = jnp.dot(a_scr[...], w_ref[...], preferred_element_type=F32)


def _aout(yf, yb, ag, w, *, tm, tn):
    b, s, wa = yf.shape
    n = w.shape[1]
    act = pl.BlockSpec((None, tm, wa), lambda bb, i, j: (bb, i, 0))
    return pl.pallas_call(
        _aout_kernel, grid=(b, s // tm, n // tn),
        in_specs=[act, act, act, pl.BlockSpec((wa, tn), lambda bb, i, j: (0, j))],
        out_specs=pl.BlockSpec((None, tm, tn), lambda bb, i, j: (bb, i, j)),
        out_shape=jax.ShapeDtypeStruct((b, s, n), F32),
        scratch_shapes=[pltpu.VMEM((tm, wa), BF16)],
        compiler_params=_cparams(("parallel", "parallel", "arbitrary")), name="aout")(yf, yb, ag, w)


def _bout_kernel(of_ref, ob_ref, zl_ref, zh_ref, nw_ref, ya_ref, ga_ref, gb_ref, w_ref, o_ref, y_scr):
    ncol, tr = o_ref.shape[0], o_ref.shape[1]
    tm = ncol * tr

    @pl.when(pl.program_id(3) == 0)
    def _():
        nw = nw_ref[...]
        nvh = of_ref.shape[0]
        half = nvh // 2
        avg = jnp.full((DV, DV), 1.0 / DV, BF16)
        for h in range(nvh):
            o = (of_ref[h].astype(F32) + ob_ref[h].astype(F32)).reshape(tm, DV)
            z_ref = zl_ref if h < half else zh_ref
            hh = h % half
            z = z_ref[:, :, hh * DV:(hh + 1) * DV].astype(F32).reshape(tm, DV)
            ms = jnp.dot((o * o).astype(BF16), avg, preferred_element_type=F32)
            y = o * lax.rsqrt(ms + EPS)
            y = y * nw * z
            y_scr[:, h * DV:(h + 1) * DV] = y.astype(y_scr.dtype)

    yb = jnp.dot(y_scr[...], w_ref[...], preferred_element_type=F32)
    for k in range(ncol):
        ga = ga_ref[k].astype(F32)
        gb = gb_ref[k].astype(F32)
        ya = ya_ref[:, k].reshape(tr, ya_ref.shape[-1])
        o_ref[k] = (ga * ya + gb * yb[k * tr:(k + 1) * tr]).astype(o_ref.dtype)


def _bout(of, ob, big, mg, nw, ya, w, *, z_col, tn):
    b, nv, s, dv = of.shape
    n = w.shape[1]
    rows = s // GRID_W
    tr = min(TILE_R, rows)
    hw = nv * dv // 2
    of5 = of.reshape(b, nv, GRID_W, rows, dv)
    ob5 = ob.reshape(b, nv, GRID_W, rows, dv)
    big4 = big.reshape(b, GRID_W, rows, big.shape[-1])
    mg4 = mg.reshape(b, GRID_W, rows, mg.shape[-1])
    ya5 = ya.reshape(b, rows // SUBLANES, GRID_W, SUBLANES, n)
    o_spec = pl.BlockSpec((None, nv, SUBLANES, tr, dv), lambda bb, wi, ri, j: (bb, 0, wi, ri, 0))
    out = pl.pallas_call(
        _bout_kernel, grid=(b, GRID_W // SUBLANES, rows // tr, n // tn),
        in_specs=[o_spec, o_spec,
                  pl.BlockSpec((None, SUBLANES, tr, hw), lambda bb, wi, ri, j: (bb, wi, ri, z_col // hw)),
                  pl.BlockSpec((None, SUBLANES, tr, hw), lambda bb, wi, ri, j: (bb, wi, ri, z_col // hw + 1)),
                  pl.BlockSpec((1, dv), lambda bb, wi, ri, j: (0, 0)),
                  pl.BlockSpec((None, tr // SUBLANES, SUBLANES, SUBLANES, tn), lambda bb, wi, ri, j: (bb, ri, wi, 0, j)),
                  pl.BlockSpec((None, SUBLANES, tr, tn), lambda bb, wi, ri, j: (bb, wi, ri, j)),
                  pl.BlockSpec((None, SUBLANES, tr, tn), lambda bb, wi, ri, j: (bb, wi, ri, n // tn + j)),
                  pl.BlockSpec((nv * dv, tn), lambda bb, wi, ri, j: (0, j))],
        out_specs=pl.BlockSpec((None, SUBLANES, tr, tn), lambda bb, wi, ri, j: (bb, wi, ri, j)),
        out_shape=jax.ShapeDtypeStruct((b, GRID_W, rows, n), BF16),
        scratch_shapes=[pltpu.VMEM((SUBLANES * tr, nv * dv), BF16)],
        compiler_params=_cparams(("parallel", "parallel", "parallel", "arbitrary")), name="bout",
    )(of5, ob5, big4, big4, nw, ya5, mg4, mg4, w)
    return out


def _ffn_up_kernel(m_ref, x_ref, perm_ref, gt1_ref, wo_ref, lnw_ref, sh_ref, sc_ref, wg_ref, wu_ref,
                   x1_ref, a_ref, h_scr):
    ncol = m_ref.shape[0]
    tm, d = h_scr.shape
    ch = min(tm, 256)

    @pl.when(pl.program_id(3) == 0)
    def _():
        mr = jnp.dot(perm_ref[...], m_ref[...].reshape(tm, d), preferred_element_type=F32).astype(BF16)
        x1_ref[...] = jnp.dot(mr, wo_ref[...], preferred_element_type=F32)
        for r0 in range(0, tm, ch):
            xb = x_ref[r0 // ncol:(r0 + ch) // ncol].reshape(ch, d)
            x1 = xb + gt1_ref[...] * x1_ref[r0:r0 + ch, :]
            x1_ref[r0:r0 + ch, :] = x1
            h_scr[r0:r0 + ch, :] = _rms_mod(x1, lnw_ref[...], 1.0 + sc_ref[...], sh_ref[...]).astype(h_scr.dtype)

    h = h_scr[...]
    g = jnp.dot(h, wg_ref[...], preferred_element_type=F32)
    u = jnp.dot(h, wu_ref[...], preferred_element_type=F32)
    a_ref[...] = (g * _sigmoid(g) * u).astype(a_ref.dtype)


def _ffn_down_kernel(a_ref, x1_ref, gt2_ref, wd_ref, fw_ref, o_ref):
    ncol = o_ref.shape[1]
    tm, d = x1_ref.shape
    x2 = x1_ref[...] + gt2_ref[...] * jnp.dot(a_ref[...], wd_ref[...], preferred_element_type=F32)
    ms = jnp.mean(x2 * x2, axis=-1, keepdims=True)
    o_ref[...] = (x2 * lax.rsqrt(ms + EPS) * fw_ref[...]).reshape(tm // ncol, ncol, d)


def _ffn(m4, x, gt1, w_o, lnw, sh, sc, gt2, w_in, w_dn, fw, *, tf):
    b, s, d = x.shape
    rows = s // GRID_W
    tr = min(TILE_R, rows)
    tm = SUBLANES * tr
    fh = w_dn.shape[0]
    nf = fh // tf
    ncb, nrb = GRID_W // SUBLANES, rows // tr
    x4 = x.reshape(b, rows, GRID_W, d)
    vec = pl.BlockSpec((None, 1, d), lambda bb, wi, ri, f: (bb, 0, 0))
    par = pl.BlockSpec((1, d), lambda bb, wi, ri, f: (0, 0))
    once = pl.Buffered(1)
    x1, a = pl.pallas_call(
        _ffn_up_kernel, grid=(b, ncb, nrb, nf),
        in_specs=[pl.BlockSpec((None, SUBLANES, tr, d), lambda bb, wi, ri, f: (bb, wi, ri, 0)),
                  pl.BlockSpec((None, tr, SUBLANES, d), lambda bb, wi, ri, f: (bb, ri, wi, 0)),
                  pl.BlockSpec((tm, tm), lambda bb, wi, ri, f: (0, 0)),
                  vec,
                  pl.BlockSpec((d, d), lambda bb, wi, ri, f: (0, 0), pipeline_mode=once),
                  par, vec, vec,
                  pl.BlockSpec((d, tf), lambda bb, wi, ri, f: (0, f)),
                  pl.BlockSpec((d, tf), lambda bb, wi, ri, f: (0, nf + f))],
        out_specs=[pl.BlockSpec((None, None, None, tm, d), lambda bb, wi, ri, f: (bb, wi, ri, 0, 0)),
                   pl.BlockSpec((None, None, None, tm, tf), lambda bb, wi, ri, f: (bb, wi, ri, 0, f))],
        out_shape=[jax.ShapeDtypeStruct((b, ncb, nrb, tm, d), F32), jax.ShapeDtypeStruct((b, ncb, nrb, tm, fh), BF16)],
        scratch_shapes=[pltpu.VMEM((tm, d), BF16)],
        compiler_params=_cparams(("parallel", "parallel", "parallel", "arbitrary")), name="ffn_up",
    )(m4, x4, _strip_perm(tm), gt1, w_o, lnw, sh, sc, w_in, w_in)
    vec3 = pl.BlockSpec((None, 1, d), lambda bb, wi, ri: (bb, 0, 0))
    out = pl.pallas_call(
        _ffn_down_kernel, grid=(b, ncb, nrb),
        in_specs=[pl.BlockSpec((None, None, None, tm, fh), lambda bb, wi, ri: (bb, wi, ri, 0, 0)),
                  pl.BlockSpec((None, None, None, tm, d), lambda bb, wi, ri: (bb, wi, ri, 0, 0)),
                  vec3,
                  pl.BlockSpec((fh, d), lambda bb, wi, ri: (0, 0), pipeline_mode=once),
                  pl.BlockSpec((1, d), lambda bb, wi, ri: (0, 0))],
        out_specs=pl.BlockSpec((None, tr, SUBLANES, d), lambda bb, wi, ri: (bb, ri, wi, 0)),
        out_shape=jax.ShapeDtypeStruct((b, rows, GRID_W, d), F32),
        compiler_params=_cparams(("parallel", "parallel", "parallel")), name="ffn_down",
    )(a, x1, gt2, w_dn, fw)
    return out.reshape(b, s, d)


def _pad_pairs(t, axis):
    n = t.shape[axis]
    bw = n // NH_A
    shp = t.shape[:axis] + (NH_A // 2, 2 * bw) + t.shape[axis + 1:]
    t = t.reshape(shp)
    padw = [(0, 0)] * t.ndim
    padw[axis + 1] = (0, PAIR_W - 2 * bw)
    t = jnp.pad(t, padw)
    return t.reshape(t.shape[:axis] + (NH_A // 2 * PAIR_W,) + t.shape[axis + 2:])


def _pair_gate_weights(gw, gb):
    bw = gw.shape[1]
    npair = NH_A // 2
    g = gw.reshape(npair, 2, bw, 2, bw)
    blocks = []
    for t in range(2):
        top = jnp.pad(g[:, 0, :, t, :], ((0, 0), (0, 0), (0, PAIR_W - bw)))
        bot = jnp.pad(g[:, 1, :, t, :], ((0, 0), (0, PAIR_W - 2 * bw), (bw, PAIR_W - 2 * bw)))
        blocks.append(jnp.concatenate([top, bot], axis=1))
    wp = jnp.concatenate(blocks, axis=2)
    b = gb.reshape(npair, 2, 2, bw).transpose(0, 2, 1, 3).reshape(npair, 2, 2 * bw)
    b = jnp.pad(b, ((0, 0), (0, 0), (0, PAIR_W - 2 * bw)))
    return wp.astype(BF16), b.reshape(npair, 1, 2 * PAIR_W)


def kernel(x, c, ctx, c_ctx, w_mod, b_mod, ln1_w, ln2_w, w_in, conv_a_w, conv_a_b, lru_gate_w, lru_gate_b, lru_lambda, conv_qkv_w, gdn_a_log, gdn_dt_bias, gdn_norm_w, w_a_out, w_b_out, w_out, w_ffn_in, w_ffn_out, final_norm_w):
    depth = w_mod.shape[0]
    assert depth == 1, "context stream updates are only needed for depth > 1"
    bsz, seq, d = x.shape
    lctx = ctx.shape[1]
    rows = seq // GRID_W
    assert seq % STRIP == 0 and rows % CHUNK == 0 and lctx % CHUNK == 0
    wa = lru_lambda.shape[-1]
    wap = NH_A // 2 * PAIR_W
    qk_dim, v_dim = NK * DK, NV * DV
    qkv_dim = 2 * qk_dim + v_dim
    l = 0
    tn = 512

    wi = w_in[l]
    o_qkv, o_z, o_g, o_mg = 2 * wa, 2 * wa + qkv_dim, 2 * wa + qkv_dim + v_dim, 2 * wa + qkv_dim + v_dim + 4 * NV
    w_axp = _pad_pairs(wi[:, :wa], 1).astype(BF16)
    w_agp = _pad_pairs(wi[:, wa:2 * wa], 1).astype(BF16)
    w_qkv = wi[:, o_qkv:o_z].astype(BF16)
    w_z = wi[:, o_z:o_g].astype(BF16)
    w_gt = wi[:, o_g:o_mg].astype(BF16)
    w_mg = wi[:, o_mg:].astype(BF16)
    cw_a = _pad_pairs(conv_a_w[l], 1)
    cb_a = _pad_pairs(conv_a_b[l][None], 1)
    nla = _pad_pairs(-LRU_C * jax.nn.softplus(-lru_lambda[l]), 1)
    gate_w = [_pair_gate_weights(lru_gate_w[l, dd], lru_gate_b[l, dd]) for dd in range(2)]
    neg_a = -jnp.exp(gdn_a_log[l]).reshape(1, 2 * NV)
    na_l = jnp.concatenate([jnp.zeros((1, 2 * NV), F32), neg_a], axis=1)
    dt_l = jnp.concatenate([jnp.zeros((1, 2 * NV), F32), gdn_dt_bias[l].reshape(1, 2 * NV)], axis=1)
    w_a = _pad_pairs(w_a_out[l], 0).astype(BF16)
    w_b = w_b_out[l].astype(BF16)
    w_o = w_out[l].astype(BF16)
    w_f1 = w_ffn_in[l].astype(BF16)
    w_f2 = w_ffn_out[l].astype(BF16)

    nrow = -(-(bsz + 1) // SUBLANES) * SUBLANES
    cc = jnp.concatenate([c, c_ctx[None], jnp.zeros((nrow - bsz - 1, d), F32)], axis=0)
    mod = _mod(cc, w_mod[l], b_mod[l][None]).reshape(nrow, N_MOD, 1, d)
    sh1, sc1, gt1, sh2, sc2, gt2 = (mod[:bsz, k] for k in range(N_MOD))
    csh1 = jnp.broadcast_to(mod[bsz, 0], (bsz, 1, d))
    csc1 = jnp.broadcast_to(mod[bsz, 1], (bsz, 1, d))
    ln1 = ln1_w[l][None]

    nax = wap // tn
    silu = lambda y: y * _sigmoid(y)
    ax, ag = _inproj(x, ln1, sh1, sc1, [(w_axp, tn, nax, BF16), (w_agp, tn, nax, BF16, _gelu_tanh)],
                     order="strip", tm=min(2 * STRIP, seq), tt=STRIP, name="inproj_a")
    big, zs, mg, gts = _inproj(x, ln1, sh1, sc1,
                               [(w_qkv, 2 * tn, qkv_dim // (2 * tn), BF16), (w_z, 2 * tn, v_dim // (2 * tn), BF16, silu),
                                (w_mg, tn, 2 * d // tn, BF16, _sigmoid), (w_gt, LANES, 1, F32)],
                               order="colmajor", name="inproj_b")
    (cax,) = _inproj(ctx, ln1, csh1, csc1, [(w_axp, tn, nax, BF16)], order="strip", tm=lctx, tt=lctx, name="inproj_ca")
    cbig, cgts = _inproj(ctx, ln1, csh1, csc1, [(w_qkv, tn, qkv_dim // tn, BF16), (w_gt, LANES, 1, F32)],
                         order="raster", tm=lctx, name="inproj_cb")

    ys = []
    for dd in range(2):
        wg, bg = gate_w[dd]
        h0 = jnp.zeros((bsz, 1, wap), F32)
        _, hc = _lru(cax, cw_a, cb_a, wg, bg, nla[dd:dd + 1], h0, reverse=bool(dd), tt=lctx, name=f"lru_c{dd}")
        y, _ = _lru(ax, cw_a, cb_a, wg, bg, nla[dd:dd + 1], hc, reverse=bool(dd), tt=STRIP, name=f"lru_x{dd}")
        ys.append(y)

    cw_q = conv_qkv_w[l]

    def prep_all(q_arr, g_arr, tt, tag):
        qh = _prep(q_arr, cw_q, mode="q", col0=0, nheads=NK, tt=tt, name="prep_q" + tag)
        kh = _prep(q_arr, cw_q, mode="k", col0=qk_dim // 512, nheads=NK, tt=tt, name="prep_k" + tag)
        vh = _prep(q_arr, cw_q, mode="v", col0=2 * qk_dim // 512, nheads=NV, tt=tt, name="prep_v" + tag)
        gc, gr = _gates(g_arr, na_l, dt_l, tt=tt)
        s_len = q_arr.shape[1]
        gr = gr.reshape(bsz, 4, NK, 2, s_len).transpose(0, 1, 3, 2, 4)
        return qh, kh, vh, gc, gr.reshape(bsz, LANES, s_len // CHUNK, CHUNK)

    pc = prep_all(cbig, cgts, lctx, "_c")
    px = prep_all(big, gts, min(512, seq), "_x")
    os_ = []
    for dd in range(2):
        s0 = jnp.zeros((bsz, NV, DK, DV), F32)
        (sc_state,) = _gdn(*pc, s0, reverse=bool(dd), tb=lctx, write_out=False, name=f"gdn_c{dd}")
        o, _ = _gdn(*px, sc_state, reverse=bool(dd), tb=min(512, seq), write_out=True, name=f"gdn_x{dd}")
        os_.append(o)

    ya = _aout(ys[0], ys[1], ag, w_a, tm=min(512, seq), tn=tn)
    m4 = _bout(os_[0], os_[1], zs, mg, gdn_norm_w[l][None], ya, w_b, z_col=0, tn=tn)
    return _ffn(m4, x, gt1, w_o, ln2_w[l][None], sh2, sc2, gt2, w_f1, w_f2, final_norm_w[None], tf=512)
```

```python
import functools

import jax
import jax.numpy as jnp
import numpy as np
from jax import lax
from jax.experimental import pallas as pl
from jax.experimental.pallas import tpu as pltpu

F32 = jnp.float32
BF16 = jnp.bfloat16
HIGHEST = lax.Precision.HIGHEST

EPS = 1e-6
GRID_W = 64
CONV_W = 4
N_MOD = 6
NH_A = 16
LRU_C = 8.0
NK, DK, NV, DV = 16, 128, 32, 128
CHUNK = 64
LANES = 128
SUBLANES = 8
HALO = 16
PAIR_W = 384
STRIP = SUBLANES * GRID_W
TILE_R = 64
VMEM_LIMIT = 63 * 1024 * 1024


def _cparams(sem):
    return pltpu.CompilerParams(dimension_semantics=sem, vmem_limit_bytes=VMEM_LIMIT)


def _nt_dot(a, b, **kw):
    return lax.dot_general(a, b, (((1,), (1,)), ((), ())), preferred_element_type=F32, **kw)


def _tn_dot(a, b):
    return lax.dot_general(a, b, (((0,), (0,)), ((), ())), preferred_element_type=F32)


def _bdot(a, b):
    return jnp.dot(a.astype(BF16), b.astype(BF16), preferred_element_type=F32)


def _sigmoid(x):
    return 0.5 * jnp.tanh(0.5 * x) + 0.5


def _mod_kernel(c_ref, w_ref, b_ref, o_ref):
    s = c_ref[...]
    s = s * _sigmoid(s)
    o_ref[...] = jnp.dot(s, w_ref[...], precision=HIGHEST, preferred_element_type=F32) + b_ref[...]


def _mod(cc, w, b, tn=1024):
    m, d = cc.shape
    n = w.shape[1]
    return pl.pallas_call(
        _mod_kernel, grid=(n // tn,),
        in_specs=[pl.BlockSpec((m, d), lambda j: (0, 0)),
                  pl.BlockSpec((d, tn), lambda j: (0, j)),
                  pl.BlockSpec((1, tn), lambda j: (0, j))],
        out_specs=pl.BlockSpec((m, tn), lambda j: (0, j)),
        out_shape=jax.ShapeDtypeStruct((m, n), F32),
        compiler_params=_cparams(("parallel",)), name="mod")(cc, w, b)


def _rms_mod(xb, lnw, sc1, sh):
    xb = xb.astype(F32)
    ms = jnp.mean(xb * xb, axis=-1, keepdims=True)
    return (xb * lax.rsqrt(ms + EPS) * lnw) * sc1 + sh


def _strip_perm(tt):
    seg = tt // SUBLANES
    i = np.arange(tt)
    p = np.zeros((tt, tt), np.float32)
    p[i, (i % SUBLANES) * seg + i // SUBLANES] = 1.0
    return jnp.asarray(p, BF16)


def _colmajor_perm(rows):
    n = SUBLANES * rows
    i = np.arange(n)
    p = np.zeros((n, n), np.float32)
    p[i, (i % rows) * SUBLANES + i // rows] = 1.0
    return jnp.asarray(p, BF16)


def _inproj_kernel(*refs, order, tiles, acts, tt):
    nseg = len(tiles)
    x_ref, lnw_ref, sh_ref, sc_ref = refs[:4]
    pos = 4 if order == "raster" else 5
    w_refs = refs[pos:pos + nseg]
    o_refs = refs[pos + nseg:pos + 2 * nseg]
    scr = refs[pos + 2 * nseg:]
    h_scr = scr[0]
    tm, d = h_scr.shape
    j = pl.program_id(2)

    @pl.when(j == 0)
    def _():
        lnw = lnw_ref[...]
        sc1 = 1.0 + sc_ref[...]
        sh = sh_ref[...]
        dst = h_scr if order == "raster" else scr[1]
        ch = min(tm, 256)
        for r0 in range(0, tm, ch):
            if order == "colmajor":
                xb = x_ref[r0 // SUBLANES:(r0 + ch) // SUBLANES].reshape(ch, d)
            else:
                xb = x_ref[r0:r0 + ch, :]
            dst[r0:r0 + ch, :] = _rms_mod(xb, lnw, sc1, sh).astype(dst.dtype)
        if order != "raster":
            perm_ref = refs[4]
            for r0 in range(0, tm, tt):
                for q0 in range(0, tt, ch):
                    h_scr[r0 + q0:r0 + q0 + ch, :] = jnp.dot(
                        perm_ref[q0:q0 + ch, :], dst[r0:r0 + tt, :], preferred_element_type=F32).astype(h_scr.dtype)

    lo = 0
    for k in range(nseg):
        @pl.when((j >= lo) & (j < lo + tiles[k]))
        def _(k=k):
            y = jnp.dot(h_scr[...], w_refs[k][...], preferred_element_type=F32)
            if acts[k] is not None:
                y = acts[k](y)
            o_refs[k][...] = y.astype(o_refs[k].dtype)
        lo += tiles[k]


def _inproj(x, lnw, sh, sc, segs, *, order, tm=None, tt=None, name="inproj"):
    b, s, d = x.shape
    ins = [None, lnw, sh, sc]
    if order == "colmajor":
        rows = s // GRID_W
        tm = SUBLANES * rows
        ins[0] = x.reshape(b, rows, GRID_W, d)
        x_spec = pl.BlockSpec((None, rows, SUBLANES, d), lambda bb, i, j: (bb, 0, i, 0))
    else:
        ins[0] = x
        x_spec = pl.BlockSpec((None, tm, d), lambda bb, i, j: (bb, i, 0))
    in_specs = [x_spec,
                pl.BlockSpec((1, d), lambda bb, i, j: (0, 0)),
                pl.BlockSpec((None, 1, d), lambda bb, i, j: (bb, 0, 0)),
                pl.BlockSpec((None, 1, d), lambda bb, i, j: (bb, 0, 0))]
    scratch = [pltpu.VMEM((tm, d), BF16)]
    if order != "raster":
        if order == "colmajor":
            tt = tm
        ins.append(_strip_perm(tt) if order == "strip" else _colmajor_perm(s // GRID_W))
        in_specs.append(pl.BlockSpec((tt, tt), lambda bb, i, j: (0, 0), pipeline_mode=pl.Buffered(1)))
        scratch.append(pltpu.VMEM((tm, d), BF16))
    tiles = tuple(sg[2] for sg in segs)
    acts = tuple(sg[4] if len(sg) > 4 else None for sg in segs)
    out_specs, out_shape = [], []
    lo = 0
    for w, tn, nt, dt in (sg[:4] for sg in segs):
        clip = functools.partial(lambda j, lo, nt: jnp.clip(j - lo, 0, nt - 1), lo=lo, nt=nt)
        ins.append(w)
        in_specs.append(pl.BlockSpec((d, tn), lambda bb, i, j, clip=clip: (0, clip(j))))
        out_specs.append(pl.BlockSpec((None, tm, tn), lambda bb, i, j, clip=clip: (bb, i, clip(j))))
        out_shape.append(jax.ShapeDtypeStruct((b, s, tn * nt), dt))
        lo += nt
    return pl.pallas_call(
        functools.partial(_inproj_kernel, order=order, tiles=tiles, acts=acts, tt=tt),
        grid=(b, s // tm, lo), in_specs=in_specs, out_specs=out_specs, out_shape=out_shape,
        scratch_shapes=scratch,
        compiler_params=_cparams(("parallel", "parallel", "arbitrary")), name=name)(*ins)


def _lru_kernel(xm_ref, xp_ref, xn_ref, cw_ref, cb_ref, wg_ref, bg_ref, nla_ref, h0_ref,
                y_ref, ht_ref, pad, a_scr, b_scr, carry, *, nt, tt, reverse):
    i = pl.program_id(2)
    tb = nt - 1 - i if reverse else i
    seg = tt // SUBLANES
    gm = min(tt, 256)

    @pl.when(i == 0)
    def _():
        carry[...] = h0_ref[...]

    sub = lax.broadcasted_iota(jnp.int32, (SUBLANES, PAIR_W), 0)
    first, last = tb == 0, tb == nt - 1
    prev_last = jnp.where(first, 0.0, xp_ref[SUBLANES:2 * SUBLANES, :].astype(F32))
    nxt0 = jnp.where(last, 0.0, xn_ref[0:SUBLANES, :].astype(F32))
    nxt1 = jnp.where(last, 0.0, xn_ref[SUBLANES:2 * SUBLANES, :].astype(F32))
    x_end = xm_ref[tt - SUBLANES:tt, :].astype(F32)
    x_0 = xm_ref[0:SUBLANES, :].astype(F32)
    x_1 = xm_ref[SUBLANES:2 * SUBLANES, :].astype(F32)
    up = SUBLANES - 1
    pad[0:SUBLANES, :] = jnp.where(sub == 0, pltpu.roll(prev_last, 1, 0), pltpu.roll(x_end, 1, 0))
    pad[SUBLANES:SUBLANES + tt, :] = xm_ref[...].astype(F32)
    pad[SUBLANES + tt:2 * SUBLANES + tt, :] = jnp.where(sub == up, pltpu.roll(nxt0, up, 0), pltpu.roll(x_0, up, 0))
    pad[2 * SUBLANES + tt:3 * SUBLANES + tt, :] = jnp.where(sub == up, pltpu.roll(nxt1, up, 0), pltpu.roll(x_1, up, 0))

    for r0 in range(0, tt, gm):
        xc = cb_ref[...] + cw_ref[0:1, :] * pad[r0:r0 + gm, :]
        for k in range(1, CONV_W):
            xc = xc + cw_ref[k:k + 1, :] * pad[SUBLANES * k + r0:SUBLANES * k + r0 + gm, :]
        gates = _bdot(xc, wg_ref[...]) + bg_ref[...]
        r = _sigmoid(gates[:, :PAIR_W])
        ig = _sigmoid(gates[:, PAIR_W:])
        la = nla_ref[...] * r
        a = jnp.exp(la)
        a_scr[r0:r0 + gm, :] = a
        b_scr[r0:r0 + gm, :] = jnp.sqrt(-jnp.tanh(la) * (a * a + 1.0)) * (ig * xc)

    def scan_body(jj, hp):
        j = seg - 1 - jj if reverse else jj
        rows = pl.ds(pl.multiple_of(j * SUBLANES, SUBLANES), SUBLANES)
        a = a_scr[rows, :]
        h = a * hp[0] + b_scr[rows, :]
        p = a * hp[1]
        b_scr[rows, :] = h
        a_scr[rows, :] = p
        return h, p

    h, p = lax.fori_loop(0, seg, scan_body,
                         (jnp.zeros((SUBLANES, PAIR_W), F32), jnp.ones((SUBLANES, PAIR_W), F32)), unroll=8)

    cin = carry[...]
    cvec = jnp.zeros((SUBLANES, PAIR_W), F32)
    for s in (range(SUBLANES - 1, -1, -1) if reverse else range(SUBLANES)):
        cvec = jnp.where(sub == s, jnp.broadcast_to(cin, (SUBLANES, PAIR_W)), cvec)
        cin = p[s:s + 1, :] * cin + h[s:s + 1, :]
    carry[...] = cin
    cvec2 = jnp.concatenate([cvec, cvec], axis=0)

    def fix_body(j, _):
        rows = pl.ds(pl.multiple_of(j * 2 * SUBLANES, 2 * SUBLANES), 2 * SUBLANES)
        y_ref[rows, :] = (b_scr[rows, :] + a_scr[rows, :] * cvec2).astype(y_ref.dtype)
        return 0

    lax.fori_loop(0, seg // 2, fix_body, 0, unroll=4)

    @pl.when(i == nt - 1)
    def _():
        ht_ref[...] = carry[...]


def _lru(ax, cw, cb, wg, bg, nla, h0, *, reverse, tt, name):
    b, s, _ = ax.shape
    npair = cw.shape[1] // PAIR_W
    nt = s // tt
    nh = tt // HALO
    last_blk = s // HALO - 1
    tb_of = (lambda i: nt - 1 - i) if reverse else (lambda i: i)
    par = lambda bb, p, i: (0, p)
    return pl.pallas_call(
        functools.partial(_lru_kernel, nt=nt, tt=tt, reverse=reverse),
        grid=(b, npair, nt),
        in_specs=[
            pl.BlockSpec((None, tt, PAIR_W), lambda bb, p, i: (bb, tb_of(i), p)),
            pl.BlockSpec((None, HALO, PAIR_W), lambda bb, p, i: (bb, jnp.maximum(tb_of(i) * nh - 1, 0), p)),
            pl.BlockSpec((None, HALO, PAIR_W), lambda bb, p, i: (bb, jnp.minimum((tb_of(i) + 1) * nh, last_blk), p)),
            pl.BlockSpec((CONV_W, PAIR_W), par),
            pl.BlockSpec((1, PAIR_W), par),
            pl.BlockSpec((None, PAIR_W, 2 * PAIR_W), lambda bb, p, i: (p, 0, 0)),
            pl.BlockSpec((None, 1, 2 * PAIR_W), lambda bb, p, i: (p, 0, 0)),
            pl.BlockSpec((1, PAIR_W), par),
            pl.BlockSpec((None, 1, PAIR_W), lambda bb, p, i: (bb, 0, p)),
        ],
        out_specs=[pl.BlockSpec((None, tt, PAIR_W), lambda bb, p, i: (bb, tb_of(i), p)),
                   pl.BlockSpec((None, 1, PAIR_W), lambda bb, p, i: (bb, 0, p))],
        out_shape=[jax.ShapeDtypeStruct((b, s, npair * PAIR_W), BF16),
                   jax.ShapeDtypeStruct((b, 1, npair * PAIR_W), F32)],
        scratch_shapes=[pltpu.VMEM((tt + 3 * SUBLANES, PAIR_W), F32),
                        pltpu.VMEM((tt, PAIR_W), F32),
                        pltpu.VMEM((tt, PAIR_W), F32),
                        pltpu.VMEM((1, PAIR_W), F32)],
        compiler_params=_cparams(("parallel", "parallel", "arbitrary")), name=name,
    )(ax, ax, ax, cw, cb, wg, bg, nla, h0)


def _prep_kernel(xm_ref, xp_ref, xn_ref, cw_ref, o_ref, pad, *, nt, tt, mode):
    i = pl.program_id(1)
    pad[0:HALO, :] = jnp.where(i == 0, 0.0, xp_ref[...].astype(F32))
    pad[HALO:HALO + tt, :] = xm_ref[...].astype(F32)
    pad[HALO + tt:2 * HALO + tt, :] = jnp.where(i == nt - 1, 0.0, xn_ref[...].astype(F32))
    y = cw_ref[0:1, :] * pad[HALO - 1:HALO - 1 + tt, :]
    for k in range(1, CONV_W):
        y = y + cw_ref[k:k + 1, :] * pad[HALO - 1 + k:HALO - 1 + k + tt, :]
    y = y * _sigmoid(y)
    for h in range(o_ref.shape[0]):
        yh = y[:, h * LANES:(h + 1) * LANES]
        if mode != "v":
            yh = yh * lax.rsqrt(jnp.sum(yh * yh, axis=-1, keepdims=True) + EPS)
        if mode == "q":
            yh = yh * (DK ** -0.5)
        o_ref[h] = yh.astype(o_ref.dtype)


def _prep(q_arr, cw, *, mode, col0, nheads, tt, name):
    b, s, _ = q_arr.shape
    nh = 4
    width = nh * LANES
    nt = s // tt
    nhalo = tt // HALO
    last_blk = s // HALO - 1
    return pl.pallas_call(
        functools.partial(_prep_kernel, nt=nt, tt=tt, mode=mode),
        grid=(b, nt, nheads // nh),
        in_specs=[pl.BlockSpec((None, tt, width), lambda bb, i, j: (bb, i, col0 + j)),
                  pl.BlockSpec((None, HALO, width), lambda bb, i, j: (bb, jnp.maximum(i * nhalo - 1, 0), col0 + j)),
                  pl.BlockSpec((None, HALO, width), lambda bb, i, j: (bb, jnp.minimum((i + 1) * nhalo, last_blk), col0 + j)),
                  pl.BlockSpec((CONV_W, width), lambda bb, i, j: (0, col0 + j))],
        out_specs=pl.BlockSpec((None, nh, tt, LANES), lambda bb, i, j: (bb, j, i, 0)),
        out_shape=jax.ShapeDtypeStruct((b, nheads, s, LANES), BF16),
        scratch_shapes=[pltpu.VMEM((tt + 2 * HALO, width), F32)],
        compiler_params=_cparams(("parallel", "parallel", "parallel")), name=name,
    )(q_arr, q_arr, q_arr, cw)


def _gates_kernel(x_ref, na_ref, dt_ref, oc_ref, or_ref):
    x = x_ref[...].astype(F32)
    lane = lax.broadcasted_iota(jnp.int32, x.shape, 1)
    z = x + dt_ref[...]
    sp = jnp.maximum(z, 0.0) + jnp.log1p(jnp.exp(-jnp.abs(z)))
    val = jnp.where(lane < 2 * NV, _sigmoid(x), na_ref[...] * sp)
    oc_ref[...] = val
    or_ref[...] = val.T


def _gates(g_arr, na, dtb, *, tt):
    b, s, _ = g_arr.shape
    return pl.pallas_call(
        _gates_kernel, grid=(b, s // tt),
        in_specs=[pl.BlockSpec((None, tt, LANES), lambda bb, i: (bb, i, 0)),
                  pl.BlockSpec((1, LANES), lambda bb, i: (0, 0)),
                  pl.BlockSpec((1, LANES), lambda bb, i: (0, 0))],
        out_specs=[pl.BlockSpec((None, tt, LANES), lambda bb, i: (bb, i, 0)),
                   pl.BlockSpec((None, LANES, tt), lambda bb, i: (bb, 0, i))],
        out_shape=[jax.ShapeDtypeStruct((b, s, LANES), F32), jax.ShapeDtypeStruct((b, LANES, s), F32)],
        compiler_params=_cparams(("parallel", "parallel")), name="gdn_gates")(g_arr, na, dtb)


def _gdn_kernel(q_ref, k_ref, v_ref, gc_ref, gr_ref, s0_ref, *rest, nt, tb, reverse, write_out, wave):
    if write_out:
        o_ref, st_ref, st, sol_scr, cs_scr, a_scr = rest
    else:
        st_ref, st, sol_scr, cs_scr = rest
    i = pl.program_id(1)
    nch = tb // CHUNK
    nkh = q_ref.shape[0]
    lane_b = NV if reverse else 0
    lane_g = 2 * NV + lane_b

    @pl.when(i == 0)
    def _():
        st[...] = s0_ref[...]

    c2 = 2 * CHUNK
    row = lax.broadcasted_iota(jnp.int32, (CHUNK, c2), 0)
    lane = lax.broadcasted_iota(jnp.int32, (CHUNK, c2), 1)
    colm = lane % CHUNK
    left = lane < CHUNK
    if reverse:
        incl, strict, tl = row <= colm, row < colm, 0
    else:
        incl, strict, tl = row >= colm, row > colm, CHUNK - 1
    eye2 = (row == colm).astype(F32)
    r64 = lax.broadcasted_iota(jnp.int32, (CHUNK, CHUNK), 0)
    c64 = lax.broadcasted_iota(jnp.int32, (CHUNK, CHUNK), 1)
    tri = (r64 <= c64 if reverse else r64 >= c64).astype(F32)
    tri_t = (c64 <= r64 if reverse else c64 >= r64).astype(F32)
    z64 = jnp.zeros((CHUNK, CHUNK), F32)
    tri_t_l = jnp.concatenate([tri_t, z64], axis=1)
    tri_t_r = jnp.concatenate([z64, tri_t], axis=1)
    r128 = lax.broadcasted_iota(jnp.int32, (c2, c2), 0)
    l128 = lax.broadcasted_iota(jnp.int32, (c2, c2), 1)
    diag_blocks = (r128 < CHUNK) == (l128 < CHUNK)

    def split_stack(x2):
        return jnp.where(diag_blocks, jnp.concatenate([x2, x2], axis=0), 0.0).astype(BF16)

    def chunk_rows(c):
        return pl.ds(pl.multiple_of(c * CHUNK, CHUNK), CHUNK)

    def head(c, slot):
        rows = chunk_rows(c)
        gcol = gc_ref[rows, :]
        cs_col = jnp.dot(tri, gcol, precision=HIGHEST, preferred_element_type=F32)
        cs_scr[slot] = cs_col
        grow = gr_ref[:, pl.ds(c, 1), :].reshape(2 * nkh, CHUNK)
        gc_row2 = (jnp.dot(grow[:nkh], tri_t_l, precision=HIGHEST, preferred_element_type=F32)
                   + jnp.dot(grow[nkh:], tri_t_r, precision=HIGHEST, preferred_element_type=F32))

        for p0 in range(0, nkh, wave):
            prs = range(p0, min(p0 + wave, nkh))
            ks, pp, tt_, cols = {}, {}, {}, {}
            for p in prs:
                k = k_ref[p, rows, :]
                q = q_ref[p, rows, :]
                kq = _nt_dot(jnp.concatenate([k, q], axis=0), jnp.concatenate([k, k], axis=0))
                ha, hb = 2 * p, 2 * p + 1
                gca, gcb = cs_col[:, lane_g + ha:lane_g + ha + 1], cs_col[:, lane_g + hb:lane_g + hb + 1]
                bca, bcb = gcol[:, lane_b + ha:lane_b + ha + 1], gcol[:, lane_b + hb:lane_b + hb + 1]
                gcc2 = jnp.where(left, gca, gcb)
                bc2 = jnp.where(left, bca, bcb)
                decay2 = jnp.where(incl, jnp.exp(jnp.where(incl, gcc2 - gc_row2[p:p + 1, :], 0.0)), 0.0)
                pn = jnp.where(strict, -(kq[:CHUNK] * bc2) * decay2, 0.0)
                ks[p] = k.astype(F32)
                pp[p] = pn
                tt_[p] = eye2 + pn
                if write_out:
                    a_scr[slot, p] = split_stack(kq[CHUNK:] * decay2)
                cols[p] = ((gca, bca), (gcb, bcb))
            yield

            for p in prs:
                pp[p] = jnp.dot(pp[p].astype(BF16), split_stack(pp[p]), preferred_element_type=F32)
            yield
            for _ in range(4):
                for p in prs:
                    y = jnp.dot(jnp.concatenate([pp[p], tt_[p]], axis=0).astype(BF16), split_stack(pp[p]),
                                preferred_element_type=F32)
                    pp[p] = y[:CHUNK]
                    tt_[p] = tt_[p] + y[CHUNK:]
                yield
            for p in prs:
                tt_[p] = tt_[p] + jnp.dot(tt_[p].astype(BF16), split_stack(pp[p]), preferred_element_type=F32)
            yield

            for p in prs:
                rhs = []
                for e in range(2):
                    gch, bch = cols[p][e]
                    vf = v_ref[2 * p + e, rows, :].astype(F32)
                    rhs.append(jnp.concatenate([vf * bch, ks[p] * (bch * jnp.exp(gch))], axis=1))
                sol_scr[slot, p] = jnp.dot(split_stack(tt_[p]), jnp.concatenate(rhs, axis=0).astype(BF16),
                                           preferred_element_type=F32)

    def tail(c, slot):
        rows = chunk_rows(c)
        cs_col = cs_scr[slot]
        wss, vns, gcs = {}, {}, {}
        for p in range(nkh):
            qf = q_ref[p, rows, :].astype(F32)
            for e in range(2):
                m = 2 * p + e
                gcs[m] = cs_col[:, lane_g + m:lane_g + m + 1]
                sol = sol_scr[slot, p, e * CHUNK:(e + 1) * CHUNK, :]
                lhs = jnp.concatenate([sol[:, DV:], qf * jnp.exp(gcs[m])], axis=0)
                wss[m] = _bdot(lhs, st[m])
                vns[m] = sol[:, :DV] - wss[m][:CHUNK]
            if p % 4 == 3:
                yield

        for p in range(nkh):
            kf = k_ref[p, rows, :].astype(F32)
            if write_out:
                oi = jnp.dot(a_scr[slot, p], jnp.concatenate([vns[2 * p], vns[2 * p + 1]], axis=0).astype(BF16),
                             preferred_element_type=F32)
            for e in range(2):
                m = 2 * p + e
                if write_out:
                    o_ref[m, rows, :] = (wss[m][CHUNK:] + oi[e * CHUNK:(e + 1) * CHUNK]).astype(o_ref.dtype)
                gl = gcs[m][tl:tl + 1, :]
                kd = kf * jnp.exp(gl - gcs[m])
                st[m] = st[m] * jnp.exp(gl) + _tn_dot(kd.astype(BF16), vns[m].astype(BF16))
            if p % 4 == 3:
                yield

    def run(*gens):
        live = list(gens)
        while live:
            for g in list(live):
                try:
                    next(g)
                except StopIteration:
                    live.remove(g)

    def body(cc, carry):
        c = nch - 1 - cc if reverse else cc
        run(head(c, 0))
        run(tail(c, 0))
        return carry

    lax.fori_loop(0, nch, body, 0)

    @pl.when(i == nt - 1)
    def _():
        st_ref[...] = st[...]


def _gdn(qh, kh, vh, gc, gr, s0, *, reverse, tb, write_out, name, wave=16):
    b, nk, s, _ = qh.shape
    nv = vh.shape[1]
    nt = s // tb
    nch = tb // CHUNK
    d = 1 if reverse else 0
    tb_of = (lambda i: nt - 1 - i) if reverse else (lambda i: i)
    in_specs = [
        pl.BlockSpec((None, nk, tb, DK), lambda bb, i: (bb, 0, tb_of(i), 0)),
        pl.BlockSpec((None, nk, tb, DK), lambda bb, i: (bb, 0, tb_of(i), 0)),
        pl.BlockSpec((None, nv, tb, DV), lambda bb, i: (bb, 0, tb_of(i), 0)),
        pl.BlockSpec((None, tb, LANES), lambda bb, i: (bb, tb_of(i), 0)),
        pl.BlockSpec((None, nv, nch, CHUNK), lambda bb, i: (bb, 2 + d, tb_of(i), 0)),
        pl.BlockSpec((None, nv, DK, DV), lambda bb, i: (bb, 0, 0, 0)),
    ]
    st_spec = pl.BlockSpec((None, nv, DK, DV), lambda bb, i: (bb, 0, 0, 0))
    st_shape = jax.ShapeDtypeStruct((b, nv, DK, DV), F32)
    if write_out:
        out_specs = [pl.BlockSpec((None, nv, tb, DV), lambda bb, i: (bb, 0, tb_of(i), 0)), st_spec]
        out_shape = [jax.ShapeDtypeStruct((b, nv, s, DV), BF16), st_shape]
    else:
        out_specs, out_shape = [st_spec], [st_shape]
    scratch = [pltpu.VMEM((nv, DK, DV), F32),
               pltpu.VMEM((1, nk, 2 * CHUNK, DV + DK), F32),
               pltpu.VMEM((1, CHUNK, LANES), F32)]
    if write_out:
        scratch.append(pltpu.VMEM((1, nk, 2 * CHUNK, 2 * CHUNK), BF16))
    return pl.pallas_call(
        functools.partial(_gdn_kernel, nt=nt, tb=tb, reverse=reverse, write_out=write_out, wave=wave),
        grid=(b, nt), in_specs=in_specs, out_specs=out_specs, out_shape=out_shape,
        scratch_shapes=scratch,
        compiler_params=_cparams(("parallel", "arbitrary")), name=name,
    )(qh, kh, vh, gc, gr, s0)


def _gelu_tanh(x):
    return 0.5 * x * (1.0 + jnp.tanh(0.7978845608028654 * (x + 0.044715 * (x * x * x))))


def _aout_kernel(yf_ref, yb_ref, ag_ref, w_ref, o_ref, a_scr):
    @pl.when(pl.program_id(2) == 0)
    def _():
        y = yf_ref[...].astype(F32) + yb_ref[...].astype(F32)
        a_scr[...] = (y * ag_ref[...].astype(F32)).astype(a_scr.dtype)

    o_ref[...] = jnp.dot(a_scr[...], w_ref[...], preferred_element_type=F32)


def _aout(yf, yb, ag, w, *, tm, tn):
    b, s, wa = yf.shape
    n = w.shape[1]
    act = pl.BlockSpec((None, tm, wa), lambda bb, i, j: (bb, i, 0))
    return pl.pallas_call(
        _aout_kernel, grid=(b, s // tm, n // tn),
        in_specs=[act, act, act, pl.BlockSpec((wa, tn), lambda bb, i, j: (0, j))],
        out_specs=pl.BlockSpec((None, tm, tn), lambda bb, i, j: (bb, i, j)),
        out_shape=jax.ShapeDtypeStruct((b, s, n), F32),
        scratch_shapes=[pltpu.VMEM((tm, wa), BF16)],
        compiler_params=_cparams(("parallel", "parallel", "arbitrary")), name="aout")(yf, yb, ag, w)


def _bout_kernel(of_ref, ob_ref, zl_ref, zh_ref, nw_ref, ya_ref, ga_ref, gb_ref, w_ref, o_ref, y_scr):
    ncol, tr = o_ref.shape[0], o_ref.shape[1]
    tm = ncol * tr

    @pl.when(pl.program_id(3) == 0)
    def _():
        nw = nw_ref[...]
        nvh = of_ref.shape[0]
        half = nvh // 2
        avg = jnp.full((DV, DV), 1.0 / DV, BF16)
        for h in range(nvh):
            o = (of_ref[h].astype(F32) + ob_ref[h].astype(F32)).reshape(tm, DV)
            z_ref = zl_ref if h < half else zh_ref
            hh = h % half
            z = z_ref[:, :, hh * DV:(hh + 1) * DV].astype(F32).reshape(tm, DV)
            ms = jnp.dot((o * o).astype(BF16), avg, preferred_element_type=F32)
            y = o * lax.rsqrt(ms + EPS)
            y = y * nw * z
            y_scr[:, h * DV:(h + 1) * DV] = y.astype(y_scr.dtype)

    yb = jnp.dot(y_scr[...], w_ref[...], preferred_element_type=F32)
    for k in range(ncol):
        ga = ga_ref[k].astype(F32)
        gb = gb_ref[k].astype(F32)
        ya = ya_ref[:, k].reshape(tr, ya_ref.shape[-1])
        o_ref[k] = (ga * ya + gb * yb[k * tr:(k + 1) * tr]).astype(o_ref.dtype)


def _bout(of, ob, big, mg, nw, ya, w, *, z_col, tn):
    b, nv, s, dv = of.shape
    n = w.shape[1]
    rows = s // GRID_W
    tr = min(TILE_R, rows)
    hw = nv * dv // 2
    of5 = of.reshape(b, nv, GRID_W, rows, dv)
    ob5 = ob.reshape(b, nv, GRID_W, rows, dv)
    big4 = big.reshape(b, GRID_W, rows, big.shape[-1])
    mg4 = mg.reshape(b, GRID_W, rows, mg.shape[-1])
    ya5 = ya.reshape(b, rows // SUBLANES, GRID_W, SUBLANES, n)
    o_spec = pl.BlockSpec((None, nv, SUBLANES, tr, dv), lambda bb, wi, ri, j: (bb, 0, wi, ri, 0))
    out = pl.pallas_call(
        _bout_kernel, grid=(b, GRID_W // SUBLANES, rows // tr, n // tn),
        in_specs=[o_spec, o_spec,
                  pl.BlockSpec((None, SUBLANES, tr, hw), lambda bb, wi, ri, j: (bb, wi, ri, z_col // hw)),
                  pl.BlockSpec((None, SUBLANES, tr, hw), lambda bb, wi, ri, j: (bb, wi, ri, z_col // hw + 1)),
                  pl.BlockSpec((1, dv), lambda bb, wi, ri, j: (0, 0)),
                  pl.BlockSpec((None, tr // SUBLANES, SUBLANES, SUBLANES, tn), lambda bb, wi, ri, j: (bb, ri, wi, 0, j)),
                  pl.BlockSpec((None, SUBLANES, tr, tn), lambda bb, wi, ri, j: (bb, wi, ri, j)),
                  pl.BlockSpec((None, SUBLANES, tr, tn), lambda bb, wi, ri, j: (bb, wi, ri, n // tn + j)),
                  pl.BlockSpec((nv * dv, tn), lambda bb, wi, ri, j: (0, j))],
        out_specs=pl.BlockSpec((None, SUBLANES, tr, tn), lambda bb, wi, ri, j: (bb, wi, ri, j)),
        out_shape=jax.ShapeDtypeStruct((b, GRID_W, rows, n), BF16),
        scratch_shapes=[pltpu.VMEM((SUBLANES * tr, nv * dv), BF16)],
        compiler_params=_cparams(("parallel", "parallel", "parallel", "arbitrary")), name="bout",
    )(of5, ob5, big4, big4, nw, ya5, mg4, mg4, w)
    return out


def _ffn_up_kernel(m_ref, x_ref, perm_ref, gt1_ref, wo_ref, lnw_ref, sh_ref, sc_ref, wg_ref, wu_ref,
                   x1_ref, a_ref, h_scr):
    ncol = m_ref.shape[0]
    tm, d = h_scr.shape
    ch = min(tm, 256)

    @pl.when(pl.program_id(3) == 0)
    def _():
        mr = jnp.dot(perm_ref[...], m_ref[...].reshape(tm, d), preferred_element_type=F32).astype(BF16)
        x1_ref[...] = jnp.dot(mr, wo_ref[...], preferred_element_type=F32)
        for r0 in range(0, tm, ch):
            xb = x_ref[r0 // ncol:(r0 + ch) // ncol].reshape(ch, d)
            x1 = xb + gt1_ref[...] * x1_ref[r0:r0 + ch, :]
            x1_ref[r0:r0 + ch, :] = x1
            h_scr[r0:r0 + ch, :] = _rms_mod(x1, lnw_ref[...], 1.0 + sc_ref[...], sh_ref[...]).astype(h_scr.dtype)

    h = h_scr[...]
    g = jnp.dot(h, wg_ref[...], preferred_element_type=F32)
    u = jnp.dot(h, wu_ref[...], preferred_element_type=F32)
    a_ref[...] = (g * _sigmoid(g) * u).astype(a_ref.dtype)


def _ffn_down_kernel(a_ref, x1_ref, gt2_ref, wd_ref, fw_ref, o_ref):
    ncol = o_ref.shape[1]
    tm, d = x1_ref.shape
    x2 = x1_ref[...] + gt2_ref[...] * jnp.dot(a_ref[...], wd_ref[...], preferred_element_type=F32)
    ms = jnp.mean(x2 * x2, axis=-1, keepdims=True)
    o_ref[...] = (x2 * lax.rsqrt(ms + EPS) * fw_ref[...]).reshape(tm // ncol, ncol, d)


def _ffn(m4, x, gt1, w_o, lnw, sh, sc, gt2, w_in, w_dn, fw, *, tf):
    b, s, d = x.shape
    rows = s // GRID_W
    tr = min(TILE_R, rows)
    tm = SUBLANES * tr
    fh = w_dn.shape[0]
    nf = fh // tf
    ncb, nrb = GRID_W // SUBLANES, rows // tr
    x4 = x.reshape(b, rows, GRID_W, d)
    vec = pl.BlockSpec((None, 1, d), lambda bb, wi, ri, f: (bb, 0, 0))
    par = pl.BlockSpec((1, d), lambda bb, wi, ri, f: (0, 0))
    once = pl.Buffered(1)
    x1, a = pl.pallas_call(
        _ffn_up_kernel, grid=(b, ncb, nrb, nf),
        in_specs=[pl.BlockSpec((None, SUBLANES, tr, d), lambda bb, wi, ri, f: (bb, wi, ri, 0)),
                  pl.BlockSpec((None, tr, SUBLANES, d), lambda bb, wi, ri, f: (bb, ri, wi, 0)),
                  pl.BlockSpec((tm, tm), lambda bb, wi, ri, f: (0, 0)),
                  vec,
                  pl.BlockSpec((d, d), lambda bb, wi, ri, f: (0, 0), pipeline_mode=once),
                  par, vec, vec,
                  pl.BlockSpec((d, tf), lambda bb, wi, ri, f: (0, f)),
                  pl.BlockSpec((d, tf), lambda bb, wi, ri, f: (0, nf + f))],
        out_specs=[pl.BlockSpec((None, None, None, tm, d), lambda bb, wi, ri, f: (bb, wi, ri, 0, 0)),
                   pl.BlockSpec((None, None, None, tm, tf), lambda bb, wi, ri, f: (bb, wi, ri, 0, f))],
        out_shape=[jax.ShapeDtypeStruct((b, ncb, nrb, tm, d), F32), jax.ShapeDtypeStruct((b, ncb, nrb, tm, fh), BF16)],
        scratch_shapes=[pltpu.VMEM((tm, d), BF16)],
        compiler_params=_cparams(("parallel", "parallel", "parallel", "arbitrary")), name="ffn_up",
    )(m4, x4, _strip_perm(tm), gt1, w_o, lnw, sh, sc, w_in, w_in)
    vec3 = pl.BlockSpec((None, 1, d), lambda bb, wi, ri: (bb, 0, 0))
    out = pl.pallas_call(
        _ffn_down_kernel, grid=(b, ncb, nrb),
        in_specs=[pl.BlockSpec((None, None, None, tm, fh), lambda bb, wi, ri: (bb, wi, ri, 0, 0)),
                  pl.BlockSpec((None, None, None, tm, d), lambda bb, wi, ri: (bb, wi, ri, 0, 0)),
                  vec3,
                  pl.BlockSpec((fh, d), lambda bb, wi, ri: (0, 0), pipeline_mode=once),
                  pl.BlockSpec((1, d), lambda bb, wi, ri: (0, 0))],
        out_specs=pl.BlockSpec((None, tr, SUBLANES, d), lambda bb, wi, ri: (bb, ri, wi, 0)),
        out_shape=jax.ShapeDtypeStruct((b, rows, GRID_W, d), F32),
        compiler_params=_cparams(("parallel", "parallel", "parallel")), name="ffn_down",
    )(a, x1, gt2, w_dn, fw)
    return out.reshape(b, s, d)


def _pad_pairs(t, axis):
    n = t.shape[axis]
    bw = n // NH_A
    shp = t.shape[:axis] + (NH_A // 2, 2 * bw) + t.shape[axis + 1:]
    t = t.reshape(shp)
    padw = [(0, 0)] * t.ndim
    padw[axis + 1] = (0, PAIR_W - 2 * bw)
    t = jnp.pad(t, padw)
    return t.reshape(t.shape[:axis] + (NH_A // 2 * PAIR_W,) + t.shape[axis + 2:])


def _pair_gate_weights(gw, gb):
    bw = gw.shape[1]
    npair = NH_A // 2
    g = gw.reshape(npair, 2, bw, 2, bw)
    blocks = []
    for t in range(2):
        top = jnp.pad(g[:, 0, :, t, :], ((0, 0), (0, 0), (0, PAIR_W - bw)))
        bot = jnp.pad(g[:, 1, :, t, :], ((0, 0), (0, PAIR_W - 2 * bw), (bw, PAIR_W - 2 * bw)))
        blocks.append(jnp.concatenate([top, bot], axis=1))
    wp = jnp.concatenate(blocks, axis=2)
    b = gb.reshape(npair, 2, 2, bw).transpose(0, 2, 1, 3).reshape(npair, 2, 2 * bw)
    b = jnp.pad(b, ((0, 0), (0, 0), (0, PAIR_W - 2 * bw)))
    return wp.astype(BF16), b.reshape(npair, 1, 2 * PAIR_W)


def kernel(x, c, ctx, c_ctx, w_mod, b_mod, ln1_w, ln2_w, w_in, conv_a_w, conv_a_b, lru_gate_w, lru_gate_b, lru_lambda, conv_qkv_w, gdn_a_log, gdn_dt_bias, gdn_norm_w, w_a_out, w_b_out, w_out, w_ffn_in, w_ffn_out, final_norm_w):
    depth = w_mod.shape[0]
    assert depth == 1, "context stream updates are only needed for depth > 1"
    bsz, seq, d = x.shape
    lctx = ctx.shape[1]
    rows = seq // GRID_W
    assert seq % STRIP == 0 and rows % CHUNK == 0 and lctx % CHUNK == 0
    wa = lru_lambda.shape[-1]
    wap = NH_A // 2 * PAIR_W
    qk_dim, v_dim = NK * DK, NV * DV
    qkv_dim = 2 * qk_dim + v_dim
    l = 0
    tn = 512

    wi = w_in[l]
    o_qkv, o_z, o_g, o_mg = 2 * wa, 2 * wa + qkv_dim, 2 * wa + qkv_dim + v_dim, 2 * wa + qkv_dim + v_dim + 4 * NV
    w_axp = _pad_pairs(wi[:, :wa], 1).astype(BF16)
    w_agp = _pad_pairs(wi[:, wa:2 * wa], 1).astype(BF16)
    w_qkv = wi[:, o_qkv:o_z].astype(BF16)
    w_z = wi[:, o_z:o_g].astype(BF16)
    w_gt = wi[:, o_g:o_mg].astype(BF16)
    w_mg = wi[:, o_mg:].astype(BF16)
    cw_a = _pad_pairs(conv_a_w[l], 1)
    cb_a = _pad_pairs(conv_a_b[l][None], 1)
    nla = _pad_pairs(-LRU_C * jax.nn.softplus(-lru_lambda[l]), 1)
    gate_w = [_pair_gate_weights(lru_gate_w[l, dd], lru_gate_b[l, dd]) for dd in range(2)]
    neg_a = -jnp.exp(gdn_a_log[l]).reshape(1, 2 * NV)
    na_l = jnp.concatenate([jnp.zeros((1, 2 * NV), F32), neg_a], axis=1)
    dt_l = jnp.concatenate([jnp.zeros((1, 2 * NV), F32), gdn_dt_bias[l].reshape(1, 2 * NV)], axis=1)
    w_a = _pad_pairs(w_a_out[l], 0).astype(BF16)
    w_b = w_b_out[l].astype(BF16)
    w_o = w_out[l].astype(BF16)
    w_f1 = w_ffn_in[l].astype(BF16)
    w_f2 = w_ffn_out[l].astype(BF16)

    nrow = -(-(bsz + 1) // SUBLANES) * SUBLANES
    cc = jnp.concatenate([c, c_ctx[None], jnp.zeros((nrow - bsz - 1, d), F32)], axis=0)
    mod = _mod(cc, w_mod[l], b_mod[l][None]).reshape(nrow, N_MOD, 1, d)
    sh1, sc1, gt1, sh2, sc2, gt2 = (mod[:bsz, k] for k in range(N_MOD))
    csh1 = jnp.broadcast_to(mod[bsz, 0], (bsz, 1, d))
    csc1 = jnp.broadcast_to(mod[bsz, 1], (bsz, 1, d))
    ln1 = ln1_w[l][None]

    nax = wap // tn
    silu = lambda y: y * _sigmoid(y)
    ax, ag = _inproj(x, ln1, sh1, sc1, [(w_axp, 2 * tn, nax // 2, BF16), (w_agp, 2 * tn, nax // 2, BF16, _gelu_tanh)],
                     order="strip", tm=min(2 * STRIP, seq), tt=STRIP, name="inproj_a")
    big, zs, mg, gts = _inproj(x, ln1, sh1, sc1,
                               [(w_qkv, 2 * tn, qkv_dim // (2 * tn), BF16), (w_z, 2 * tn, v_dim // (2 * tn), BF16, silu),
                                (w_mg, tn, 2 * d // tn, BF16, _sigmoid), (w_gt, LANES, 1, F32)],
                               order="colmajor", name="inproj_b")
    (cax,) = _inproj(ctx, ln1, csh1, csc1, [(w_axp, tn, nax, BF16)], order="strip", tm=lctx, tt=lctx, name="inproj_ca")
    cbig, cgts = _inproj(ctx, ln1, csh1, csc1, [(w_qkv, tn, qkv_dim // tn, BF16), (w_gt, LANES, 1, F32)],
                         order="raster", tm=lctx, name="inproj_cb")

    ys = []
    for dd in range(2):
        wg, bg = gate_w[dd]
        h0 = jnp.zeros((bsz, 1, wap), F32)
        _, hc = _lru(cax, cw_a, cb_a, wg, bg, nla[dd:dd + 1], h0, reverse=bool(dd), tt=lctx, name=f"lru_c{dd}")
        y, _ = _lru(ax, cw_a, cb_a, wg, bg, nla[dd:dd + 1], hc, reverse=bool(dd), tt=STRIP, name=f"lru_x{dd}")
        ys.append(y)

    cw_q = conv_qkv_w[l]

    def prep_all(q_arr, g_arr, tt, tag):
        qh = _prep(q_arr, cw_q, mode="q", col0=0, nheads=NK, tt=tt, name="prep_q" + tag)
        kh = _prep(q_arr, cw_q, mode="k", col0=qk_dim // 512, nheads=NK, tt=tt, name="prep_k" + tag)
        vh = _prep(q_arr, cw_q, mode="v", col0=2 * qk_dim // 512, nheads=NV, tt=tt, name="prep_v" + tag)
        gc, gr = _gates(g_arr, na_l, dt_l, tt=tt)
        s_len = q_arr.shape[1]
        gr = gr.reshape(bsz, 4, NK, 2, s_len).transpose(0, 1, 3, 2, 4)
        return qh, kh, vh, gc, gr.reshape(bsz, LANES, s_len // CHUNK, CHUNK)

    pc = prep_all(cbig, cgts, lctx, "_c")
    px = prep_all(big, gts, min(512, seq), "_x")
    os_ = []
    for dd in range(2):
        s0 = jnp.zeros((bsz, NV, DK, DV), F32)
        (sc_state,) = _gdn(*pc, s0, reverse=bool(dd), tb=lctx, write_out=False, name=f"gdn_c{dd}")
        o, _ = _gdn(*px, sc_state, reverse=bool(dd), tb=min(512, seq), write_out=True, name=f"gdn_x{dd}")
        os_.append(o)

    ya = _aout(ys[0], ys[1], ag, w_a, tm=min(512, seq), tn=2 * tn)
    m4 = _bout(os_[0], os_[1], zs, mg, gdn_norm_w[l][None], ya, w_b, z_col=0, tn=tn)
    return _ffn(m4, x, gt1, w_o, ln2_w[l][None], sh2, sc2, gt2, w_f1, w_f2, final_norm_w[None], tf=512)
```

```python
import functools

import jax
import jax.numpy as jnp
import numpy as np
from jax import lax
from jax.experimental import pallas as pl
from jax.experimental.pallas import tpu as pltpu

F32 = jnp.float32
BF16 = jnp.bfloat16
HIGHEST = lax.Precision.HIGHEST

EPS = 1e-6
GRID_W = 64
CONV_W = 4
N_MOD = 6
NH_A = 16
LRU_C = 8.0
NK, DK, NV, DV = 16, 128, 32, 128
CHUNK = 64
LANES = 128
SUBLANES = 8
HALO = 16
PAIR_W = 384
STRIP = SUBLANES * GRID_W
TILE_R = 64
VMEM_LIMIT = 63 * 1024 * 1024


def _cparams(sem):
    return pltpu.CompilerParams(dimension_semantics=sem, vmem_limit_bytes=VMEM_LIMIT)


def _nt_dot(a, b, **kw):
    return lax.dot_general(a, b, (((1,), (1,)), ((), ())), preferred_element_type=F32, **kw)


def _tn_dot(a, b):
    return lax.dot_general(a, b, (((0,), (0,)), ((), ())), preferred_element_type=F32)


def _bdot(a, b):
    return jnp.dot(a.astype(BF16), b.astype(BF16), preferred_element_type=F32)


def _sigmoid(x):
    return 0.5 * jnp.tanh(0.5 * x) + 0.5


def _mod_kernel(c_ref, w_ref, b_ref, o_ref):
    s = c_ref[...]
    s = s * _sigmoid(s)
    o_ref[...] = jnp.dot(s, w_ref[...], precision=HIGHEST, preferred_element_type=F32) + b_ref[...]


def _mod(cc, w, b, tn=1024):
    m, d = cc.shape
    n = w.shape[1]
    return pl.pallas_call(
        _mod_kernel, grid=(n // tn,),
        in_specs=[pl.BlockSpec((m, d), lambda j: (0, 0)),
                  pl.BlockSpec((d, tn), lambda j: (0, j)),
                  pl.BlockSpec((1, tn), lambda j: (0, j))],
        out_specs=pl.BlockSpec((m, tn), lambda j: (0, j)),
        out_shape=jax.ShapeDtypeStruct((m, n), F32),
        compiler_params=_cparams(("parallel",)), name="mod")(cc, w, b)


def _rms_mod(xb, lnw, sc1, sh):
    xb = xb.astype(F32)
    ms = jnp.mean(xb * xb, axis=-1, keepdims=True)
    return (xb * lax.rsqrt(ms + EPS) * lnw) * sc1 + sh


def _strip_perm(tt):
    seg = tt // SUBLANES
    i = np.arange(tt)
    p = np.zeros((tt, tt), np.float32)
    p[i, (i % SUBLANES) * seg + i // SUBLANES] = 1.0
    return jnp.asarray(p, BF16)


def _colmajor_perm(rows):
    n = SUBLANES * rows
    i = np.arange(n)
    p = np.zeros((n, n), np.float32)
    p[i, (i % rows) * SUBLANES + i // rows] = 1.0
    return jnp.asarray(p, BF16)


def _inproj_kernel(*refs, order, tiles, acts, tt):
    nseg = len(tiles)
    x_ref, lnw_ref, sh_ref, sc_ref = refs[:4]
    pos = 4 if order == "raster" else 5
    w_refs = refs[pos:pos + nseg]
    o_refs = refs[pos + nseg:pos + 2 * nseg]
    scr = refs[pos + 2 * nseg:]
    h_scr = scr[0]
    tm, d = h_scr.shape
    j = pl.program_id(2)

    @pl.when(j == 0)
    def _():
        lnw = lnw_ref[...]
        sc1 = 1.0 + sc_ref[...]
        sh = sh_ref[...]
        dst = h_scr if order == "raster" else scr[1]
        ch = min(tm, 256)
        for r0 in range(0, tm, ch):
            if order == "colmajor":
                xb = x_ref[r0 // SUBLANES:(r0 + ch) // SUBLANES].reshape(ch, d)
            else:
                xb = x_ref[r0:r0 + ch, :]
            dst[r0:r0 + ch, :] = _rms_mod(xb, lnw, sc1, sh).astype(dst.dtype)
        if order != "raster":
            perm_ref = refs[4]
            cp = min(ch, tt)
            for r0 in range(0, tm, tt):
                for q0 in range(0, tt, cp):
                    h_scr[r0 + q0:r0 + q0 + cp, :] = jnp.dot(
                        perm_ref[q0:q0 + cp, :], dst[r0:r0 + tt, :], preferred_element_type=F32).astype(h_scr.dtype)

    lo = 0
    for k in range(nseg):
        @pl.when((j >= lo) & (j < lo + tiles[k]))
        def _(k=k):
            y = jnp.dot(h_scr[...], w_refs[k][...], preferred_element_type=F32)
            if acts[k] is not None:
                y = acts[k](y)
            o_refs[k][...] = y.astype(o_refs[k].dtype)
        lo += tiles[k]


def _inproj(x, lnw, sh, sc, segs, *, order, tm=None, tt=None, name="inproj"):
    b, s, d = x.shape
    ins = [None, lnw, sh, sc]
    if order == "colmajor":
        rows = s // GRID_W
        tm = SUBLANES * rows
        ins[0] = x.reshape(b, rows, GRID_W, d)
        x_spec = pl.BlockSpec((None, rows, SUBLANES, d), lambda bb, i, j: (bb, 0, i, 0))
    else:
        ins[0] = x
        x_spec = pl.BlockSpec((None, tm, d), lambda bb, i, j: (bb, i, 0))
    in_specs = [x_spec,
                pl.BlockSpec((1, d), lambda bb, i, j: (0, 0)),
                pl.BlockSpec((None, 1, d), lambda bb, i, j: (bb, 0, 0)),
                pl.BlockSpec((None, 1, d), lambda bb, i, j: (bb, 0, 0))]
    scratch = [pltpu.VMEM((tm, d), BF16)]
    if order != "raster":
        if order == "colmajor":
            tt = tm
        ins.append(_strip_perm(tt) if order == "strip" else _colmajor_perm(s // GRID_W))
        in_specs.append(pl.BlockSpec((tt, tt), lambda bb, i, j: (0, 0), pipeline_mode=pl.Buffered(1)))
        scratch.append(pltpu.VMEM((tm, d), BF16))
    tiles = tuple(sg[2] for sg in segs)
    acts = tuple(sg[4] if len(sg) > 4 else None for sg in segs)
    out_specs, out_shape = [], []
    lo = 0
    for w, tn, nt, dt in (sg[:4] for sg in segs):
        clip = functools.partial(lambda j, lo, nt: jnp.clip(j - lo, 0, nt - 1), lo=lo, nt=nt)
        ins.append(w)
        in_specs.append(pl.BlockSpec((d, tn), lambda bb, i, j, clip=clip: (0, clip(j))))
        out_specs.append(pl.BlockSpec((None, tm, tn), lambda bb, i, j, clip=clip: (bb, i, clip(j))))
        out_shape.append(jax.ShapeDtypeStruct((b, s, tn * nt), dt))
        lo += nt
    return pl.pallas_call(
        functools.partial(_inproj_kernel, order=order, tiles=tiles, acts=acts, tt=tt),
        grid=(b, s // tm, lo), in_specs=in_specs, out_specs=out_specs, out_shape=out_shape,
        scratch_shapes=scratch,
        compiler_params=_cparams(("parallel", "parallel", "arbitrary")), name=name)(*ins)


def _lru_kernel(xm_ref, xp_ref, xn_ref, cw_ref, cb_ref, wg_ref, bg_ref, nla_ref, h0_ref,
                y_ref, ht_ref, pad, a_scr, b_scr, carry, *, nt, tt, reverse):
    i = pl.program_id(2)
    tb = nt - 1 - i if reverse else i
    seg = tt // SUBLANES
    gm = min(tt, 256)

    @pl.when(i == 0)
    def _():
        carry[...] = h0_ref[...]

    sub = lax.broadcasted_iota(jnp.int32, (SUBLANES, PAIR_W), 0)
    first, last = tb == 0, tb == nt - 1
    prev_last = jnp.where(first, 0.0, xp_ref[SUBLANES:2 * SUBLANES, :].astype(F32))
    nxt0 = jnp.where(last, 0.0, xn_ref[0:SUBLANES, :].astype(F32))
    nxt1 = jnp.where(last, 0.0, xn_ref[SUBLANES:2 * SUBLANES, :].astype(F32))
    x_end = xm_ref[tt - SUBLANES:tt, :].astype(F32)
    x_0 = xm_ref[0:SUBLANES, :].astype(F32)
    x_1 = xm_ref[SUBLANES:2 * SUBLANES, :].astype(F32)
    up = SUBLANES - 1
    pad[0:SUBLANES, :] = jnp.where(sub == 0, pltpu.roll(prev_last, 1, 0), pltpu.roll(x_end, 1, 0))
    pad[SUBLANES:SUBLANES + tt, :] = xm_ref[...].astype(F32)
    pad[SUBLANES + tt:2 * SUBLANES + tt, :] = jnp.where(sub == up, pltpu.roll(nxt0, up, 0), pltpu.roll(x_0, up, 0))
    pad[2 * SUBLANES + tt:3 * SUBLANES + tt, :] = jnp.where(sub == up, pltpu.roll(nxt1, up, 0), pltpu.roll(x_1, up, 0))

    for r0 in range(0, tt, gm):
        xc = cb_ref[...] + cw_ref[0:1, :] * pad[r0:r0 + gm, :]
        for k in range(1, CONV_W):
            xc = xc + cw_ref[k:k + 1, :] * pad[SUBLANES * k + r0:SUBLANES * k + r0 + gm, :]
        gates = _bdot(xc, wg_ref[...]) + bg_ref[...]
        r = _sigmoid(gates[:, :PAIR_W])
        ig = _sigmoid(gates[:, PAIR_W:])
        la = nla_ref[...] * r
        a = jnp.exp(la)
        a_scr[r0:r0 + gm, :] = a
        b_scr[r0:r0 + gm, :] = jnp.sqrt(-jnp.tanh(la) * (a * a + 1.0)) * (ig * xc)

    def scan_body(jj, hp):
        j = seg - 1 - jj if reverse else jj
        rows = pl.ds(pl.multiple_of(j * SUBLANES, SUBLANES), SUBLANES)
        a = a_scr[rows, :]
        h = a * hp[0] + b_scr[rows, :]
        p = a * hp[1]
        b_scr[rows, :] = h
        a_scr[rows, :] = p
        return h, p

    h, p = lax.fori_loop(0, seg, scan_body,
                         (jnp.zeros((SUBLANES, PAIR_W), F32), jnp.ones((SUBLANES, PAIR_W), F32)), unroll=8)

    cin = carry[...]
    cvec = jnp.zeros((SUBLANES, PAIR_W), F32)
    for s in (range(SUBLANES - 1, -1, -1) if reverse else range(SUBLANES)):
        cvec = jnp.where(sub == s, jnp.broadcast_to(cin, (SUBLANES, PAIR_W)), cvec)
        cin = p[s:s + 1, :] * cin + h[s:s + 1, :]
    carry[...] = cin
    cvec2 = jnp.concatenate([cvec, cvec], axis=0)

    def fix_body(j, _):
        rows = pl.ds(pl.multiple_of(j * 2 * SUBLANES, 2 * SUBLANES), 2 * SUBLANES)
        y_ref[rows, :] = (b_scr[rows, :] + a_scr[rows, :] * cvec2).astype(y_ref.dtype)
        return 0

    lax.fori_loop(0, seg // 2, fix_body, 0, unroll=4)

    @pl.when(i == nt - 1)
    def _():
        ht_ref[...] = carry[...]


def _lru(ax, cw, cb, wg, bg, nla, h0, *, reverse, tt, name):
    b, s, _ = ax.shape
    npair = cw.shape[1] // PAIR_W
    nt = s // tt
    nh = tt // HALO
    last_blk = s // HALO - 1
    tb_of = (lambda i: nt - 1 - i) if reverse else (lambda i: i)
    par = lambda bb, p, i: (0, p)
    return pl.pallas_call(
        functools.partial(_lru_kernel, nt=nt, tt=tt, reverse=reverse),
        grid=(b, npair, nt),
        in_specs=[
            pl.BlockSpec((None, tt, PAIR_W), lambda bb, p, i: (bb, tb_of(i), p)),
            pl.BlockSpec((None, HALO, PAIR_W), lambda bb, p, i: (bb, jnp.maximum(tb_of(i) * nh - 1, 0), p)),
            pl.BlockSpec((None, HALO, PAIR_W), lambda bb, p, i: (bb, jnp.minimum((tb_of(i) + 1) * nh, last_blk), p)),
            pl.BlockSpec((CONV_W, PAIR_W), par),
            pl.BlockSpec((1, PAIR_W), par),
            pl.BlockSpec((None, PAIR_W, 2 * PAIR_W), lambda bb, p, i: (p, 0, 0)),
            pl.BlockSpec((None, 1, 2 * PAIR_W), lambda bb, p, i: (p, 0, 0)),
            pl.BlockSpec((1, PAIR_W), par),
            pl.BlockSpec((None, 1, PAIR_W), lambda bb, p, i: (bb, 0, p)),
        ],
        out_specs=[pl.BlockSpec((None, tt, PAIR_W), lambda bb, p, i: (bb, tb_of(i), p)),
                   pl.BlockSpec((None, 1, PAIR_W), lambda bb, p, i: (bb, 0, p))],
        out_shape=[jax.ShapeDtypeStruct((b, s, npair * PAIR_W), BF16),
                   jax.ShapeDtypeStruct((b, 1, npair * PAIR_W), F32)],
        scratch_shapes=[pltpu.VMEM((tt + 3 * SUBLANES, PAIR_W), F32),
                        pltpu.VMEM((tt, PAIR_W), F32),
                        pltpu.VMEM((tt, PAIR_W), F32),
                        pltpu.VMEM((1, PAIR_W), F32)],
        compiler_params=_cparams(("parallel", "parallel", "arbitrary")), name=name,
    )(ax, ax, ax, cw, cb, wg, bg, nla, h0)


def _prep_kernel(xm_ref, xp_ref, xn_ref, cw_ref, o_ref, pad, *, nt, tt, mode):
    i = pl.program_id(1)
    pad[0:HALO, :] = jnp.where(i == 0, 0.0, xp_ref[...].astype(F32))
    pad[HALO:HALO + tt, :] = xm_ref[...].astype(F32)
    pad[HALO + tt:2 * HALO + tt, :] = jnp.where(i == nt - 1, 0.0, xn_ref[...].astype(F32))
    y = cw_ref[0:1, :] * pad[HALO - 1:HALO - 1 + tt, :]
    for k in range(1, CONV_W):
        y = y + cw_ref[k:k + 1, :] * pad[HALO - 1 + k:HALO - 1 + k + tt, :]
    y = y * _sigmoid(y)
    for h in range(o_ref.shape[0]):
        yh = y[:, h * LANES:(h + 1) * LANES]
        if mode != "v":
            yh = yh * lax.rsqrt(jnp.sum(yh * yh, axis=-1, keepdims=True) + EPS)
        if mode == "q":
            yh = yh * (DK ** -0.5)
        o_ref[h] = yh.astype(o_ref.dtype)


def _prep(q_arr, cw, *, mode, col0, nheads, tt, name):
    b, s, _ = q_arr.shape
    nh = 4
    width = nh * LANES
    nt = s // tt
    nhalo = tt // HALO
    last_blk = s // HALO - 1
    return pl.pallas_call(
        functools.partial(_prep_kernel, nt=nt, tt=tt, mode=mode),
        grid=(b, nt, nheads // nh),
        in_specs=[pl.BlockSpec((None, tt, width), lambda bb, i, j: (bb, i, col0 + j)),
                  pl.BlockSpec((None, HALO, width), lambda bb, i, j: (bb, jnp.maximum(i * nhalo - 1, 0), col0 + j)),
                  pl.BlockSpec((None, HALO, width), lambda bb, i, j: (bb, jnp.minimum((i + 1) * nhalo, last_blk), col0 + j)),
                  pl.BlockSpec((CONV_W, width), lambda bb, i, j: (0, col0 + j))],
        out_specs=pl.BlockSpec((None, nh, tt, LANES), lambda bb, i, j: (bb, j, i, 0)),
        out_shape=jax.ShapeDtypeStruct((b, nheads, s, LANES), BF16),
        scratch_shapes=[pltpu.VMEM((tt + 2 * HALO, width), F32)],
        compiler_params=_cparams(("parallel", "parallel", "parallel")), name=name,
    )(q_arr, q_arr, q_arr, cw)


def _gates_kernel(x_ref, na_ref, dt_ref, oc_ref, or_ref):
    x = x_ref[...].astype(F32)
    lane = lax.broadcasted_iota(jnp.int32, x.shape, 1)
    z = x + dt_ref[...]
    sp = jnp.maximum(z, 0.0) + jnp.log1p(jnp.exp(-jnp.abs(z)))
    val = jnp.where(lane < 2 * NV, _sigmoid(x), na_ref[...] * sp)
    oc_ref[...] = val
    or_ref[...] = val.T


def _gates(g_arr, na, dtb, *, tt):
    b, s, _ = g_arr.shape
    return pl.pallas_call(
        _gates_kernel, grid=(b, s // tt),
        in_specs=[pl.BlockSpec((None, tt, LANES), lambda bb, i: (bb, i, 0)),
                  pl.BlockSpec((1, LANES), lambda bb, i: (0, 0)),
                  pl.BlockSpec((1, LANES), lambda bb, i: (0, 0))],
        out_specs=[pl.BlockSpec((None, tt, LANES), lambda bb, i: (bb, i, 0)),
                   pl.BlockSpec((None, LANES, tt), lambda bb, i: (bb, 0, i))],
        out_shape=[jax.ShapeDtypeStruct((b, s, LANES), F32), jax.ShapeDtypeStruct((b, LANES, s), F32)],
        compiler_params=_cparams(("parallel", "parallel")), name="gdn_gates")(g_arr, na, dtb)


def _gdn_kernel(q_ref, k_ref, v_ref, gc_ref, gr_ref, s0_ref, *rest, nt, tb, reverse, write_out, wave):
    if write_out:
        o_ref, st_ref, st, sol_scr, cs_scr, a_scr = rest
    else:
        st_ref, st, sol_scr, cs_scr = rest
    i = pl.program_id(1)
    nch = tb // CHUNK
    nkh = q_ref.shape[0]
    lane_b = NV if reverse else 0
    lane_g = 2 * NV + lane_b

    @pl.when(i == 0)
    def _():
        st[...] = s0_ref[...]

    c2 = 2 * CHUNK
    row = lax.broadcasted_iota(jnp.int32, (CHUNK, c2), 0)
    lane = lax.broadcasted_iota(jnp.int32, (CHUNK, c2), 1)
    colm = lane % CHUNK
    left = lane < CHUNK
    if reverse:
        incl, strict, tl = row <= colm, row < colm, 0
    else:
        incl, strict, tl = row >= colm, row > colm, CHUNK - 1
    eye2 = (row == colm).astype(F32)
    r64 = lax.broadcasted_iota(jnp.int32, (CHUNK, CHUNK), 0)
    c64 = lax.broadcasted_iota(jnp.int32, (CHUNK, CHUNK), 1)
    tri = (r64 <= c64 if reverse else r64 >= c64).astype(F32)
    tri_t = (c64 <= r64 if reverse else c64 >= r64).astype(F32)
    z64 = jnp.zeros((CHUNK, CHUNK), F32)
    tri_t_l = jnp.concatenate([tri_t, z64], axis=1)
    tri_t_r = jnp.concatenate([z64, tri_t], axis=1)
    r128 = lax.broadcasted_iota(jnp.int32, (c2, c2), 0)
    l128 = lax.broadcasted_iota(jnp.int32, (c2, c2), 1)
    diag_blocks = (r128 < CHUNK) == (l128 < CHUNK)

    def split_stack(x2):
        return jnp.where(diag_blocks, jnp.concatenate([x2, x2], axis=0), 0.0).astype(BF16)

    def chunk_rows(c):
        return pl.ds(pl.multiple_of(c * CHUNK, CHUNK), CHUNK)

    def head(c, slot):
        rows = chunk_rows(c)
        gcol = gc_ref[rows, :]
        cs_col = jnp.dot(tri, gcol, precision=HIGHEST, preferred_element_type=F32)
        cs_scr[slot] = cs_col
        grow = gr_ref[:, pl.ds(c, 1), :].reshape(2 * nkh, CHUNK)
        gc_row2 = (jnp.dot(grow[:nkh], tri_t_l, precision=HIGHEST, preferred_element_type=F32)
                   + jnp.dot(grow[nkh:], tri_t_r, precision=HIGHEST, preferred_element_type=F32))

        for p0 in range(0, nkh, wave):
            prs = range(p0, min(p0 + wave, nkh))
            ks, pp, tt_, cols = {}, {}, {}, {}
            for p in prs:
                k = k_ref[p, rows, :]
                q = q_ref[p, rows, :]
                kq = _nt_dot(jnp.concatenate([k, q], axis=0), jnp.concatenate([k, k], axis=0))
                ha, hb = 2 * p, 2 * p + 1
                gca, gcb = cs_col[:, lane_g + ha:lane_g + ha + 1], cs_col[:, lane_g + hb:lane_g + hb + 1]
                bca, bcb = gcol[:, lane_b + ha:lane_b + ha + 1], gcol[:, lane_b + hb:lane_b + hb + 1]
                gcc2 = jnp.where(left, gca, gcb)
                bc2 = jnp.where(left, bca, bcb)
                decay2 = jnp.where(incl, jnp.exp(jnp.where(incl, gcc2 - gc_row2[p:p + 1, :], 0.0)), 0.0)
                pn = jnp.where(strict, -(kq[:CHUNK] * bc2) * decay2, 0.0)
                ks[p] = k.astype(F32)
                pp[p] = pn
                tt_[p] = eye2 + pn
                if write_out:
                    a_scr[slot, p] = split_stack(kq[CHUNK:] * decay2)
                cols[p] = ((gca, bca), (gcb, bcb))
            yield

            for p in prs:
                pp[p] = jnp.dot(pp[p].astype(BF16), split_stack(pp[p]), preferred_element_type=F32)
            yield
            for _ in range(4):
                for p in prs:
                    y = jnp.dot(jnp.concatenate([pp[p], tt_[p]], axis=0).astype(BF16), split_stack(pp[p]),
                                preferred_element_type=F32)
                    pp[p] = y[:CHUNK]
                    tt_[p] = tt_[p] + y[CHUNK:]
                yield
            for p in prs:
                tt_[p] = tt_[p] + jnp.dot(tt_[p].astype(BF16), split_stack(pp[p]), preferred_element_type=F32)
            yield

            for p in prs:
                rhs = []
                for e in range(2):
                    gch, bch = cols[p][e]
                    vf = v_ref[2 * p + e, rows, :].astype(F32)
                    rhs.append(jnp.concatenate([vf * bch, ks[p] * (bch * jnp.exp(gch))], axis=1))
                sol_scr[slot, p] = jnp.dot(split_stack(tt_[p]), jnp.concatenate(rhs, axis=0).astype(BF16),
                                           preferred_element_type=F32)

    def tail(c, slot):
        rows = chunk_rows(c)
        cs_col = cs_scr[slot]
        wss, vns, gcs = {}, {}, {}
        for p in range(nkh):
            qf = q_ref[p, rows, :].astype(F32)
            for e in range(2):
                m = 2 * p + e
                gcs[m] = cs_col[:, lane_g + m:lane_g + m + 1]
                sol = sol_scr[slot, p, e * CHUNK:(e + 1) * CHUNK, :]
                lhs = jnp.concatenate([sol[:, DV:], qf * jnp.exp(gcs[m])], axis=0)
                wss[m] = _bdot(lhs, st[m])
                vns[m] = sol[:, :DV] - wss[m][:CHUNK]
            if p % 4 == 3:
                yield

        for p in range(nkh):
            kf = k_ref[p, rows, :].astype(F32)
            if write_out:
                oi = jnp.dot(a_scr[slot, p], jnp.concatenate([vns[2 * p], vns[2 * p + 1]], axis=0).astype(BF16),
                             preferred_element_type=F32)
            for e in range(2):
                m = 2 * p + e
                if write_out:
                    o_ref[m, rows, :] = (wss[m][CHUNK:] + oi[e * CHUNK:(e + 1) * CHUNK]).astype(o_ref.dtype)
                gl = gcs[m][tl:tl + 1, :]
                kd = kf * jnp.exp(gl - gcs[m])
                st[m] = st[m] * jnp.exp(gl) + _tn_dot(kd.astype(BF16), vns[m].astype(BF16))
            if p % 4 == 3:
                yield

    def run(*gens):
        live = list(gens)
        while live:
            for g in list(live):
                try:
                    next(g)
                except StopIteration:
                    live.remove(g)

    def body(cc, carry):
        c = nch - 1 - cc if reverse else cc
        run(head(c, 0))
        run(tail(c, 0))
        return carry

    lax.fori_loop(0, nch, body, 0)

    @pl.when(i == nt - 1)
    def _():
        st_ref[...] = st[...]


def _gdn(qh, kh, vh, gc, gr, s0, *, reverse, tb, write_out, name, wave=16):
    b, nk, s, _ = qh.shape
    nv = vh.shape[1]
    nt = s // tb
    nch = tb // CHUNK
    d = 1 if reverse else 0
    tb_of = (lambda i: nt - 1 - i) if reverse else (lambda i: i)
    in_specs = [
        pl.BlockSpec((None, nk, tb, DK), lambda bb, i: (bb, 0, tb_of(i), 0)),
        pl.BlockSpec((None, nk, tb, DK), lambda bb, i: (bb, 0, tb_of(i), 0)),
        pl.BlockSpec((None, nv, tb, DV), lambda bb, i: (bb, 0, tb_of(i), 0)),
        pl.BlockSpec((None, tb, LANES), lambda bb, i: (bb, tb_of(i), 0)),
        pl.BlockSpec((None, nv, nch, CHUNK), lambda bb, i: (bb, 2 + d, tb_of(i), 0)),
        pl.BlockSpec((None, nv, DK, DV), lambda bb, i: (bb, 0, 0, 0)),
    ]
    st_spec = pl.BlockSpec((None, nv, DK, DV), lambda bb, i: (bb, 0, 0, 0))
    st_shape = jax.ShapeDtypeStruct((b, nv, DK, DV), F32)
    if write_out:
        out_specs = [pl.BlockSpec((None, nv, tb, DV), lambda bb, i: (bb, 0, tb_of(i), 0)), st_spec]
        out_shape = [jax.ShapeDtypeStruct((b, nv, s, DV), BF16), st_shape]
    else:
        out_specs, out_shape = [st_spec], [st_shape]
    scratch = [pltpu.VMEM((nv, DK, DV), F32),
               pltpu.VMEM((1, nk, 2 * CHUNK, DV + DK), F32),
               pltpu.VMEM((1, CHUNK, LANES), F32)]
    if write_out:
        scratch.append(pltpu.VMEM((1, nk, 2 * CHUNK, 2 * CHUNK), BF16))
    return pl.pallas_call(
        functools.partial(_gdn_kernel, nt=nt, tb=tb, reverse=reverse, write_out=write_out, wave=wave),
        grid=(b, nt), in_specs=in_specs, out_specs=out_specs, out_shape=out_shape,
        scratch_shapes=scratch,
        compiler_params=_cparams(("parallel", "arbitrary")), name=name,
    )(qh, kh, vh, gc, gr, s0)


def _gelu_tanh(x):
    return 0.5 * x * (1.0 + jnp.tanh(0.7978845608028654 * (x + 0.044715 * (x * x * x))))


def _aout_kernel(yf_ref, yb_ref, ag_ref, w_ref, o_ref, a_scr):
    @pl.when(pl.program_id(2) == 0)
    def _():
        y = yf_ref[...].astype(F32) + yb_ref[...].astype(F32)
        a_scr[...] = (y * ag_ref[...].astype(F32)).astype(a_scr.dtype)

    o_ref[...] = jnp.dot(a_scr[...], w_ref[...], preferred_element_type=F32)


def _aout(yf, yb, ag, w, *, tm, tn):
    b, s, wa = yf.shape
    n = w.shape[1]
    act = pl.BlockSpec((None, tm, wa), lambda bb, i, j: (bb, i, 0))
    return pl.pallas_call(
        _aout_kernel, grid=(b, s // tm, n // tn),
        in_specs=[act, act, act, pl.BlockSpec((wa, tn), lambda bb, i, j: (0, j))],
        out_specs=pl.BlockSpec((None, tm, tn), lambda bb, i, j: (bb, i, j)),
        out_shape=jax.ShapeDtypeStruct((b, s, n), F32),
        scratch_shapes=[pltpu.VMEM((tm, wa), BF16)],
        compiler_params=_cparams(("parallel", "parallel", "arbitrary")), name="aout")(yf, yb, ag, w)


def _bout_kernel(of_ref, ob_ref, zl_ref, zh_ref, nw_ref, ya_ref, ga_ref, gb_ref, w_ref, o_ref, y_scr):
    ncol, tr = o_ref.shape[0], o_ref.shape[1]
    tm = ncol * tr

    @pl.when(pl.program_id(3) == 0)
    def _():
        nw = nw_ref[...]
        nvh = of_ref.shape[0]
        half = nvh // 2
        avg = jnp.full((DV, DV), 1.0 / DV, BF16)
        for h in range(nvh):
            o = (of_ref[h].astype(F32) + ob_ref[h].astype(F32)).reshape(tm, DV)
            z_ref = zl_ref if h < half else zh_ref
            hh = h % half
            z = z_ref[:, :, hh * DV:(hh + 1) * DV].astype(F32).reshape(tm, DV)
            ms = jnp.dot((o * o).astype(BF16), avg, preferred_element_type=F32)
            y = o * lax.rsqrt(ms + EPS)
            y = y * nw * z
            y_scr[:, h * DV:(h + 1) * DV] = y.astype(y_scr.dtype)

    yb = jnp.dot(y_scr[...], w_ref[...], preferred_element_type=F32)
    for k in range(ncol):
        ga = ga_ref[k].astype(F32)
        gb = gb_ref[k].astype(F32)
        ya = ya_ref[:, k].reshape(tr, ya_ref.shape[-1])
        o_ref[k] = (ga * ya + gb * yb[k * tr:(k + 1) * tr]).astype(o_ref.dtype)


def _bout(of, ob, big, mg, nw, ya, w, *, z_col, tn):
    b, nv, s, dv = of.shape
    n = w.shape[1]
    rows = s // GRID_W
    tr = min(TILE_R, rows)
    hw = nv * dv // 2
    of5 = of.reshape(b, nv, GRID_W, rows, dv)
    ob5 = ob.reshape(b, nv, GRID_W, rows, dv)
    big4 = big.reshape(b, GRID_W, rows, big.shape[-1])
    mg4 = mg.reshape(b, GRID_W, rows, mg.shape[-1])
    ya5 = ya.reshape(b, rows // SUBLANES, GRID_W, SUBLANES, n)
    o_spec = pl.BlockSpec((None, nv, SUBLANES, tr, dv), lambda bb, wi, ri, j: (bb, 0, wi, ri, 0))
    out = pl.pallas_call(
        _bout_kernel, grid=(b, GRID_W // SUBLANES, rows // tr, n // tn),
        in_specs=[o_spec, o_spec,
                  pl.BlockSpec((None, SUBLANES, tr, hw), lambda bb, wi, ri, j: (bb, wi, ri, z_col // hw)),
                  pl.BlockSpec((None, SUBLANES, tr, hw), lambda bb, wi, ri, j: (bb, wi, ri, z_col // hw + 1)),
                  pl.BlockSpec((1, dv), lambda bb, wi, ri, j: (0, 0)),
                  pl.BlockSpec((None, tr // SUBLANES, SUBLANES, SUBLANES, tn), lambda bb, wi, ri, j: (bb, ri, wi, 0, j)),
                  pl.BlockSpec((None, SUBLANES, tr, tn), lambda bb, wi, ri, j: (bb, wi, ri, j)),
                  pl.BlockSpec((None, SUBLANES, tr, tn), lambda bb, wi, ri, j: (bb, wi, ri, n // tn + j)),
                  pl.BlockSpec((nv * dv, tn), lambda bb, wi, ri, j: (0, j))],
        out_specs=pl.BlockSpec((None, SUBLANES, tr, tn), lambda bb, wi, ri, j: (bb, wi, ri, j)),
        out_shape=jax.ShapeDtypeStruct((b, GRID_W, rows, n), BF16),
        scratch_shapes=[pltpu.VMEM((SUBLANES * tr, nv * dv), BF16)],
        compiler_params=_cparams(("parallel", "parallel", "parallel", "arbitrary")), name="bout",
    )(of5, ob5, big4, big4, nw, ya5, mg4, mg4, w)
    return out


def _ffn_up_kernel(m_ref, x_ref, perm_ref, gt1_ref, wo_ref, lnw_ref, sh_ref, sc_ref, wg_ref, wu_ref,
                   x1_ref, a_ref, h_scr):
    ncol = m_ref.shape[0]
    tm, d = h_scr.shape
    ch = min(tm, 256)

    @pl.when(pl.program_id(3) == 0)
    def _():
        mr = jnp.dot(perm_ref[...], m_ref[...].reshape(tm, d), preferred_element_type=F32).astype(BF16)
        x1_ref[...] = jnp.dot(mr, wo_ref[...], preferred_element_type=F32)
        for r0 in range(0, tm, ch):
            xb = x_ref[r0 // ncol:(r0 + ch) // ncol].reshape(ch, d)
            x1 = xb + gt1_ref[...] * x1_ref[r0:r0 + ch, :]
            x1_ref[r0:r0 + ch, :] = x1
            h_scr[r0:r0 + ch, :] = _rms_mod(x1, lnw_ref[...], 1.0 + sc_ref[...], sh_ref[...]).astype(h_scr.dtype)

    h = h_scr[...]
    g = jnp.dot(h, wg_ref[...], preferred_element_type=F32)
    u = jnp.dot(h, wu_ref[...], preferred_element_type=F32)
    a_ref[...] = (g * _sigmoid(g) * u).astype(a_ref.dtype)


def _ffn_down_kernel(a_ref, x1_ref, gt2_ref, wd_ref, fw_ref, o_ref):
    ncol = o_ref.shape[1]
    tm, d = x1_ref.shape
    x2 = x1_ref[...] + gt2_ref[...] * jnp.dot(a_ref[...], wd_ref[...], preferred_element_type=F32)
    ms = jnp.mean(x2 * x2, axis=-1, keepdims=True)
    o_ref[...] = (x2 * lax.rsqrt(ms + EPS) * fw_ref[...]).reshape(tm // ncol, ncol, d)


def _ffn(m4, x, gt1, w_o, lnw, sh, sc, gt2, w_in, w_dn, fw, *, tf):
    b, s, d = x.shape
    rows = s // GRID_W
    tr = min(TILE_R, rows)
    tm = SUBLANES * tr
    fh = w_dn.shape[0]
    nf = fh // tf
    ncb, nrb = GRID_W // SUBLANES, rows // tr
    x4 = x.reshape(b, rows, GRID_W, d)
    vec = pl.BlockSpec((None, 1, d), lambda bb, wi, ri, f: (bb, 0, 0))
    par = pl.BlockSpec((1, d), lambda bb, wi, ri, f: (0, 0))
    once = pl.Buffered(1)
    x1, a = pl.pallas_call(
        _ffn_up_kernel, grid=(b, ncb, nrb, nf),
        in_specs=[pl.BlockSpec((None, SUBLANES, tr, d), lambda bb, wi, ri, f: (bb, wi, ri, 0)),
                  pl.BlockSpec((None, tr, SUBLANES, d), lambda bb, wi, ri, f: (bb, ri, wi, 0)),
                  pl.BlockSpec((tm, tm), lambda bb, wi, ri, f: (0, 0)),
                  vec,
                  pl.BlockSpec((d, d), lambda bb, wi, ri, f: (0, 0), pipeline_mode=once),
                  par, vec, vec,
                  pl.BlockSpec((d, tf), lambda bb, wi, ri, f: (0, f)),
                  pl.BlockSpec((d, tf), lambda bb, wi, ri, f: (0, nf + f))],
        out_specs=[pl.BlockSpec((None, None, None, tm, d), lambda bb, wi, ri, f: (bb, wi, ri, 0, 0)),
                   pl.BlockSpec((None, None, None, tm, tf), lambda bb, wi, ri, f: (bb, wi, ri, 0, f))],
        out_shape=[jax.ShapeDtypeStruct((b, ncb, nrb, tm, d), F32), jax.ShapeDtypeStruct((b, ncb, nrb, tm, fh), BF16)],
        scratch_shapes=[pltpu.VMEM((tm, d), BF16)],
        compiler_params=_cparams(("parallel", "parallel", "parallel", "arbitrary")), name="ffn_up",
    )(m4, x4, _strip_perm(tm), gt1, w_o, lnw, sh, sc, w_in, w_in)
    vec3 = pl.BlockSpec((None, 1, d), lambda bb, wi, ri: (bb, 0, 0))
    out = pl.pallas_call(
        _ffn_down_kernel, grid=(b, ncb, nrb),
        in_specs=[pl.BlockSpec((None, None, None, tm, fh), lambda bb, wi, ri: (bb, wi, ri, 0, 0)),
                  pl.BlockSpec((None, None, None, tm, d), lambda bb, wi, ri: (bb, wi, ri, 0, 0)),
                  vec3,
                  pl.BlockSpec((fh, d), lambda bb, wi, ri: (0, 0), pipeline_mode=once),
                  pl.BlockSpec((1, d), lambda bb, wi, ri: (0, 0))],
        out_specs=pl.BlockSpec((None, tr, SUBLANES, d), lambda bb, wi, ri: (bb, ri, wi, 0)),
        out_shape=jax.ShapeDtypeStruct((b, rows, GRID_W, d), F32),
        compiler_params=_cparams(("parallel", "parallel", "parallel")), name="ffn_down",
    )(a, x1, gt2, w_dn, fw)
    return out.reshape(b, s, d)


def _pad_pairs(t, axis):
    n = t.shape[axis]
    bw = n // NH_A
    shp = t.shape[:axis] + (NH_A // 2, 2 * bw) + t.shape[axis + 1:]
    t = t.reshape(shp)
    padw = [(0, 0)] * t.ndim
    padw[axis + 1] = (0, PAIR_W - 2 * bw)
    t = jnp.pad(t, padw)
    return t.reshape(t.shape[:axis] + (NH_A // 2 * PAIR_W,) + t.shape[axis + 2:])


def _pair_gate_weights(gw, gb):
    bw = gw.shape[1]
    npair = NH_A // 2
    g = gw.reshape(npair, 2, bw, 2, bw)
    blocks = []
    for t in range(2):
        top = jnp.pad(g[:, 0, :, t, :], ((0, 0), (0, 0), (0, PAIR_W - bw)))
        bot = jnp.pad(g[:, 1, :, t, :], ((0, 0), (0, PAIR_W - 2 * bw), (bw, PAIR_W - 2 * bw)))
        blocks.append(jnp.concatenate([top, bot], axis=1))
    wp = jnp.concatenate(blocks, axis=2)
    b = gb.reshape(npair, 2, 2, bw).transpose(0, 2, 1, 3).reshape(npair, 2, 2 * bw)
    b = jnp.pad(b, ((0, 0), (0, 0), (0, PAIR_W - 2 * bw)))
    return wp.astype(BF16), b.reshape(npair, 1, 2 * PAIR_W)


def kernel(x, c, ctx, c_ctx, w_mod, b_mod, ln1_w, ln2_w, w_in, conv_a_w, conv_a_b, lru_gate_w, lru_gate_b, lru_lambda, conv_qkv_w, gdn_a_log, gdn_dt_bias, gdn_norm_w, w_a_out, w_b_out, w_out, w_ffn_in, w_ffn_out, final_norm_w):
    depth = w_mod.shape[0]
    assert depth == 1, "context stream updates are only needed for depth > 1"
    bsz, seq, d = x.shape
    lctx = ctx.shape[1]
    rows = seq // GRID_W
    assert seq % STRIP == 0 and rows % CHUNK == 0 and lctx % CHUNK == 0
    wa = lru_lambda.shape[-1]
    wap = NH_A // 2 * PAIR_W
    qk_dim, v_dim = NK * DK, NV * DV
    qkv_dim = 2 * qk_dim + v_dim
    l = 0
    tn = 512

    wi = w_in[l]
    o_qkv, o_z, o_g, o_mg = 2 * wa, 2 * wa + qkv_dim, 2 * wa + qkv_dim + v_dim, 2 * wa + qkv_dim + v_dim + 4 * NV
    w_axp = _pad_pairs(wi[:, :wa], 1).astype(BF16)
    w_agp = _pad_pairs(wi[:, wa:2 * wa], 1).astype(BF16)
    w_qkv = wi[:, o_qkv:o_z].astype(BF16)
    w_z = wi[:, o_z:o_g].astype(BF16)
    w_gt = wi[:, o_g:o_mg].astype(BF16)
    w_mg = wi[:, o_mg:].astype(BF16)
    cw_a = _pad_pairs(conv_a_w[l], 1)
    cb_a = _pad_pairs(conv_a_b[l][None], 1)
    nla = _pad_pairs(-LRU_C * jax.nn.softplus(-lru_lambda[l]), 1)
    gate_w = [_pair_gate_weights(lru_gate_w[l, dd], lru_gate_b[l, dd]) for dd in range(2)]
    neg_a = -jnp.exp(gdn_a_log[l]).reshape(1, 2 * NV)
    na_l = jnp.concatenate([jnp.zeros((1, 2 * NV), F32), neg_a], axis=1)
    dt_l = jnp.concatenate([jnp.zeros((1, 2 * NV), F32), gdn_dt_bias[l].reshape(1, 2 * NV)], axis=1)
    w_a = _pad_pairs(w_a_out[l], 0).astype(BF16)
    w_b = w_b_out[l].astype(BF16)
    w_o = w_out[l].astype(BF16)
    w_f1 = w_ffn_in[l].astype(BF16)
    w_f2 = w_ffn_out[l].astype(BF16)

    nrow = -(-(bsz + 1) // SUBLANES) * SUBLANES
    cc = jnp.concatenate([c, c_ctx[None], jnp.zeros((nrow - bsz - 1, d), F32)], axis=0)
    mod = _mod(cc, w_mod[l], b_mod[l][None]).reshape(nrow, N_MOD, 1, d)
    sh1, sc1, gt1, sh2, sc2, gt2 = (mod[:bsz, k] for k in range(N_MOD))
    csh1 = jnp.broadcast_to(mod[bsz, 0], (bsz, 1, d))
    csc1 = jnp.broadcast_to(mod[bsz, 1], (bsz, 1, d))
    ln1 = ln1_w[l][None]

    nax = wap // tn
    silu = lambda y: y * _sigmoid(y)
    ax, ag = _inproj(x, ln1, sh1, sc1, [(w_axp, 2 * tn, nax // 2, BF16), (w_agp, 2 * tn, nax // 2, BF16, _gelu_tanh)],
                     order="strip", tm=min(2 * STRIP, seq), tt=STRIP, name="inproj_a")
    big, zs, mg, gts = _inproj(x, ln1, sh1, sc1,
                               [(w_qkv, 2 * tn, qkv_dim // (2 * tn), BF16), (w_z, 2 * tn, v_dim // (2 * tn), BF16, silu),
                                (w_mg, tn, 2 * d // tn, BF16, _sigmoid), (w_gt, LANES, 1, F32)],
                               order="colmajor", name="inproj_b")
    ctx1 = ctx.reshape(1, bsz * lctx, d)
    (cax,) = _inproj(ctx1, ln1, csh1[:1], csc1[:1], [(w_axp, 2 * tn, nax // 2, BF16)], order="strip",
                     tm=bsz * lctx, tt=lctx, name="inproj_ca")
    cbig, cgts = _inproj(ctx1, ln1, csh1[:1], csc1[:1], [(w_qkv, 2 * tn, qkv_dim // (2 * tn), BF16), (w_gt, LANES, 1, F32)],
                         order="raster", tm=bsz * lctx, name="inproj_cb")
    cax, cbig, cgts = (t.reshape(bsz, lctx, t.shape[-1]) for t in (cax, cbig, cgts))

    ys = []
    for dd in range(2):
        wg, bg = gate_w[dd]
        h0 = jnp.zeros((bsz, 1, wap), F32)
        _, hc = _lru(cax, cw_a, cb_a, wg, bg, nla[dd:dd + 1], h0, reverse=bool(dd), tt=lctx, name=f"lru_c{dd}")
        y, _ = _lru(ax, cw_a, cb_a, wg, bg, nla[dd:dd + 1], hc, reverse=bool(dd), tt=STRIP, name=f"lru_x{dd}")
        ys.append(y)

    cw_q = conv_qkv_w[l]

    def prep_all(q_arr, g_arr, tt, tag):
        qh = _prep(q_arr, cw_q, mode="q", col0=0, nheads=NK, tt=tt, name="prep_q" + tag)
        kh = _prep(q_arr, cw_q, mode="k", col0=qk_dim // 512, nheads=NK, tt=tt, name="prep_k" + tag)
        vh = _prep(q_arr, cw_q, mode="v", col0=2 * qk_dim // 512, nheads=NV, tt=tt, name="prep_v" + tag)
        gc, gr = _gates(g_arr, na_l, dt_l, tt=tt)
        s_len = q_arr.shape[1]
        gr = gr.reshape(bsz, 4, NK, 2, s_len).transpose(0, 1, 3, 2, 4)
        return qh, kh, vh, gc, gr.reshape(bsz, LANES, s_len // CHUNK, CHUNK)

    pc = prep_all(cbig, cgts, lctx, "_c")
    px = prep_all(big, gts, min(512, seq), "_x")
    os_ = []
    for dd in range(2):
        s0 = jnp.zeros((bsz, NV, DK, DV), F32)
        (sc_state,) = _gdn(*pc, s0, reverse=bool(dd), tb=lctx, write_out=False, name=f"gdn_c{dd}")
        o, _ = _gdn(*px, sc_state, reverse=bool(dd), tb=min(512, seq), write_out=True, name=f"gdn_x{dd}")
        os_.append(o)

    ya = _aout(ys[0], ys[1], ag, w_a, tm=min(512, seq), tn=2 * tn)
    m4 = _bout(os_[0], os_[1], zs, mg, gdn_norm_w[l][None], ya, w_b, z_col=0, tn=2 * tn)
    return _ffn(m4, x, gt1, w_o, ln2_w[l][None], sh2, sc2, gt2, w_f1, w_f2, final_norm_w[None], tf=512)
```
